```python
import math
import jax, jax.numpy as jnp
from jax import lax
import numpy as np

D_MODEL = 2048
BATCH = 2
SEQ = 8192
DEPTH = 1

CHUNK = 64
Q_BLOCK = 128
MLA_HEADS = 8
MLA_NOPE_DIM = 128
MLA_ROPE_DIM = 64
MLA_V_DIM = 128
MLA_Q_RANK = 384
MLA_KV_RANK = 256
ROPE_THETA = 10000.0
DIFF_HEADS = 8
DIFF_HEAD_DIM = 64
DIFF_V_DIM = 2 * DIFF_HEAD_DIM
REL_BUCKETS = 32
REL_MAX_DIST = 128
N_EXPERTS = 32
TOP_K = 4
D_FF = 2048
SWIGLU_LIMIT = 7.0
SWIGLU_ALPHA = 1.702
MOE_BLOCK = 256
EPS = 1e-6

IN_SPLITS = (MLA_Q_RANK, MLA_KV_RANK, MLA_ROPE_DIM,
             DIFF_HEADS * 2 * DIFF_HEAD_DIM, DIFF_HEADS * 2 * DIFF_HEAD_DIM, DIFF_HEADS * DIFF_V_DIM)
IN_WIDTH = sum(IN_SPLITS)
SPLIT_POINTS = tuple(int(v) for v in np.cumsum(IN_SPLITS)[:-1])
MIX_WIDTH = MLA_HEADS * MLA_V_DIM + DIFF_HEADS * DIFF_V_DIM

kernel_name = "hybrid_mla_diffattn_moe_block"


def rms_norm(x, w):
    xf = x.astype(jnp.float32)
    y = xf * lax.rsqrt(jnp.mean(xf * xf, axis=-1, keepdims=True) + EPS)
    return (y * w.astype(jnp.float32)).astype(x.dtype)


def rope_tables(pos):
    inv_freq = ROPE_THETA ** (-jnp.arange(0, MLA_ROPE_DIM, 2, dtype=jnp.float32) / MLA_ROPE_DIM)
    ang = pos.astype(jnp.float32)[:, None] * inv_freq[None, :]
    return jnp.cos(ang), jnp.sin(ang)


def apply_rope(x, cos, sin):
    half = x.shape[-1] // 2
    x1, x2 = x[..., :half], x[..., half:]
    c = cos[None, :, None, :].astype(x.dtype)
    s = sin[None, :, None, :].astype(x.dtype)
    return jnp.concatenate([x1 * c - x2 * s, x2 * c + x1 * s], axis=-1)


def chunk_mask(qpos, kpos):
    return (kpos[None, :] // CHUNK) <= (qpos[:, None] // CHUNK)


def t5_bucket(rel):
    nb = REL_BUCKETS // 2
    max_exact = nb // 2
    ret = jnp.where(rel > 0, nb, 0)
    n = jnp.abs(rel)
    nf = jnp.maximum(n, 1).astype(jnp.float32)
    large = max_exact + (jnp.log(nf / max_exact) / math.log(REL_MAX_DIST / max_exact)
                         * (nb - max_exact)).astype(jnp.int32)
    large = jnp.minimum(large, nb - 1)
    return ret + jnp.where(n < max_exact, n, large)


def to_blocks(t):
    b, h, s = t.shape[:3]
    t = t.reshape((b, h, s // Q_BLOCK, Q_BLOCK) + t.shape[3:])
    return jnp.moveaxis(t, 2, 0)


def from_blocks(o):
    nb, b, h, qb, dv = o.shape
    return o.transpose(1, 0, 3, 2, 4).reshape(b, nb * qb, h * dv)


def mla_mixer(c_q, c_kv, k_rope_raw, pos, cos, sin,
              q_a_norm_w, w_uq, kv_a_norm_w, w_ukv, q_norm_w, k_norm_w):
    B, S, _ = c_q.shape
    qk_dim = MLA_NOPE_DIM + MLA_ROPE_DIM
    q = (rms_norm(c_q, q_a_norm_w) @ w_uq).reshape(B, S, MLA_HEADS, qk_dim)
    kv = (rms_norm(c_kv, kv_a_norm_w) @ w_ukv).reshape(B, S, MLA_HEADS, MLA_NOPE_DIM + MLA_V_DIM)
    k_nope, v = kv[..., :MLA_NOPE_DIM], kv[..., MLA_NOPE_DIM:]
    k_shared = jnp.broadcast_to(k_rope_raw[:, :, None, :], (B, S, MLA_HEADS, MLA_ROPE_DIM))
    k = jnp.concatenate([k_nope, k_shared], axis=-1)
    q = rms_norm(q, q_norm_w)
    k = rms_norm(k, k_norm_w)
    q = jnp.concatenate([q[..., :MLA_NOPE_DIM], apply_rope(q[..., MLA_NOPE_DIM:], cos, sin)], axis=-1)
    k = jnp.concatenate([k[..., :MLA_NOPE_DIM], apply_rope(k[..., MLA_NOPE_DIM:], cos, sin)], axis=-1)
    scale = qk_dim ** -0.5
    qb = to_blocks(q.transpose(0, 2, 1, 3))
    k = k.transpose(0, 2, 1, 3)
    v = v.transpose(0, 2, 1, 3)

    def body(args):
        qblk, qp = args
        s = jnp.einsum('bhqd,bhkd->bhqk', qblk, k).astype(jnp.float32) * scale
        s = jnp.where(chunk_mask(qp, pos)[None, None], s, -jnp.inf)
        p = jax.nn.softmax(s, axis=-1).astype(v.dtype)
        return jnp.einsum('bhqk,bhkd->bhqd', p, v)

    o = lax.map(body, (qb, pos.reshape(-1, Q_BLOCK)))
    return from_blocks(o)


def diff_mixer(q_raw, k_raw, v_raw, pos, rel_bias, q_norm_w, k_norm_w,
               lambda_q1, lambda_k1, lambda_q2, lambda_k2, subln_w, lambda_init):
    B, S, _ = q_raw.shape
    q = rms_norm(q_raw.reshape(B, S, DIFF_HEADS, 2, DIFF_HEAD_DIM), q_norm_w).transpose(0, 2, 1, 3, 4)
    k = rms_norm(k_raw.reshape(B, S, DIFF_HEADS, 2, DIFF_HEAD_DIM), k_norm_w).transpose(0, 2, 1, 3, 4)
    v = v_raw.reshape(B, S, DIFF_HEADS, DIFF_V_DIM).transpose(0, 2, 1, 3)
    lam = (jnp.exp(jnp.sum(lambda_q1.astype(jnp.float32) * lambda_k1.astype(jnp.float32)))
           - jnp.exp(jnp.sum(lambda_q2.astype(jnp.float32) * lambda_k2.astype(jnp.float32)))
           + lambda_init)
    scale = DIFF_HEAD_DIM ** -0.5
    qb = to_blocks(q)

    def body(args):
        qblk, qp = args
        s = jnp.einsum('bhqcd,bhkcd->cbhqk', qblk, k).astype(jnp.float32) * scale
        bias = rel_bias[t5_bucket(pos[None, :] - qp[:, None])]
        s = s + bias.transpose(2, 0, 1)[None, None].astype(jnp.float32)
        s = jnp.where(chunk_mask(qp, pos)[None, None, None], s, -jnp.inf)
        p = jax.nn.softmax(s, axis=-1)
        a = (p[0] - lam * p[1]).astype(v.dtype)
        return jnp.einsum('bhqk,bhkd->bhqd', a, v)

    o = lax.map(body, (qb, pos.reshape(-1, Q_BLOCK)))
    o = rms_norm(o, subln_w) * (1.0 - lambda_init)
    return from_blocks(o)


def moe_ffn(h, router_w, router_b, w_gate_up, b_gate_up, w_down, b_down):
    T, D = h.shape
    logits = (h @ router_w + router_b).astype(jnp.float32)
    top_logits, top_idx = lax.top_k(logits, TOP_K)
    gates = jax.nn.softmax(top_logits, axis=-1)
    tk = T * TOP_K
    flat_e = top_idx.reshape(tk)
    order = jnp.argsort(flat_e)
    sorted_e = flat_e[order]
    sorted_tok = order // TOP_K
    sorted_gate = gates.reshape(tk)[order]
    counts = jnp.bincount(flat_e, length=N_EXPERTS)
    starts = jnp.cumsum(counts) - counts
    padded = (counts + MOE_BLOCK - 1) // MOE_BLOCK * MOE_BLOCK
    pad_ends = jnp.cumsum(padded)
    pad_starts = pad_ends - padded
    dest = pad_starts[sorted_e] + (jnp.arange(tk) - starts[sorted_e])
    n_blocks = -(-tk // MOE_BLOCK) + N_EXPERTS
    slot_tok = jnp.zeros((n_blocks * MOE_BLOCK,), jnp.int32).at[dest].set(sorted_tok.astype(jnp.int32))
    block_e = jnp.minimum(jnp.searchsorted(pad_ends, jnp.arange(n_blocks) * MOE_BLOCK, side='right'),
                          N_EXPERTS - 1)

    def expert_block(args):
        toks, e = args
        xb = h[toks]
        gu = xb @ w_gate_up[e] + b_gate_up[e]
        g = jnp.minimum(gu[:, ::2], SWIGLU_LIMIT)
        u = jnp.clip(gu[:, 1::2], -SWIGLU_LIMIT, SWIGLU_LIMIT)
        a = g * jax.nn.sigmoid(SWIGLU_ALPHA * g) * (u + 1.0)
        return a @ w_down[e] + b_down[e]

    ys = lax.map(expert_block, (slot_tok.reshape(n_blocks, MOE_BLOCK), block_e))
    ys = ys.reshape(n_blocks * MOE_BLOCK, D)
    y_assign = ys[dest] * sorted_gate[:, None].astype(ys.dtype)
    return jax.ops.segment_sum(y_assign, sorted_tok, num_segments=T)


def setup_inputs(seed: int = 0) -> dict:
    key = jax.random.key(seed)
    ks = jax.random.split(key, 26)
    f32 = jnp.float32

    def nrm(k, shape, scale):
        return jax.random.normal(k, shape, f32) * scale

    def gain(k, shape):
        return 1.0 + 0.02 * jax.random.normal(k, shape, f32)

    L = DEPTH
    return {
        "x": jax.random.normal(ks[0], (BATCH, SEQ, D_MODEL), f32),
        "attn_norm_w": gain(ks[1], (L, D_MODEL)),
        "w_in": nrm(ks[2], (L, D_MODEL, IN_WIDTH), D_MODEL ** -0.5),
        "q_a_norm_w": gain(ks[3], (L, MLA_Q_RANK)),
        "w_uq": nrm(ks[4], (L, MLA_Q_RANK, MLA_HEADS * (MLA_NOPE_DIM + MLA_ROPE_DIM)), MLA_Q_RANK ** -0.5),
        "kv_a_norm_w": gain(ks[5], (L, MLA_KV_RANK)),
        "w_ukv": nrm(ks[6], (L, MLA_KV_RANK, MLA_HEADS * (MLA_NOPE_DIM + MLA_V_DIM)), MLA_KV_RANK ** -0.5),
        "mla_q_norm_w": gain(ks[7], (L, MLA_NOPE_DIM + MLA_ROPE_DIM)),
        "mla_k_norm_w": gain(ks[8], (L, MLA_NOPE_DIM + MLA_ROPE_DIM)),
        "diff_q_norm_w": gain(ks[9], (L, DIFF_HEAD_DIM)),
        "diff_k_norm_w": gain(ks[10], (L, DIFF_HEAD_DIM)),
        "lambda_q1": nrm(ks[11], (L, DIFF_HEAD_DIM), 0.1),
        "lambda_k1": nrm(ks[12], (L, DIFF_HEAD_DIM), 0.1),
        "lambda_q2": nrm(ks[13], (L, DIFF_HEAD_DIM), 0.1),
        "lambda_k2": nrm(ks[14], (L, DIFF_HEAD_DIM), 0.1),
        "diff_subln_w": gain(ks[15], (L, DIFF_V_DIM)),
        "w_o": nrm(ks[16], (L, MIX_WIDTH, D_MODEL), MIX_WIDTH ** -0.5),
        "ffn_norm_w": gain(ks[17], (L, D_MODEL)),
        "router_w": nrm(ks[18], (L, D_MODEL, N_EXPERTS), D_MODEL ** -0.5),
        "router_b": nrm(ks[19], (L, N_EXPERTS), 0.01),
        "w_gate_up": nrm(ks[20], (L, N_EXPERTS, D_MODEL, 2 * D_FF), D_MODEL ** -0.5),
        "b_gate_up": nrm(ks[21], (L, N_EXPERTS, 2 * D_FF), 0.02),
        "w_down": nrm(ks[22], (L, N_EXPERTS, D_FF, D_MODEL), D_FF ** -0.5),
        "b_down": nrm(ks[23], (L, N_EXPERTS, D_MODEL), 0.02),
        "rel_bias": nrm(ks[24], (REL_BUCKETS, DIFF_HEADS), 0.5),
    }


def reference(x, attn_norm_w, w_in, q_a_norm_w, w_uq, kv_a_norm_w, w_ukv, mla_q_norm_w, mla_k_norm_w,
              diff_q_norm_w, diff_k_norm_w, lambda_q1, lambda_k1, lambda_q2, lambda_k2, diff_subln_w,
              w_o, ffn_norm_w, router_w, router_b, w_gate_up, b_gate_up, w_down, b_down, rel_bias):
    B, S, D = x.shape
    pos = jnp.arange(S, dtype=jnp.int32)
    cos, sin = rope_tables(pos)
    for l in range(DEPTH):
        lambda_init = 0.8 - 0.6 * math.exp(-0.3 * l)
        h = rms_norm(x, attn_norm_w[l])
        proj = h @ w_in[l]
        c_q, c_kv, k_rope_raw, dq, dk, dv = jnp.split(proj, SPLIT_POINTS, axis=-1)
        y_mla = mla_mixer(c_q, c_kv, k_rope_raw, pos, cos, sin,
                          q_a_norm_w[l], w_uq[l], kv_a_norm_w[l], w_ukv[l],
                          mla_q_norm_w[l], mla_k_norm_w[l])
        y_diff = diff_mixer(dq, dk, dv, pos, rel_bias, diff_q_norm_w[l], diff_k_norm_w[l],
                            lambda_q1[l], lambda_k1[l], lambda_q2[l], lambda_k2[l],
                            diff_subln_w[l], lambda_init)
        x = x + jnp.concatenate([y_mla, y_diff], axis=-1) @ w_o[l]
        h = rms_norm(x, ffn_norm_w[l]).reshape(B * S, D)
        x = x + moe_ffn(h, router_w[l], router_b[l], w_gate_up[l], b_gate_up[l],
                        w_down[l], b_down[l]).reshape(B, S, D)
    return x
```

```python
import functools
import math

import jax
import jax.numpy as jnp
import numpy as np
from jax import lax
from jax.experimental import pallas as pl
from jax.experimental.pallas import tpu as pltpu

D_MODEL = 2048
CHUNK = 64
MLA_HEADS = 8
MLA_NOPE = 128
MLA_ROPE = 64
MLA_V = 128
MLA_Q_RANK = 384
MLA_KV_RANK = 256
ROPE_THETA = 10000.0
DIFF_HEADS = 8
DIFF_DH = 64
DIFF_V = 128
REL_BUCKETS = 32
REL_MAX_DIST = 128
N_EXPERTS = 32
TOP_K = 4
D_FF = 2048
SWIGLU_LIMIT = 7.0
SWIGLU_ALPHA = 1.702
EPS = 1e-6
LAMBDA_INIT = 0.8 - 0.6 * math.exp(-0.3 * 0)

LANES = 128
MXU_DIM = 256
VMEM_LIMIT = 48 * 1024 * 1024

PREP_TM = 256
ATT_TQ = 512
ATT_TK = 512
RT_TM = 256
MOE_TM = 256
MOE_TN = 1024
CMB_TM = 128

LOG2E = math.log2(math.e)
NEG = -1e30
F32 = jnp.float32
BF16 = jnp.bfloat16

_C_CQ = 0
_C_CKV = MLA_Q_RANK
_C_KR = _C_CKV + MLA_KV_RANK
_C_KRR = _C_KR + LANES
_C_MLA_END = _C_KRR + LANES
_C_DQ = _C_MLA_END
_C_DK = _C_DQ + DIFF_HEADS * 2 * DIFF_DH
_C_DV = _C_DK + DIFF_HEADS * 2 * DIFF_DH
_C_END = _C_DV + DIFF_HEADS * DIFF_V


def _cparams(sem):
    return pltpu.CompilerParams(dimension_semantics=sem, vmem_limit_bytes=VMEM_LIMIT)


def _const_spec(shape):
    nd = len(shape)
    return pl.BlockSpec(shape, lambda *_: (0,) * nd, pipeline_mode=pl.Buffered(1))


def _prep_kernel(x_ref, anw_ref, win_ref, qaw_ref, wuq_ref, kvaw_ref, wukv_ref, vec_ref, cos_ref, sin_ref,
                 qm_ref, km_ref, vm_ref, qd_ref, kd_ref, vd_ref):
    xf = x_ref[...]
    inv = lax.rsqrt(jnp.mean(xf * xf, axis=-1, keepdims=True) + EPS)
    h = (xf * inv * anw_ref[...]).astype(BF16)

    def rms(v, w):
        return v * lax.rsqrt(jnp.mean(v * v, axis=-1, keepdims=True) + EPS) * w

    pm = jnp.dot(h, win_ref[:, _C_CQ:_C_MLA_END], preferred_element_type=F32)
    cqn = rms(pm[:, _C_CQ:_C_CKV], qaw_ref[...]).astype(BF16)
    ckn = rms(pm[:, _C_CKV:_C_KR], kvaw_ref[...]).astype(BF16)
    kr = pm[:, _C_KR:_C_KRR]
    krr = pm[:, _C_KRR:_C_MLA_END]
    qall = jnp.dot(cqn, wuq_ref[...], preferred_element_type=F32)
    kvall = jnp.dot(ckn, wukv_ref[...], preferred_element_type=F32)

    vec = vec_ref[...]
    wqn, wqr, wqrr = vec[0:1], vec[1:2], vec[2:3]
    wkn, wkr, wkrr = vec[3:4], vec[4:5], vec[5:6]
    wdq, wdk = vec[6:7], vec[7:8]
    cos = cos_ref[...]
    sin = sin_ref[...]
    cq_r, sq_r = wqr * cos, wqrr * sin
    ck_r, sk_r = wkr * cos, wkrr * sin
    kr2 = kr * kr
    k_rope_base = kr * ck_r + krr * sk_r
    q_scale = (MLA_NOPE + MLA_ROPE) ** -0.5 * LOG2E
    inv_qk = 1.0 / (MLA_NOPE + MLA_ROPE)
    nq = MLA_HEADS * LANES
    for hh in range(MLA_HEADS):
        a = hh * LANES
        qn = qall[:, a:a + LANES]
        qr = qall[:, nq + a:nq + a + LANES]
        qrr = qall[:, 2 * nq + a:2 * nq + a + LANES]
        iq = lax.rsqrt(jnp.sum(qn * qn + qr * qr, axis=-1, keepdims=True) * inv_qk + EPS) * q_scale
        qm_ref[:, 2 * a:2 * a + LANES] = (qn * iq * wqn).astype(BF16)
        qm_ref[:, 2 * a + LANES:2 * a + 2 * LANES] = ((qr * cq_r + qrr * sq_r) * iq).astype(BF16)
        kn = kvall[:, 2 * a:2 * a + LANES]
        ik = lax.rsqrt(jnp.sum(kn * kn + kr2, axis=-1, keepdims=True) * inv_qk + EPS)
        km_ref[:, 2 * a:2 * a + LANES] = (kn * ik * wkn).astype(BF16)
        km_ref[:, 2 * a + LANES:2 * a + 2 * LANES] = (k_rope_base * ik).astype(BF16)
        vm_ref[:, a:a + LANES] = kvall[:, 2 * a + LANES:2 * a + 2 * LANES].astype(BF16)

    dq = jnp.dot(h, win_ref[:, _C_DQ:_C_DK], preferred_element_type=F32)
    dk = jnp.dot(h, win_ref[:, _C_DK:_C_DV], preferred_element_type=F32)
    vd_ref[...] = jnp.dot(h, win_ref[:, _C_DV:_C_END], preferred_element_type=F32).astype(BF16)
    lane = lax.broadcasted_iota(jnp.int32, (xf.shape[0], LANES), 1)
    lo = lane < DIFF_DH
    d_scale = DIFF_DH ** -0.5 * LOG2E
    inv_dh = 1.0 / DIFF_DH

    def half_norm(v):
        sq = v * v
        s1 = jnp.sum(jnp.where(lo, sq, 0.0), axis=-1, keepdims=True)
        s2 = jnp.sum(jnp.where(lo, 0.0, sq), axis=-1, keepdims=True)
        return v * jnp.where(lo, lax.rsqrt(s1 * inv_dh + EPS), lax.rsqrt(s2 * inv_dh + EPS))

    for hh in range(DIFF_HEADS):
        a = hh * LANES
        qn = half_norm(dq[:, a:a + LANES]) * (wdq * d_scale)
        qd_ref[:, 2 * a:2 * a + LANES] = jnp.where(lo, qn, 0.0).astype(BF16)
        qd_ref[:, 2 * a + LANES:2 * a + 2 * LANES] = jnp.where(lo, 0.0, qn).astype(BF16)
        kd_ref[:, a:a + LANES] = (half_norm(dk[:, a:a + LANES]) * wdk).astype(BF16)


def _prep(x2, anw, win_r, qaw, wuq_r, kvaw, wukv_b, vecs, cos128, sin128, seq):
    T = x2.shape[0]
    tm = PREP_TM
    nseq = seq // tm
    row = lambda i: (i, 0)
    outs = [
        jax.ShapeDtypeStruct((T, MLA_HEADS * 2 * LANES), BF16),
        jax.ShapeDtypeStruct((T, MLA_HEADS * 2 * LANES), BF16),
        jax.ShapeDtypeStruct((T, MLA_HEADS * MLA_V), BF16),
        jax.ShapeDtypeStruct((T, DIFF_HEADS * 2 * LANES), BF16),
        jax.ShapeDtypeStruct((T, DIFF_HEADS * LANES), BF16),
        jax.ShapeDtypeStruct((T, DIFF_HEADS * DIFF_V), BF16),
    ]
    return pl.pallas_call(
        _prep_kernel,
        grid=(T // tm,),
        in_specs=[
            pl.BlockSpec((tm, D_MODEL), row),
            _const_spec(anw.shape), _const_spec(win_r.shape), _const_spec(qaw.shape), _const_spec(wuq_r.shape),
            _const_spec(kvaw.shape), _const_spec(wukv_b.shape), _const_spec(vecs.shape),
            pl.BlockSpec((tm, LANES), lambda i: (i % nseq, 0)),
            pl.BlockSpec((tm, LANES), lambda i: (i % nseq, 0)),
        ],
        out_specs=[pl.BlockSpec((tm, o.shape[1]), row) for o in outs],
        out_shape=outs,
        compiler_params=_cparams(("parallel",)),
        name="prep",
    )(x2, anw, win_r, qaw, wuq_r, kvaw, wukv_b, vecs, cos128, sin128)


def _relbias_kernel(rb_ref, o_ref, *, tq, tk):
    hh = pl.program_id(0)
    which = pl.program_id(1)
    row = lax.broadcasted_iota(jnp.int32, (tq, tk), 0)
    col = lax.broadcasted_iota(jnp.int32, (tq, tk), 1)
    rel = col - which * tk - row
    nb = REL_BUCKETS // 2
    max_exact = nb // 2
    ret = jnp.where(rel > 0, nb, 0)
    n = jnp.abs(rel)
    nf = jnp.maximum(n, 1).astype(F32)
    large = max_exact + (jnp.log(nf / max_exact) / math.log(REL_MAX_DIST / max_exact)
                         * (nb - max_exact)).astype(jnp.int32)
    large = jnp.minimum(large, nb - 1)
    bucket = ret + jnp.where(n < max_exact, n, large)
    bias = jnp.zeros((tq, tk), F32)
    for b in range(REL_BUCKETS):
        bias = jnp.where(bucket == b, rb_ref[b, hh], bias)
    bias = (bias - rb_ref[nb - 1, hh]) * LOG2E
    allowed = ((col // CHUNK) <= (row // CHUNK)) | (which > 0)
    o_ref[0, 0] = jnp.where(allowed, bias, NEG)


def _relbias(rel_bias, tq, tk):
    return pl.pallas_call(
        functools.partial(_relbias_kernel, tq=tq, tk=tk),
        grid=(DIFF_HEADS, 2),
        in_specs=[pl.BlockSpec(memory_space=pltpu.SMEM)],
        out_specs=pl.BlockSpec((1, 1, tq, tk), lambda h, w: (h, w, 0, 0)),
        out_shape=jax.ShapeDtypeStruct((DIFF_HEADS, 2, tq, tk), F32),
        compiler_params=_cparams(("parallel", "parallel")),
        name="relbias",
    )(rel_bias)


def _attn_kernel(*refs, diff, tq, tk):
    if diff:
        q_ref, k_ref, v_ref, nb_ref, lam_ref, sub_ref, o_ref, m_ref, l_ref, acc_ref = refs
    else:
        q_ref, k_ref, v_ref, o_ref, m_ref, l_ref, acc_ref = refs
    i = pl.program_id(2)
    if diff:
        blk = q_ref[...]
        q = jnp.concatenate([blk[:, :LANES], blk[:, LANES:]], axis=0)
    else:
        q = q_ref[...]
    m_ref[...] = jnp.full(m_ref.shape, NEG, F32)
    l_ref[...] = jnp.zeros(l_ref.shape, F32)
    acc_ref[...] = jnp.zeros(acc_ref.shape, F32)

    def step(j, bias):
        start = pl.multiple_of(j * tk, tk)
        ks = k_ref[pl.ds(start, tk), :]
        vs = v_ref[pl.ds(start, tk), :]
        s = lax.dot_general(q, ks, (((1,), (1,)), ((), ())), preferred_element_type=F32)
        if bias is not None:
            s = s + bias
        m_prev = m_ref[...]
        m_new = jnp.maximum(m_prev, jnp.max(s, axis=1, keepdims=True))
        alpha = jnp.exp2(m_prev - m_new)
        p = jnp.exp2(s - jnp.concatenate([m_new] * (tk // LANES), axis=1))
        psum = p[:, 0:LANES]
        for c in range(1, tk // LANES):
            psum = psum + p[:, c * LANES:(c + 1) * LANES]
        l_ref[...] = alpha * l_ref[...] + psum
        acc_ref[...] = alpha * acc_ref[...] + jnp.dot(p.astype(BF16), vs, preferred_element_type=F32)
        m_ref[...] = m_new

    def far_body(j, c):
        step(j, None)
        return c

    if diff:
        lax.fori_loop(0, jnp.maximum(i - 1, 0), far_body, 0)

        @pl.when(i > 0)
        def _():
            b1 = nb_ref[0, 1]
            step(i - 1, jnp.concatenate([b1, b1], axis=0))

        b0 = nb_ref[0, 0]
        step(i, jnp.concatenate([b0, b0], axis=0))
    else:
        lax.fori_loop(0, i, far_body, 0)
        row = lax.broadcasted_iota(jnp.int32, (tq, tk), 0)
        col = lax.broadcasted_iota(jnp.int32, (tq, tk), 1)
        step(i, jnp.where((col // CHUNK) <= (row // CHUNK), 0.0, NEG))

    l = jnp.sum(l_ref[...], axis=1, keepdims=True)
    o = acc_ref[...] / l
    if diff:
        lv = lam_ref[...]
        lam = (jnp.exp(jnp.sum(lv[0:1] * lv[1:2], axis=1, keepdims=True))
               - jnp.exp(jnp.sum(lv[2:3] * lv[3:4], axis=1, keepdims=True)) + LAMBDA_INIT)
        od = o[:tq] - lam * o[tq:]
        od = od * lax.rsqrt(jnp.mean(od * od, axis=-1, keepdims=True) + EPS) * sub_ref[...]
        o_ref[...] = (od * (1.0 - LAMBDA_INIT)).astype(BF16)
    else:
        o_ref[...] = o.astype(BF16)


def _attention(q, k, v, batch, seq, heads, dk, diff, extra=()):
    tq, tk = ATT_TQ, ATT_TK
    nq = seq // tq
    rows = 2 * tq if diff else tq
    in_specs = [
        pl.BlockSpec((tq, 2 * LANES), lambda b, h, i: (b * nq + i, h)),
        pl.BlockSpec((seq, dk), lambda b, h, i: (b, h)),
        pl.BlockSpec((seq, LANES), lambda b, h, i: (b, h)),
    ]
    if diff:
        nbias, lamv, subw = extra
        in_specs += [
            pl.BlockSpec((1, 2, tq, tk), lambda b, h, i: (h, 0, 0, 0)),
            pl.BlockSpec(lamv.shape, lambda b, h, i: (0, 0)),
            pl.BlockSpec(subw.shape, lambda b, h, i: (0, 0)),
        ]
    return pl.pallas_call(
        functools.partial(_attn_kernel, diff=diff, tq=tq, tk=tk),
        grid=(batch, heads, nq),
        in_specs=in_specs,
        out_specs=pl.BlockSpec((tq, LANES), lambda b, h, i: (b * nq + i, h)),
        out_shape=jax.ShapeDtypeStruct((batch * seq, heads * LANES), BF16),
        scratch_shapes=[pltpu.VMEM((rows, LANES), F32)] * 3,
        compiler_params=_cparams(("parallel", "parallel", "arbitrary")),
        name="attn_diff" if diff else "attn_mla",
    )(q, k, v, *extra)


def _router_kernel(om_ref, od_ref, x_ref, wo_ref, fw_ref, rw_ref, rb_ref,
                   x1_ref, h_ref, route_ref, cnt_ref, carry_ref):
    tm = x_ref.shape[0]
    half = om_ref.shape[1]

    @pl.when(pl.program_id(0) == 0)
    def _():
        carry_ref[...] = jnp.zeros(carry_ref.shape, F32)

    y = (jnp.dot(om_ref[...], wo_ref[0:half, :], preferred_element_type=F32)
         + jnp.dot(od_ref[...], wo_ref[half:2 * half, :], preferred_element_type=F32))
    x1 = x_ref[...] + y
    x1_ref[...] = x1
    hh = x1 * lax.rsqrt(jnp.mean(x1 * x1, axis=-1, keepdims=True) + EPS) * fw_ref[...]
    h_ref[...] = hh
    logits = jnp.dot(hh, rw_ref[...], preferred_element_type=F32, precision=lax.Precision.HIGHEST) + rb_ref[...]
    lane = lax.broadcasted_iota(jnp.int32, (tm, LANES), 1).astype(F32)
    work = jnp.where(lane < N_EXPERTS, logits, -jnp.inf)
    vals, idxs = [], []
    for _ in range(TOP_K):
        mx = jnp.max(work, axis=-1, keepdims=True)
        ix = jnp.min(jnp.where(work == mx, lane, float(LANES)), axis=-1, keepdims=True)
        vals.append(mx)
        idxs.append(ix)
        work = jnp.where(lane == ix, -jnp.inf, work)
    es = [jnp.exp(v - vals[0]) for v in vals]
    den = es[0] + es[1] + es[2] + es[3]
    onehot = jnp.zeros((tm, LANES), F32)
    for ix in idxs:
        onehot = onehot + jnp.where(lane == ix, 1.0, 0.0)
    r_i = lax.broadcasted_iota(jnp.int32, (tm, tm), 0)
    c_i = lax.broadcasted_iota(jnp.int32, (tm, tm), 1)
    tri = jnp.where(c_i < r_i, 1.0, 0.0).astype(BF16)
    prefix = jnp.dot(tri, onehot.astype(BF16), preferred_element_type=F32) + carry_ref[...]
    route = jnp.zeros((tm, LANES), F32)
    for kk in range(TOP_K):
        rank = jnp.sum(jnp.where(lane == idxs[kk], prefix, 0.0), axis=-1, keepdims=True)
        route = jnp.where(lane == kk, idxs[kk], route)
        route = jnp.where(lane == TOP_K + kk, es[kk] / den, route)
        route = jnp.where(lane == 2 * TOP_K + kk, rank, route)
    route_ref[...] = route
    carry = carry_ref[...] + jnp.sum(onehot, axis=0, keepdims=True)
    carry_ref[...] = carry
    cnt_ref[...] = carry


def _router(om, od, x2, wo_b, fw, rw_pad, rb_pad):
    T = x2.shape[0]
    tm = RT_TM
    row = lambda i: (i, 0)
    outs = [
        jax.ShapeDtypeStruct((T, D_MODEL), F32),
        jax.ShapeDtypeStruct((T, D_MODEL), F32),
        jax.ShapeDtypeStruct((T, LANES), F32),
        jax.ShapeDtypeStruct((1, LANES), F32),
    ]
    return pl.pallas_call(
        _router_kernel,
        grid=(T // tm,),
        in_specs=[
            pl.BlockSpec((tm, om.shape[1]), row), pl.BlockSpec((tm, od.shape[1]), row),
            pl.BlockSpec((tm, D_MODEL), row),
            _const_spec(wo_b.shape), _const_spec(fw.shape), _const_spec(rw_pad.shape), _const_spec(rb_pad.shape),
        ],
        out_specs=[pl.BlockSpec((tm, D_MODEL), row), pl.BlockSpec((tm, D_MODEL), row),
                   pl.BlockSpec((tm, LANES), row), pl.BlockSpec((1, LANES), lambda i: (0, 0))],
        out_shape=outs,
        scratch_shapes=[pltpu.VMEM((1, LANES), F32)],
        compiler_params=_cparams(("arbitrary",)),
        name="router",
    )(om, od, x2, wo_b, fw, rw_pad, rb_pad)


def _row_copy(src_hbm, idx, dst_ref, r, sem):
    return pltpu.make_async_copy(src_hbm.at[pl.ds(idx, 1), :], dst_ref.at[pl.ds(r, 1), :], sem)


def _dispatch_kernel(tok_ref, h_hbm, o_ref, sem):
    rows = o_ref.shape[0]

    def issue(r, c):
        _row_copy(h_hbm, tok_ref[0, 0, r], o_ref, r, sem).start()
        return c

    lax.fori_loop(0, rows, issue, 0, unroll=8)
    pltpu.make_async_copy(h_hbm.at[pl.ds(0, rows), :], o_ref, sem).wait()


def _dispatch(slot_tok3, h):
    nblk, _, tm = slot_tok3.shape
    return pl.pallas_call(
        _dispatch_kernel,
        grid=(nblk,),
        in_specs=[pl.BlockSpec((1, 1, tm), lambda i: (i, 0, 0), memory_space=pltpu.SMEM),
                  pl.BlockSpec(memory_space=pl.ANY)],
        out_specs=pl.BlockSpec((tm, D_MODEL), lambda i: (i, 0)),
        out_shape=jax.ShapeDtypeStruct((nblk * tm, D_MODEL), F32),
        scratch_shapes=[pltpu.SemaphoreType.DMA(())],
        compiler_params=_cparams(("arbitrary",)),
        name="dispatch",
    )(slot_tok3, h)


def _new_expert(be_ref, m):
    prev = be_ref[jnp.maximum(m - 1, 0)]
    return (m == 0) | (be_ref[m] != prev)


def _expert_a_kernel(be_ref, nu_ref, x_ref, w_ref, b_ref, o_ref, wb_ref):
    m = pl.program_id(1)
    tn = w_ref.shape[2]

    @pl.when(m < nu_ref[0])
    def _():
        @pl.when(_new_expert(be_ref, m))
        def _():
            wb_ref[...] = w_ref[0].astype(BF16)

        gu = jnp.dot(x_ref[...].astype(BF16), wb_ref[...], preferred_element_type=F32) + b_ref[0]
        r_i = lax.broadcasted_iota(jnp.int32, (MXU_DIM, LANES), 0)
        c_i = lax.broadcasted_iota(jnp.int32, (MXU_DIM, LANES), 1)
        sel = jnp.where(r_i == 2 * c_i, 1.0, 0.0).astype(BF16)
        for c in range(tn // MXU_DIM):
            parts = []
            for s in range(MXU_DIM // LANES):
                g = gu[:, c * MXU_DIM + s * LANES:c * MXU_DIM + (s + 1) * LANES]
                gate = jnp.minimum(g, SWIGLU_LIMIT)
                up1 = jnp.clip(g, -SWIGLU_LIMIT, SWIGLU_LIMIT) + 1.0
                act = gate * jax.nn.sigmoid(SWIGLU_ALPHA * gate)
                parts.append((act * pltpu.roll(up1, LANES - 1, 1)).astype(BF16))
            pair = jnp.concatenate(parts, axis=1)
            o_ref[:, c * LANES:(c + 1) * LANES] = jnp.dot(pair, sel, preferred_element_type=F32).astype(BF16)

    @pl.when(m >= nu_ref[0])
    def _():
        o_ref[...] = jnp.zeros(o_ref.shape, o_ref.dtype)


def _expert_b_kernel(be_ref, nu_ref, a_ref, w_ref, b_ref, o_ref, wb_ref):
    m = pl.program_id(1)

    @pl.when(m < nu_ref[0])
    def _():
        @pl.when(_new_expert(be_ref, m))
        def _():
            wb_ref[...] = w_ref[0].astype(BF16)

        o_ref[...] = jnp.dot(a_ref[...], wb_ref[...], preferred_element_type=F32) + b_ref[0]

    @pl.when(m >= nu_ref[0])
    def _():
        o_ref[...] = jnp.zeros(o_ref.shape, o_ref.dtype)


def _expert_call(kernel, name, block_e, n_used, xin, w, b3, out_cols, out_dtype, out_tn):
    nblk = block_e.shape[0]
    tm, tn = MOE_TM, MOE_TN
    kdim = w.shape[1]
    nj = w.shape[2] // tn

    def blk(m, nu):
        return jnp.minimum(m, nu[0] - 1)

    grid_spec = pltpu.PrefetchScalarGridSpec(
        num_scalar_prefetch=2,
        grid=(nj, nblk),
        in_specs=[
            pl.BlockSpec((tm, kdim), lambda j, m, be, nu: (blk(m, nu), 0)),
            pl.BlockSpec((1, kdim, tn), lambda j, m, be, nu: (be[blk(m, nu)], 0, j)),
            pl.BlockSpec((1, 1, tn), lambda j, m, be, nu: (be[blk(m, nu)], 0, j)),
        ],
        out_specs=pl.BlockSpec((tm, out_tn), lambda j, m, be, nu: (m, j)),
        scratch_shapes=[pltpu.VMEM((kdim, tn), BF16)],
    )
    return pl.pallas_call(
        kernel,
        grid_spec=grid_spec,
        out_shape=jax.ShapeDtypeStruct((nblk * tm, out_cols), out_dtype),
        compiler_params=_cparams(("arbitrary", "arbitrary")),
        name=name,
    )(block_e, n_used, xin, w, b3)


def _combine_kernel(dest_ref, y_hbm, x1_ref, route_ref, o_ref, buf_ref, sem):
    tm = o_ref.shape[0]

    def issue(r, c):
        for kk in range(TOP_K):
            _row_copy(y_hbm, dest_ref[0, 0, r * TOP_K + kk], buf_ref.at[kk], r, sem).start()
        return c

    lax.fori_loop(0, tm, issue, 0, unroll=4)
    for kk in range(TOP_K):
        pltpu.make_async_copy(y_hbm.at[pl.ds(0, tm), :], buf_ref.at[kk], sem).wait()
    route = route_ref[...]
    out = x1_ref[...]
    for kk in range(TOP_K):
        out = out + route[:, TOP_K + kk:TOP_K + kk + 1] * buf_ref[kk]
    o_ref[...] = out


def _combine(dest3, y, x1, route):
    T = x1.shape[0]
    tm = CMB_TM
    row = lambda i: (i, 0)
    return pl.pallas_call(
        _combine_kernel,
        grid=(T // tm,),
        in_specs=[pl.BlockSpec((1, 1, tm * TOP_K), lambda i: (i, 0, 0), memory_space=pltpu.SMEM),
                  pl.BlockSpec(memory_space=pl.ANY),
                  pl.BlockSpec((tm, D_MODEL), row),
                  pl.BlockSpec((tm, LANES), row)],
        out_specs=pl.BlockSpec((tm, D_MODEL), row),
        out_shape=jax.ShapeDtypeStruct((T, D_MODEL), F32),
        scratch_shapes=[pltpu.VMEM((TOP_K, tm, D_MODEL), F32), pltpu.SemaphoreType.DMA(())],
        compiler_params=_cparams(("arbitrary",)),
        name="combine",
    )(dest3, y, x1, route)


def _pad_lanes(v, width=LANES):
    return jnp.pad(v, [(0, 0)] * (v.ndim - 1) + [(0, width - v.shape[-1])])


def _swap_halves(v):
    half = v.shape[-1] // 2
    return jnp.concatenate([v[..., half:], v[..., :half]], axis=-1)


def _token_mixers(x2, B, S, attn_norm_w, w_in, q_a_norm_w, w_uq, kv_a_norm_w, w_ukv, mla_q_norm_w, mla_k_norm_w,
                  diff_q_norm_w, diff_k_norm_w, lambda_q1, lambda_k1, lambda_q2, lambda_k2, diff_subln_w, rel_bias):
    L = 0
    win = w_in[L]
    sp = np.cumsum([MLA_Q_RANK, MLA_KV_RANK, MLA_ROPE])
    w_kr = win[:, sp[1]:sp[2]]
    win_r = jnp.concatenate(
        [win[:, :sp[1]], _pad_lanes(w_kr), _pad_lanes(_swap_halves(w_kr)), win[:, sp[2]:]], axis=1).astype(BF16)
    qk = MLA_NOPE + MLA_ROPE
    wuq3 = w_uq[L].reshape(MLA_Q_RANK, MLA_HEADS, qk)
    wuq_rope = wuq3[:, :, MLA_NOPE:]
    wuq_r = jnp.concatenate([
        wuq3[:, :, :MLA_NOPE].reshape(MLA_Q_RANK, -1),
        _pad_lanes(wuq_rope).reshape(MLA_Q_RANK, -1),
        _pad_lanes(_swap_halves(wuq_rope)).reshape(MLA_Q_RANK, -1)], axis=1).astype(BF16)
    wukv_b = w_ukv[L].astype(BF16)
    qn_w, kn_w = mla_q_norm_w[L], mla_k_norm_w[L]
    vecs = jnp.stack([
        qn_w[:MLA_NOPE], _pad_lanes(qn_w[MLA_NOPE:]), _pad_lanes(_swap_halves(qn_w[MLA_NOPE:])),
        kn_w[:MLA_NOPE], _pad_lanes(kn_w[MLA_NOPE:]), _pad_lanes(_swap_halves(kn_w[MLA_NOPE:])),
        jnp.tile(diff_q_norm_w[L], 2), jnp.tile(diff_k_norm_w[L], 2)])
    lamv = _pad_lanes(jnp.stack([lambda_q1[L], lambda_k1[L], lambda_q2[L], lambda_k2[L]]))

    pos = jnp.arange(S, dtype=jnp.int32)
    inv_freq = ROPE_THETA ** (-jnp.arange(0, MLA_ROPE, 2, dtype=F32) / MLA_ROPE)
    ang = pos.astype(F32)[:, None] * inv_freq[None, :]
    cos128 = _pad_lanes(jnp.concatenate([jnp.cos(ang), jnp.cos(ang)], axis=1))
    sin128 = _pad_lanes(jnp.concatenate([-jnp.sin(ang), jnp.sin(ang)], axis=1))

    qm, km, vm, qd, kd, vd = _prep(x2, attn_norm_w[L][None], win_r, q_a_norm_w[L][None], wuq_r,
                                   kv_a_norm_w[L][None], wukv_b, vecs, cos128, sin128, S)
    nbias = _relbias(rel_bias, ATT_TQ, ATT_TK)
    om = _attention(qm, km, vm, B, S, MLA_HEADS, 2 * LANES, diff=False)
    od = _attention(qd, kd, vd, B, S, DIFF_HEADS, LANES, diff=True,
                    extra=(nbias, lamv, diff_subln_w[L][None]))
    return om, od


def _moe_block(om, od, x2, w_o, ffn_norm_w, router_w, router_b, w_gate_up, b_gate_up, w_down, b_down):
    L = 0
    T, D = x2.shape
    wo_b = w_o[L].astype(BF16)
    rw_pad = _pad_lanes(router_w[L])
    rb_pad = _pad_lanes(router_b[L][None, :])

    x1, h, route, cnt = _router(om, od, x2, wo_b, ffn_norm_w[L][None], rw_pad, rb_pad)

    tm = MOE_TM
    nblk = -(-T * TOP_K // tm) + N_EXPERTS
    idx = route[:, 0:TOP_K].astype(jnp.int32)
    rank = route[:, 2 * TOP_K:3 * TOP_K].astype(jnp.int32)
    counts = cnt[0, :N_EXPERTS].astype(jnp.int32)
    padded = (counts + tm - 1) // tm * tm
    pad_ends = jnp.cumsum(padded)
    pad_starts = pad_ends - padded
    dest = pad_starts[idx] + rank
    tok = jnp.broadcast_to(jnp.arange(T, dtype=jnp.int32)[:, None], (T, TOP_K))
    slot_tok = jnp.zeros((nblk * tm,), jnp.int32).at[dest.reshape(-1)].set(tok.reshape(-1))
    block_e = jnp.minimum(jnp.searchsorted(pad_ends, jnp.arange(nblk, dtype=jnp.int32) * tm, side='right'),
                          N_EXPERTS - 1).astype(jnp.int32)
    n_used = (pad_ends[-1] // tm).astype(jnp.int32)[None]

    xs = _dispatch(slot_tok.reshape(nblk, 1, tm), h)
    act = _expert_call(_expert_a_kernel, "expert_a", block_e, n_used, xs, w_gate_up[L],
                       b_gate_up[L][:, None, :], D_FF, BF16, MOE_TN // 2)
    y = _expert_call(_expert_b_kernel, "expert_b", block_e, n_used, act, w_down[L],
                     b_down[L][:, None, :], D, F32, MOE_TN)
    return _combine(dest.reshape(T // CMB_TM, 1, CMB_TM * TOP_K), y, x1, route)


def kernel(x, attn_norm_w, w_in, q_a_norm_w, w_uq, kv_a_norm_w, w_ukv, mla_q_norm_w, mla_k_norm_w, diff_q_norm_w, diff_k_norm_w, lambda_q1, lambda_k1, lambda_q2, lambda_k2, diff_subln_w, w_o, ffn_norm_w, router_w, router_b, w_gate_up, b_gate_up, w_down, b_down, rel_bias):
    B, S, D = x.shape
    x2 = x.reshape(B * S, D)
    om, od = _token_mixers(x2, B, S, attn_norm_w, w_in, q_a_norm_w, w_uq, kv_a_norm_w, w_ukv, mla_q_norm_w,
                           mla_k_norm_w, diff_q_norm_w, diff_k_norm_w, lambda_q1, lambda_k1, lambda_q2, lambda_k2,
                           diff_subln_w, rel_bias)
    out = _moe_block(om, od, x2, w_o, ffn_norm_w, router_w, router_b, w_gate_up, b_gate_up, w_down, b_down)
    return out.reshape(B, S, D)
```

```python
import functools
import math

import jax
import jax.numpy as jnp
import numpy as np
from jax import lax
from jax.experimental import pallas as pl
from jax.experimental.pallas import tpu as pltpu

D_MODEL = 2048
CHUNK = 64
MLA_HEADS = 8
MLA_NOPE = 128
MLA_ROPE = 64
MLA_V = 128
MLA_Q_RANK = 384
MLA_KV_RANK = 256
ROPE_THETA = 10000.0
DIFF_HEADS = 8
DIFF_DH = 64
DIFF_V = 128
REL_BUCKETS = 32
REL_MAX_DIST = 128
N_EXPERTS = 32
TOP_K = 4
D_FF = 2048
SWIGLU_LIMIT = 7.0
SWIGLU_ALPHA = 1.702
EPS = 1e-6
LAMBDA_INIT = 0.8 - 0.6 * math.exp(-0.3 * 0)

LANES = 128
MXU_DIM = 256
VMEM_LIMIT = 48 * 1024 * 1024

PREP_TM = 256
ATT_TQ = 512
ATT_TK = 512
RT_TM = 256
SUB = 256
MOE_TM = 1024
NSUB = MOE_TM // SUB
MOE_TN = 1024
CMB_TM = 128
TOK_ROWS = D_MODEL // LANES

LOG2E = math.log2(math.e)
NEG = -1e30
F32 = jnp.float32
BF16 = jnp.bfloat16

_C_CQ = 0
_C_CKV = MLA_Q_RANK
_C_KR = _C_CKV + MLA_KV_RANK
_C_KRR = _C_KR + LANES
_C_MLA_END = _C_KRR + LANES
_C_DQ = _C_MLA_END
_C_DK = _C_DQ + DIFF_HEADS * 2 * DIFF_DH
_C_DV = _C_DK + DIFF_HEADS * 2 * DIFF_DH
_C_END = _C_DV + DIFF_HEADS * DIFF_V


def _cparams(sem):
    return pltpu.CompilerParams(dimension_semantics=sem, vmem_limit_bytes=VMEM_LIMIT)


def _const_spec(shape):
    nd = len(shape)
    return pl.BlockSpec(shape, lambda *_: (0,) * nd, pipeline_mode=pl.Buffered(1))


def _prep_kernel(x_ref, anw_ref, win_ref, qaw_ref, wuq_ref, kvaw_ref, wukv_ref, vec_ref, cos_ref, sin_ref,
                 qm_ref, km_ref, vm_ref, qd_ref, kd_ref, vd_ref):
    xf = x_ref[...]
    inv = lax.rsqrt(jnp.mean(xf * xf, axis=-1, keepdims=True) + EPS)
    h = (xf * inv * anw_ref[...]).astype(BF16)

    def rms(v, w):
        return v * lax.rsqrt(jnp.mean(v * v, axis=-1, keepdims=True) + EPS) * w

    pm = jnp.dot(h, win_ref[:, _C_CQ:_C_MLA_END], preferred_element_type=F32)
    cqn = rms(pm[:, _C_CQ:_C_CKV], qaw_ref[...]).astype(BF16)
    ckn = rms(pm[:, _C_CKV:_C_KR], kvaw_ref[...]).astype(BF16)
    kr = pm[:, _C_KR:_C_KRR]
    krr = pm[:, _C_KRR:_C_MLA_END]
    qall = jnp.dot(cqn, wuq_ref[...], preferred_element_type=F32)
    kvall = jnp.dot(ckn, wukv_ref[...], preferred_element_type=F32)

    vec = vec_ref[...]
    wqn, wqr, wqrr = vec[0:1], vec[1:2], vec[2:3]
    wkn, wkr, wkrr = vec[3:4], vec[4:5], vec[5:6]
    wdq, wdk = vec[6:7], vec[7:8]
    cos = cos_ref[...]
    sin = sin_ref[...]
    cq_r, sq_r = wqr * cos, wqrr * sin
    ck_r, sk_r = wkr * cos, wkrr * sin
    kr2 = kr * kr
    k_rope_base = kr * ck_r + krr * sk_r
    q_scale = (MLA_NOPE + MLA_ROPE) ** -0.5 * LOG2E
    inv_qk = 1.0 / (MLA_NOPE + MLA_ROPE)
    nq = MLA_HEADS * LANES
    for hh in range(MLA_HEADS):
        a = hh * LANES
        qn = qall[:, a:a + LANES]
        qr = qall[:, nq + a:nq + a + LANES]
        qrr = qall[:, 2 * nq + a:2 * nq + a + LANES]
        iq = lax.rsqrt(jnp.sum(qn * qn + qr * qr, axis=-1, keepdims=True) * inv_qk + EPS) * q_scale
        qm_ref[:, 2 * a:2 * a + LANES] = (qn * iq * wqn).astype(BF16)
        qm_ref[:, 2 * a + LANES:2 * a + 2 * LANES] = ((qr * cq_r + qrr * sq_r) * iq).astype(BF16)
        kn = kvall[:, 2 * a:2 * a + LANES]
        ik = lax.rsqrt(jnp.sum(kn * kn + kr2, axis=-1, keepdims=True) * inv_qk + EPS)
        km_ref[:, 2 * a:2 * a + LANES] = (kn * ik * wkn).astype(BF16)
        km_ref[:, 2 * a + LANES:2 * a + 2 * LANES] = (k_rope_base * ik).astype(BF16)
        vm_ref[:, a:a + LANES] = kvall[:, 2 * a + LANES:2 * a + 2 * LANES].astype(BF16)

    dq = jnp.dot(h, win_ref[:, _C_DQ:_C_DK], preferred_element_type=F32)
    dk = jnp.dot(h, win_ref[:, _C_DK:_C_DV], preferred_element_type=F32)
    vd_ref[...] = jnp.dot(h, win_ref[:, _C_DV:_C_END], preferred_element_type=F32).astype(BF16)
    lane = lax.broadcasted_iota(jnp.int32, (xf.shape[0], LANES), 1)
    lo = lane < DIFF_DH
    d_scale = DIFF_DH ** -0.5 * LOG2E
    inv_dh = 1.0 / DIFF_DH

    def half_norm(v):
        sq = v * v
        s1 = jnp.sum(jnp.where(lo, sq, 0.0), axis=-1, keepdims=True)
        s2 = jnp.sum(jnp.where(lo, 0.0, sq), axis=-1, keepdims=True)
        return v * jnp.where(lo, lax.rsqrt(s1 * inv_dh + EPS), lax.rsqrt(s2 * inv_dh + EPS))

    for hh in range(DIFF_HEADS):
        a = hh * LANES
        qn = half_norm(dq[:, a:a + LANES]) * (wdq * d_scale)
        qd_ref[:, 2 * a:2 * a + LANES] = jnp.where(lo, qn, 0.0).astype(BF16)
        qd_ref[:, 2 * a + LANES:2 * a + 2 * LANES] = jnp.where(lo, 0.0, qn).astype(BF16)
        kd_ref[:, a:a + LANES] = (half_norm(dk[:, a:a + LANES]) * wdk).astype(BF16)


def _prep(x2, anw, win_r, qaw, wuq_r, kvaw, wukv_b, vecs, cos128, sin128, seq):
    T = x2.shape[0]
    tm = PREP_TM
    nseq = seq // tm
    row = lambda i: (i, 0)
    outs = [
        jax.ShapeDtypeStruct((T, MLA_HEADS * 2 * LANES), BF16),
        jax.ShapeDtypeStruct((T, MLA_HEADS * 2 * LANES), BF16),
        jax.ShapeDtypeStruct((T, MLA_HEADS * MLA_V), BF16),
        jax.ShapeDtypeStruct((T, DIFF_HEADS * 2 * LANES), BF16),
        jax.ShapeDtypeStruct((T, DIFF_HEADS * LANES), BF16),
        jax.ShapeDtypeStruct((T, DIFF_HEADS * DIFF_V), BF16),
    ]
    return pl.pallas_call(
        _prep_kernel,
        grid=(T // tm,),
        in_specs=[
            pl.BlockSpec((tm, D_MODEL), row),
            _const_spec(anw.shape), _const_spec(win_r.shape), _const_spec(qaw.shape), _const_spec(wuq_r.shape),
            _const_spec(kvaw.shape), _const_spec(wukv_b.shape), _const_spec(vecs.shape),
            pl.BlockSpec((tm, LANES), lambda i: (i % nseq, 0)),
            pl.BlockSpec((tm, LANES), lambda i: (i % nseq, 0)),
        ],
        out_specs=[pl.BlockSpec((tm, o.shape[1]), row) for o in outs],
        out_shape=outs,
        compiler_params=_cparams(("parallel",)),
        name="prep",
    )(x2, anw, win_r, qaw, wuq_r, kvaw, wukv_b, vecs, cos128, sin128)


def _relbias_kernel(rb_ref, o_ref, *, tq, tk):
    hh = pl.program_id(0)
    which = pl.program_id(1)
    row = lax.broadcasted_iota(jnp.int32, (tq, tk), 0)
    col = lax.broadcasted_iota(jnp.int32, (tq, tk), 1)
    rel = col - which * tk - row
    nb = REL_BUCKETS // 2
    max_exact = nb // 2
    ret = jnp.where(rel > 0, nb, 0)
    n = jnp.abs(rel)
    nf = jnp.maximum(n, 1).astype(F32)
    large = max_exact + (jnp.log(nf / max_exact) / math.log(REL_MAX_DIST / max_exact)
                         * (nb - max_exact)).astype(jnp.int32)
    large = jnp.minimum(large, nb - 1)
    bucket = ret + jnp.where(n < max_exact, n, large)
    bias = jnp.zeros((tq, tk), F32)
    for b in range(REL_BUCKETS):
        bias = jnp.where(bucket == b, rb_ref[b, hh], bias)
    bias = (bias - rb_ref[nb - 1, hh]) * LOG2E
    allowed = ((col // CHUNK) <= (row // CHUNK)) | (which > 0)
    o_ref[0, 0] = jnp.where(allowed, bias, NEG)


def _relbias(rel_bias, tq, tk):
    return pl.pallas_call(
        functools.partial(_relbias_kernel, tq=tq, tk=tk),
        grid=(DIFF_HEADS, 2),
        in_specs=[pl.BlockSpec(memory_space=pltpu.SMEM)],
        out_specs=pl.BlockSpec((1, 1, tq, tk), lambda h, w: (h, w, 0, 0)),
        out_shape=jax.ShapeDtypeStruct((DIFF_HEADS, 2, tq, tk), F32),
        compiler_params=_cparams(("parallel", "parallel")),
        name="relbias",
    )(rel_bias)


def _attn_kernel(*refs, diff, tq, tk):
    if diff:
        q_ref, k_ref, v_ref, nb_ref, lam_ref, sub_ref, o_ref, m_ref, l_ref, acc_ref = refs
    else:
        q_ref, k_ref, v_ref, o_ref, m_ref, l_ref, acc_ref = refs
    i = pl.program_id(2)
    if diff:
        blk = q_ref[...]
        q = jnp.concatenate([blk[:, :LANES], blk[:, LANES:]], axis=0)
    else:
        q = q_ref[...]
    m_ref[...] = jnp.full(m_ref.shape, NEG, F32)
    l_ref[...] = jnp.zeros(l_ref.shape, F32)
    acc_ref[...] = jnp.zeros(acc_ref.shape, F32)

    def step(j, bias):
        start = pl.multiple_of(j * tk, tk)
        ks = k_ref[pl.ds(start, tk), :]
        vs = v_ref[pl.ds(start, tk), :]
        s = lax.dot_general(q, ks, (((1,), (1,)), ((), ())), preferred_element_type=F32)
        if bias is not None:
            s = s + bias
        m_prev = m_ref[...]
        m_new = jnp.maximum(m_prev, jnp.max(s, axis=1, keepdims=True))
        alpha = jnp.exp2(m_prev - m_new)
        p = jnp.exp2(s - jnp.concatenate([m_new] * (tk // LANES), axis=1))
        psum = p[:, 0:LANES]
        for c in range(1, tk // LANES):
            psum = psum + p[:, c * LANES:(c + 1) * LANES]
        l_ref[...] = alpha * l_ref[...] + psum
        acc_ref[...] = alpha * acc_ref[...] + jnp.dot(p.astype(BF16), vs, preferred_element_type=F32)
        m_ref[...] = m_new

    def far_body(j, c):
        step(j, None)
        return c

    if diff:
        lax.fori_loop(0, jnp.maximum(i - 1, 0), far_body, 0)

        @pl.when(i > 0)
        def _():
            b1 = nb_ref[0, 1]
            step(i - 1, jnp.concatenate([b1, b1], axis=0))

        b0 = nb_ref[0, 0]
        step(i, jnp.concatenate([b0, b0], axis=0))
    else:
        lax.fori_loop(0, i, far_body, 0)
        row = lax.broadcasted_iota(jnp.int32, (tq, tk), 0)
        col = lax.broadcasted_iota(jnp.int32, (tq, tk), 1)
        step(i, jnp.where((col // CHUNK) <= (row // CHUNK), 0.0, NEG))

    l = jnp.sum(l_ref[...], axis=1, keepdims=True)
    o = acc_ref[...] / l
    if diff:
        lv = lam_ref[...]
        lam = (jnp.exp(jnp.sum(lv[0:1] * lv[1:2], axis=1, keepdims=True))
               - jnp.exp(jnp.sum(lv[2:3] * lv[3:4], axis=1, keepdims=True)) + LAMBDA_INIT)
        od = o[:tq] - lam * o[tq:]
        od = od * lax.rsqrt(jnp.mean(od * od, axis=-1, keepdims=True) + EPS) * sub_ref[...]
        o_ref[...] = (od * (1.0 - LAMBDA_INIT)).astype(BF16)
    else:
        o_ref[...] = o.astype(BF16)


def _attention(q, k, v, batch, seq, heads, dk, diff, extra=()):
    tq, tk = ATT_TQ, ATT_TK
    nq = seq // tq
    rows = 2 * tq if diff else tq
    in_specs = [
        pl.BlockSpec((tq, 2 * LANES), lambda b, h, i: (b * nq + i, h)),
        pl.BlockSpec((seq, dk), lambda b, h, i: (b, h)),
        pl.BlockSpec((seq, LANES), lambda b, h, i: (b, h)),
    ]
    if diff:
        nbias, lamv, subw = extra
        in_specs += [
            pl.BlockSpec((1, 2, tq, tk), lambda b, h, i: (h, 0, 0, 0)),
            pl.BlockSpec(lamv.shape, lambda b, h, i: (0, 0)),
            pl.BlockSpec(subw.shape, lambda b, h, i: (0, 0)),
        ]
    return pl.pallas_call(
        functools.partial(_attn_kernel, diff=diff, tq=tq, tk=tk),
        grid=(batch, heads, nq),
        in_specs=in_specs,
        out_specs=pl.BlockSpec((tq, LANES), lambda b, h, i: (b * nq + i, h)),
        out_shape=jax.ShapeDtypeStruct((batch * seq, heads * LANES), BF16),
        scratch_shapes=[pltpu.VMEM((rows, LANES), F32)] * 3,
        compiler_params=_cparams(("parallel", "parallel", "arbitrary")),
        name="attn_diff" if diff else "attn_mla",
    )(q, k, v, *extra)


def _router_kernel(om_ref, od_ref, x_ref, wo_ref, fw_ref, rw_ref, rb_ref,
                   x1_ref, h_ref, route_ref, cnt_ref, carry_ref):
    tm = x_ref.shape[0]
    half = om_ref.shape[1]

    @pl.when(pl.program_id(0) == 0)
    def _():
        carry_ref[...] = jnp.zeros(carry_ref.shape, F32)

    y = (jnp.dot(om_ref[...], wo_ref[0:half, :], preferred_element_type=F32)
         + jnp.dot(od_ref[...], wo_ref[half:2 * half, :], preferred_element_type=F32))
    x1 = x_ref[...] + y
    x1_ref[...] = x1
    hh = x1 * lax.rsqrt(jnp.mean(x1 * x1, axis=-1, keepdims=True) + EPS) * fw_ref[...]
    for c in range(D_MODEL // LANES):
        h_ref[pl.ds(c, tm, stride=D_MODEL // LANES), :] = hh[:, c * LANES:(c + 1) * LANES]
    logits = jnp.dot(hh, rw_ref[...], preferred_element_type=F32, precision=lax.Precision.HIGHEST) + rb_ref[...]
    lane = lax.broadcasted_iota(jnp.int32, (tm, LANES), 1).astype(F32)
    work = jnp.where(lane < N_EXPERTS, logits, -jnp.inf)
    vals, idxs = [], []
    for _ in range(TOP_K):
        mx = jnp.max(work, axis=-1, keepdims=True)
        ix = jnp.min(jnp.where(work == mx, lane, float(LANES)), axis=-1, keepdims=True)
        vals.append(mx)
        idxs.append(ix)
        work = jnp.where(lane == ix, -jnp.inf, work)
    es = [jnp.exp(v - vals[0]) for v in vals]
    den = es[0] + es[1] + es[2] + es[3]
    onehot = jnp.zeros((tm, LANES), F32)
    for ix in idxs:
        onehot = onehot + jnp.where(lane == ix, 1.0, 0.0)
    r_i = lax.broadcasted_iota(jnp.int32, (tm, tm), 0)
    c_i = lax.broadcasted_iota(jnp.int32, (tm, tm), 1)
    tri = jnp.where(c_i < r_i, 1.0, 0.0).astype(BF16)
    prefix = jnp.dot(tri, onehot.astype(BF16), preferred_element_type=F32) + carry_ref[...]
    route = jnp.zeros((tm, LANES), F32)
    for kk in range(TOP_K):
        rank = jnp.sum(jnp.where(lane == idxs[kk], prefix, 0.0), axis=-1, keepdims=True)
        route = jnp.where(lane == kk, idxs[kk], route)
        route = jnp.where(lane == TOP_K + kk, es[kk] / den, route)
        route = jnp.where(lane == 2 * TOP_K + kk, rank, route)
    route_ref[...] = route
    carry = carry_ref[...] + jnp.sum(onehot, axis=0, keepdims=True)
    carry_ref[...] = carry
    cnt_ref[...] = carry


def _router(om, od, x2, wo_b, fw, rw_pad, rb_pad):
    T = x2.shape[0]
    tm = RT_TM
    row = lambda i: (i, 0)
    outs = [
        jax.ShapeDtypeStruct((T, D_MODEL), F32),
        jax.ShapeDtypeStruct((T * (D_MODEL // LANES), LANES), F32),
        jax.ShapeDtypeStruct((T, LANES), F32),
        jax.ShapeDtypeStruct((1, LANES), F32),
    ]
    return pl.pallas_call(
        _router_kernel,
        grid=(T // tm,),
        in_specs=[
            pl.BlockSpec((tm, om.shape[1]), row), pl.BlockSpec((tm, od.shape[1]), row),
            pl.BlockSpec((tm, D_MODEL), row),
            _const_spec(wo_b.shape), _const_spec(fw.shape), _const_spec(rw_pad.shape), _const_spec(rb_pad.shape),
        ],
        out_specs=[pl.BlockSpec((tm, D_MODEL), row), pl.BlockSpec((tm * (D_MODEL // LANES), LANES), row),
                   pl.BlockSpec((tm, LANES), row), pl.BlockSpec((1, LANES), lambda i: (0, 0))],
        out_shape=outs,
        scratch_shapes=[pltpu.VMEM((1, LANES), F32)],
        compiler_params=_cparams(("arbitrary",)),
        name="router",
    )(om, od, x2, wo_b, fw, rw_pad, rb_pad)


def _row_copy(src_hbm, idx, dst_ref, r, sem):
    return pltpu.make_async_copy(src_hbm.at[pl.ds(idx, 1), :], dst_ref.at[pl.ds(r, 1), :], sem)


def _token_copy(h_hbm, tok, dst_ref, r, sem):
    return pltpu.make_async_copy(h_hbm.at[pl.ds(tok * TOK_ROWS, TOK_ROWS), :],
                                 dst_ref.at[pl.ds(r * TOK_ROWS, TOK_ROWS), :], sem)


def _dispatch_kernel(act_ref, tok_ref, nxt_ref, h_hbm, o_ref, buf_ref, sem):
    i = pl.program_id(0)
    n = pl.num_programs(0)
    rows = o_ref.shape[0]

    def gather(idx_ref, slot):
        def issue(r, c):
            _token_copy(h_hbm, idx_ref[0, 0, r], buf_ref.at[slot], r, sem.at[slot]).start()
            return c
        lax.fori_loop(0, rows, issue, 0, unroll=8)

    @pl.when((i == 0) & (act_ref[0] > 0))
    def _():
        gather(tok_ref, 0)

    @pl.when((i + 1 < n) & (act_ref[jnp.minimum(i + 1, n - 1)] > 0))
    def _():
        gather(nxt_ref, (i + 1) % 2)

    @pl.when(act_ref[i] > 0)
    def _():
        slot = i % 2
        pltpu.make_async_copy(h_hbm.at[pl.ds(0, rows * TOK_ROWS), :], buf_ref.at[slot], sem.at[slot]).wait()
        for c in range(TOK_ROWS):
            o_ref[:, c * LANES:(c + 1) * LANES] = buf_ref[slot, pl.ds(c, rows, stride=TOK_ROWS), :].astype(BF16)

    @pl.when(act_ref[i] == 0)
    def _():
        o_ref[...] = jnp.zeros(o_ref.shape, o_ref.dtype)


def _dispatch(blk_act, slot_tok3, h):
    nblk, _, rows = slot_tok3.shape
    grid_spec = pltpu.PrefetchScalarGridSpec(
        num_scalar_prefetch=1,
        grid=(nblk,),
        in_specs=[pl.BlockSpec((1, 1, rows), lambda i, act: (i, 0, 0), memory_space=pltpu.SMEM),
                  pl.BlockSpec((1, 1, rows), lambda i, act: (jnp.minimum(i + 1, nblk - 1), 0, 0),
                               memory_space=pltpu.SMEM),
                  pl.BlockSpec(memory_space=pl.ANY)],
        out_specs=pl.BlockSpec((rows, D_MODEL), lambda i, act: (i, 0)),
        scratch_shapes=[pltpu.VMEM((2, rows * TOK_ROWS, LANES), F32), pltpu.SemaphoreType.DMA((2,))],
    )
    return pl.pallas_call(
        _dispatch_kernel,
        grid_spec=grid_spec,
        out_shape=jax.ShapeDtypeStruct((nblk * rows, D_MODEL), BF16),
        compiler_params=_cparams(("arbitrary",)),
        name="dispatch",
    )(blk_act, slot_tok3, slot_tok3, h)


def _cache_weights(se_ref, m, w_ref, wb_ref):
    prev = se_ref[jnp.maximum(m - 1, 0)]

    @pl.when((m == 0) | (se_ref[m] != prev))
    def _():
        wb_ref[...] = w_ref[0].astype(BF16)


def _for_row_count(nsub, o_ref, compute):
    @pl.when(nsub == 0)
    def _():
        o_ref[...] = jnp.zeros(o_ref.shape, o_ref.dtype)

    for s in range(1, NSUB + 1):
        @pl.when(nsub == s)
        def _(s=s):
            rows = s * SUB
            compute(rows)
            if rows < o_ref.shape[0]:
                o_ref[rows:, :] = jnp.zeros((o_ref.shape[0] - rows, o_ref.shape[1]), o_ref.dtype)


def _expert_a_kernel(se_ref, ns_ref, nu_ref, x_ref, w_ref, b_ref, o_ref, wb_ref):
    m = pl.program_id(1)
    tn = w_ref.shape[2]
    nsub = ns_ref[m]

    @pl.when(nsub > 0)
    def _():
        _cache_weights(se_ref, m, w_ref, wb_ref)

    def compute(rows):
        r_i = lax.broadcasted_iota(jnp.int32, (MXU_DIM, LANES), 0)
        c_i = lax.broadcasted_iota(jnp.int32, (MXU_DIM, LANES), 1)
        sel = jnp.where(r_i == 2 * c_i, 1.0, 0.0).astype(BF16)
        x = x_ref[0:rows, :]
        for c in range(tn // MXU_DIM):
            cols = slice(c * MXU_DIM, (c + 1) * MXU_DIM)
            gu = jnp.dot(x, wb_ref[:, cols], preferred_element_type=F32) + b_ref[0, :, cols]
            parts = []
            for s in range(MXU_DIM // LANES):
                g = gu[:, s * LANES:(s + 1) * LANES]
                gate = jnp.minimum(g, SWIGLU_LIMIT)
                up1 = jnp.clip(g, -SWIGLU_LIMIT, SWIGLU_LIMIT) + 1.0
                act = gate * jax.nn.sigmoid(SWIGLU_ALPHA * gate)
                parts.append((act * pltpu.roll(up1, LANES - 1, 1)).astype(BF16))
            pair = jnp.concatenate(parts, axis=1)
            o_ref[0:rows, c * LANES:(c + 1) * LANES] = jnp.dot(
                pair, sel, preferred_element_type=F32).astype(BF16)

    _for_row_count(nsub, o_ref, compute)


def _expert_b_kernel(se_ref, ns_ref, nu_ref, a_ref, w_ref, b_ref, o_ref, wb_ref):
    m = pl.program_id(1)
    nsub = ns_ref[m]

    @pl.when(nsub > 0)
    def _():
        _cache_weights(se_ref, m, w_ref, wb_ref)

    def compute(rows):
        o_ref[0:rows, :] = jnp.dot(a_ref[0:rows, :], wb_ref[...], preferred_element_type=F32) + b_ref[0]

    _for_row_count(nsub, o_ref, compute)


def _expert_call(kernel, name, sb_e, sb_nsub, n_used, xin, w, b3, out_cols, out_dtype, out_tn):
    n_sb = sb_e.shape[0]
    tm, tn = MOE_TM, MOE_TN
    kdim = w.shape[1]
    nj = w.shape[2] // tn

    def blk(m, nu):
        return jnp.minimum(m, nu[0] - 1)

    grid_spec = pltpu.PrefetchScalarGridSpec(
        num_scalar_prefetch=3,
        grid=(nj, n_sb),
        in_specs=[
            pl.BlockSpec((tm, kdim), lambda j, m, se, ns, nu: (blk(m, nu), 0)),
            pl.BlockSpec((1, kdim, tn), lambda j, m, se, ns, nu: (se[blk(m, nu)], 0, j)),
            pl.BlockSpec((1, 1, tn), lambda j, m, se, ns, nu: (se[blk(m, nu)], 0, j)),
        ],
        out_specs=pl.BlockSpec((tm, out_tn), lambda j, m, se, ns, nu: (m, j)),
        scratch_shapes=[pltpu.VMEM((kdim, tn), BF16)],
    )
    return pl.pallas_call(
        kernel,
        grid_spec=grid_spec,
        out_shape=jax.ShapeDtypeStruct((n_sb * tm, out_cols), out_dtype),
        compiler_params=_cparams(("arbitrary", "arbitrary")),
        name=name,
    )(sb_e, sb_nsub, n_used, xin, w, b3)


def _combine_kernel(dest_ref, y_hbm, x1_ref, route_ref, o_ref, buf_ref, sem):
    tm = o_ref.shape[0]

    def issue(r, c):
        for kk in range(TOP_K):
            _row_copy(y_hbm, dest_ref[0, 0, r * TOP_K + kk], buf_ref.at[kk], r, sem).start()
        return c

    lax.fori_loop(0, tm, issue, 0, unroll=4)
    for kk in range(TOP_K):
        pltpu.make_async_copy(y_hbm.at[pl.ds(0, tm), :], buf_ref.at[kk], sem).wait()
    route = route_ref[...]
    out = x1_ref[...]
    for kk in range(TOP_K):
        out = out + route[:, TOP_K + kk:TOP_K + kk + 1] * buf_ref[kk]
    o_ref[...] = out


def _combine(dest3, y, x1, route):
    T = x1.shape[0]
    tm = CMB_TM
    row = lambda i: (i, 0)
    return pl.pallas_call(
        _combine_kernel,
        grid=(T // tm,),
        in_specs=[pl.BlockSpec((1, 1, tm * TOP_K), lambda i: (i, 0, 0), memory_space=pltpu.SMEM),
                  pl.BlockSpec(memory_space=pl.ANY),
                  pl.BlockSpec((tm, D_MODEL), row),
                  pl.BlockSpec((tm, LANES), row)],
        out_specs=pl.BlockSpec((tm, D_MODEL), row),
        out_shape=jax.ShapeDtypeStruct((T, D_MODEL), F32),
        scratch_shapes=[pltpu.VMEM((TOP_K, tm, D_MODEL), F32), pltpu.SemaphoreType.DMA(())],
        compiler_params=_cparams(("arbitrary",)),
        name="combine",
    )(dest3, y, x1, route)


def _pad_lanes(v, width=LANES):
    return jnp.pad(v, [(0, 0)] * (v.ndim - 1) + [(0, width - v.shape[-1])])


def _swap_halves(v):
    half = v.shape[-1] // 2
    return jnp.concatenate([v[..., half:], v[..., :half]], axis=-1)


def _token_mixers(x2, B, S, attn_norm_w, w_in, q_a_norm_w, w_uq, kv_a_norm_w, w_ukv, mla_q_norm_w, mla_k_norm_w,
                  diff_q_norm_w, diff_k_norm_w, lambda_q1, lambda_k1, lambda_q2, lambda_k2, diff_subln_w, rel_bias):
    L = 0
    win = w_in[L]
    sp = np.cumsum([MLA_Q_RANK, MLA_KV_RANK, MLA_ROPE])
    w_kr = win[:, sp[1]:sp[2]]
    win_r = jnp.concatenate(
        [win[:, :sp[1]], _pad_lanes(w_kr), _pad_lanes(_swap_halves(w_kr)), win[:, sp[2]:]], axis=1).astype(BF16)
    qk = MLA_NOPE + MLA_ROPE
    wuq3 = w_uq[L].reshape(MLA_Q_RANK, MLA_HEADS, qk)
    wuq_rope = wuq3[:, :, MLA_NOPE:]
    wuq_r = jnp.concatenate([
        wuq3[:, :, :MLA_NOPE].reshape(MLA_Q_RANK, -1),
        _pad_lanes(wuq_rope).reshape(MLA_Q_RANK, -1),
        _pad_lanes(_swap_halves(wuq_rope)).reshape(MLA_Q_RANK, -1)], axis=1).astype(BF16)
    wukv_b = w_ukv[L].astype(BF16)
    qn_w, kn_w = mla_q_norm_w[L], mla_k_norm_w[L]
    vecs = jnp.stack([
        qn_w[:MLA_NOPE], _pad_lanes(qn_w[MLA_NOPE:]), _pad_lanes(_swap_halves(qn_w[MLA_NOPE:])),
        kn_w[:MLA_NOPE], _pad_lanes(kn_w[MLA_NOPE:]), _pad_lanes(_swap_halves(kn_w[MLA_NOPE:])),
        jnp.tile(diff_q_norm_w[L], 2), jnp.tile(diff_k_norm_w[L], 2)])
    lamv = _pad_lanes(jnp.stack([lambda_q1[L], lambda_k1[L], lambda_q2[L], lambda_k2[L]]))

    inv_freq = np.float32(ROPE_THETA) ** (-np.arange(0, MLA_ROPE, 2, dtype=np.float32) / np.float32(MLA_ROPE))
    ang = np.arange(S, dtype=np.float32)[:, None] * inv_freq[None, :].astype(np.float32)
    cos, sin = np.cos(ang).astype(np.float32), np.sin(ang).astype(np.float32)
    zpad = np.zeros((S, LANES - MLA_ROPE), np.float32)
    cos128 = jnp.asarray(np.concatenate([cos, cos, zpad], axis=1))
    sin128 = jnp.asarray(np.concatenate([-sin, sin, zpad], axis=1))

    qm, km, vm, qd, kd, vd = _prep(x2, attn_norm_w[L][None], win_r, q_a_norm_w[L][None], wuq_r,
                                   kv_a_norm_w[L][None], wukv_b, vecs, cos128, sin128, S)
    nbias = _relbias(rel_bias, ATT_TQ, ATT_TK)
    om = _attention(qm, km, vm, B, S, MLA_HEADS, 2 * LANES, diff=False)
    od = _attention(qd, kd, vd, B, S, DIFF_HEADS, LANES, diff=True,
                    extra=(nbias, lamv, diff_subln_w[L][None]))
    return om, od


def _moe_block(om, od, x2, w_o, ffn_norm_w, router_w, router_b, w_gate_up, b_gate_up, w_down, b_down):
    L = 0
    T, D = x2.shape
    wo_b = w_o[L].astype(BF16)
    rw_pad = _pad_lanes(router_w[L])
    rb_pad = _pad_lanes(router_b[L][None, :])

    x1, h, route, cnt = _router(om, od, x2, wo_b, ffn_norm_w[L][None], rw_pad, rb_pad)

    tm = MOE_TM
    n_sb = -(-T * TOP_K // tm) + N_EXPERTS
    idx = route[:, 0:TOP_K].astype(jnp.int32)
    rank = route[:, 2 * TOP_K:3 * TOP_K].astype(jnp.int32)
    counts = cnt[0, :N_EXPERTS].astype(jnp.int32)
    padded = (counts + tm - 1) // tm * tm
    pad_ends = jnp.cumsum(padded)
    pad_starts = pad_ends - padded
    dest = pad_starts[idx] + rank
    tok = jnp.broadcast_to(jnp.arange(T, dtype=jnp.int32)[:, None], (T, TOP_K))
    slot_tok = jnp.zeros((n_sb * tm,), jnp.int32).at[dest.reshape(-1)].set(tok.reshape(-1))
    sb_start = jnp.arange(n_sb, dtype=jnp.int32) * tm
    sb_e = jnp.minimum(jnp.sum((pad_ends[None, :] <= sb_start[:, None]).astype(jnp.int32), axis=1), N_EXPERTS - 1)
    sb_valid = jnp.clip(pad_starts[sb_e] + counts[sb_e] - sb_start, 0, tm)
    sb_valid = jnp.where(sb_start < pad_ends[-1], sb_valid, 0)
    sb_nsub = (sb_valid + SUB - 1) // SUB
    n_used = (pad_ends[-1] // tm).astype(jnp.int32)[None]
    blk_act = (jnp.arange(n_sb * NSUB, dtype=jnp.int32) % NSUB < jnp.repeat(sb_nsub, NSUB)).astype(jnp.int32)

    xs = _dispatch(blk_act, slot_tok.reshape(n_sb * NSUB, 1, SUB), h)
    act = _expert_call(_expert_a_kernel, "expert_a", sb_e, sb_nsub, n_used, xs, w_gate_up[L],
                       b_gate_up[L][:, None, :], D_FF, BF16, MOE_TN // 2)
    y = _expert_call(_expert_b_kernel, "expert_b", sb_e, sb_nsub, n_used, act, w_down[L],
                     b_down[L][:, None, :], D, F32, MOE_TN)
    return _combine(dest.reshape(T // CMB_TM, 1, CMB_TM * TOP_K), y, x1, route)


def kernel(x, attn_norm_w, w_in, q_a_norm_w, w_uq, kv_a_norm_w, w_ukv, mla_q_norm_w, mla_k_norm_w, diff_q_norm_w, diff_k_norm_w, lambda_q1, lambda_k1, lambda_q2, lambda_k2, diff_subln_w, w_o, ffn_norm_w, router_w, router_b, w_gate_up, b_gate_up, w_down, b_down, rel_bias):
    B, S, D = x.shape
    x2 = x.reshape(B * S, D)
    om, od = _token_mixers(x2, B, S, attn_norm_w, w_in, q_a_norm_w, w_uq, kv_a_norm_w, w_ukv, mla_q_norm_w,
                           mla_k_norm_w, diff_q_norm_w, diff_k_norm_w, lambda_q1, lambda_k1, lambda_q2, lambda_k2,
                           diff_subln_w, rel_bias)
    out = _moe_block(om, od, x2, w_o, ffn_norm_w, router_w, router_b, w_gate_up, b_gate_up, w_down, b_down)
    return out.reshape(B, S, D)
```

```python
import functools
import math

import jax
import jax.numpy as jnp
import numpy as np
from jax import lax
from jax.experimental import pallas as pl
from jax.experimental.pallas import tpu as pltpu

D_MODEL = 2048
CHUNK = 64
MLA_HEADS = 8
MLA_NOPE = 128
MLA_ROPE = 64
MLA_V = 128
MLA_Q_RANK = 384
MLA_KV_RANK = 256
ROPE_THETA = 10000.0
DIFF_HEADS = 8
DIFF_DH = 64
DIFF_V = 128
REL_BUCKETS = 32
REL_MAX_DIST = 128
N_EXPERTS = 32
TOP_K = 4
D_FF = 2048
SWIGLU_LIMIT = 7.0
SWIGLU_ALPHA = 1.702
EPS = 1e-6
LAMBDA_INIT = 0.8 - 0.6 * math.exp(-0.3 * 0)

LANES = 128
MXU_DIM = 256
VMEM_LIMIT = 48 * 1024 * 1024

PREP_TM = 256
ATT_TQ = 512
ATT_TK = 512
ATT_HP = 4
RT_TM = 512
SUB = 256
MOE_TM = 1024
NSUB = MOE_TM // SUB
MOE_TN = 1024
CMB_TM = 128
TOK_ROWS = D_MODEL // LANES
TOK_PITCH = TOK_ROWS + 1

LOG2E = math.log2(math.e)
NEG = -1e30
F32 = jnp.float32
BF16 = jnp.bfloat16

_C_CQ = 0
_C_CKV = MLA_Q_RANK
_C_KR = _C_CKV + MLA_KV_RANK
_C_KRR = _C_KR + LANES
_C_MLA_END = _C_KRR + LANES
_C_DQ = _C_MLA_END
_C_DK = _C_DQ + DIFF_HEADS * 2 * DIFF_DH
_C_DV = _C_DK + DIFF_HEADS * 2 * DIFF_DH
_C_END = _C_DV + DIFF_HEADS * DIFF_V


def _cparams(sem):
    return pltpu.CompilerParams(dimension_semantics=sem, vmem_limit_bytes=VMEM_LIMIT)


def _const_spec(shape):
    nd = len(shape)
    return pl.BlockSpec(shape, lambda *_: (0,) * nd, pipeline_mode=pl.Buffered(1))


def _prep_kernel(x_ref, anw_ref, win_ref, qaw_ref, wuq_ref, kvaw_ref, wukv_ref, vec_ref, cos_ref, sin_ref,
                 qm_ref, km_ref, vm_ref, qd_ref, kd_ref, vd_ref):
    xf = x_ref[...]
    inv = lax.rsqrt(jnp.mean(xf * xf, axis=-1, keepdims=True) + EPS)
    h = (xf * inv * anw_ref[...]).astype(BF16)

    def rms(v, w):
        return v * lax.rsqrt(jnp.mean(v * v, axis=-1, keepdims=True) + EPS) * w

    pm = jnp.dot(h, win_ref[:, _C_CQ:_C_MLA_END], preferred_element_type=F32)
    cqn = rms(pm[:, _C_CQ:_C_CKV], qaw_ref[...]).astype(BF16)
    ckn = rms(pm[:, _C_CKV:_C_KR], kvaw_ref[...]).astype(BF16)
    kr = pm[:, _C_KR:_C_KRR]
    krr = pm[:, _C_KRR:_C_MLA_END]
    qall = jnp.dot(cqn, wuq_ref[...], preferred_element_type=F32)
    kvall = jnp.dot(ckn, wukv_ref[...], preferred_element_type=F32)

    vec = vec_ref[...]
    wqn, wqr, wqrr = vec[0:1], vec[1:2], vec[2:3]
    wkn, wkr, wkrr = vec[3:4], vec[4:5], vec[5:6]
    wdq, wdk = vec[6:7], vec[7:8]
    cos = cos_ref[...]
    sin = sin_ref[...]
    cq_r, sq_r = wqr * cos, wqrr * sin
    ck_r, sk_r = wkr * cos, wkrr * sin
    kr2 = kr * kr
    k_rope_base = kr * ck_r + krr * sk_r
    q_scale = (MLA_NOPE + MLA_ROPE) ** -0.5 * LOG2E
    inv_qk = 1.0 / (MLA_NOPE + MLA_ROPE)
    nq = MLA_HEADS * LANES
    for hh in range(MLA_HEADS):
        a = hh * LANES
        qn = qall[:, a:a + LANES]
        qr = qall[:, nq + a:nq + a + LANES]
        qrr = qall[:, 2 * nq + a:2 * nq + a + LANES]
        iq = lax.rsqrt(jnp.sum(qn * qn + qr * qr, axis=-1, keepdims=True) * inv_qk + EPS) * q_scale
        qm_ref[:, 2 * a:2 * a + LANES] = (qn * iq * wqn).astype(BF16)
        qm_ref[:, 2 * a + LANES:2 * a + 2 * LANES] = ((qr * cq_r + qrr * sq_r) * iq).astype(BF16)
        kn = kvall[:, 2 * a:2 * a + LANES]
        ik = lax.rsqrt(jnp.sum(kn * kn + kr2, axis=-1, keepdims=True) * inv_qk + EPS)
        km_ref[:, 2 * a:2 * a + LANES] = (kn * ik * wkn).astype(BF16)
        km_ref[:, 2 * a + LANES:2 * a + 2 * LANES] = (k_rope_base * ik).astype(BF16)
        vm_ref[:, a:a + LANES] = kvall[:, 2 * a + LANES:2 * a + 2 * LANES].astype(BF16)

    dq = jnp.dot(h, win_ref[:, _C_DQ:_C_DK], preferred_element_type=F32)
    dk = jnp.dot(h, win_ref[:, _C_DK:_C_DV], preferred_element_type=F32)
    vd_ref[...] = jnp.dot(h, win_ref[:, _C_DV:_C_END], preferred_element_type=F32).astype(BF16)
    lane = lax.broadcasted_iota(jnp.int32, (xf.shape[0], LANES), 1)
    lo = lane < DIFF_DH
    d_scale = DIFF_DH ** -0.5 * LOG2E
    inv_dh = 1.0 / DIFF_DH

    def half_norm(v):
        sq = v * v
        s1 = jnp.sum(jnp.where(lo, sq, 0.0), axis=-1, keepdims=True)
        s2 = jnp.sum(jnp.where(lo, 0.0, sq), axis=-1, keepdims=True)
        return v * jnp.where(lo, lax.rsqrt(s1 * inv_dh + EPS), lax.rsqrt(s2 * inv_dh + EPS))

    for hh in range(DIFF_HEADS):
        a = hh * LANES
        qn = half_norm(dq[:, a:a + LANES]) * (wdq * d_scale)
        qd_ref[:, 2 * a:2 * a + LANES] = jnp.where(lo, qn, 0.0).astype(BF16)
        qd_ref[:, 2 * a + LANES:2 * a + 2 * LANES] = jnp.where(lo, 0.0, qn).astype(BF16)
        kd_ref[:, a:a + LANES] = (half_norm(dk[:, a:a + LANES]) * wdk).astype(BF16)


def _prep(x2, anw, win_r, qaw, wuq_r, kvaw, wukv_b, vecs, cos128, sin128, seq):
    T = x2.shape[0]
    tm = PREP_TM
    nseq = seq // tm
    row = lambda i: (i, 0)
    outs = [
        jax.ShapeDtypeStruct((T, MLA_HEADS * 2 * LANES), BF16),
        jax.ShapeDtypeStruct((T, MLA_HEADS * 2 * LANES), BF16),
        jax.ShapeDtypeStruct((T, MLA_HEADS * MLA_V), BF16),
        jax.ShapeDtypeStruct((T, DIFF_HEADS * 2 * LANES), BF16),
        jax.ShapeDtypeStruct((T, DIFF_HEADS * LANES), BF16),
        jax.ShapeDtypeStruct((T, DIFF_HEADS * DIFF_V), BF16),
    ]
    return pl.pallas_call(
        _prep_kernel,
        grid=(T // tm,),
        in_specs=[
            pl.BlockSpec((tm, D_MODEL), row),
            _const_spec(anw.shape), _const_spec(win_r.shape), _const_spec(qaw.shape), _const_spec(wuq_r.shape),
            _const_spec(kvaw.shape), _const_spec(wukv_b.shape), _const_spec(vecs.shape),
            pl.BlockSpec((tm, LANES), lambda i: (i % nseq, 0)),
            pl.BlockSpec((tm, LANES), lambda i: (i % nseq, 0)),
        ],
        out_specs=[pl.BlockSpec((tm, o.shape[1]), row) for o in outs],
        out_shape=outs,
        compiler_params=_cparams(("parallel",)),
        name="prep",
    )(x2, anw, win_r, qaw, wuq_r, kvaw, wukv_b, vecs, cos128, sin128)


def _relbias_kernel(rb_ref, o_ref, *, tq, tk):
    hh = pl.program_id(0)
    which = pl.program_id(1)
    row = lax.broadcasted_iota(jnp.int32, (tq, tk), 0)
    col = lax.broadcasted_iota(jnp.int32, (tq, tk), 1)
    rel = col - which * tk - row
    nb = REL_BUCKETS // 2
    max_exact = nb // 2
    ret = jnp.where(rel > 0, nb, 0)
    n = jnp.abs(rel)
    nf = jnp.maximum(n, 1).astype(F32)
    large = max_exact + (jnp.log(nf / max_exact) / math.log(REL_MAX_DIST / max_exact)
                         * (nb - max_exact)).astype(jnp.int32)
    large = jnp.minimum(large, nb - 1)
    bucket = ret + jnp.where(n < max_exact, n, large)
    bias = jnp.zeros((tq, tk), F32)
    for b in range(REL_BUCKETS):
        bias = jnp.where(bucket == b, rb_ref[b, hh], bias)
    bias = (bias - rb_ref[nb - 1, hh]) * LOG2E
    allowed = ((col // CHUNK) <= (row // CHUNK)) | (which > 0)
    o_ref[0, 0] = jnp.where(allowed, bias, NEG)


def _relbias(rel_bias, tq, tk):
    return pl.pallas_call(
        functools.partial(_relbias_kernel, tq=tq, tk=tk),
        grid=(DIFF_HEADS, 2),
        in_specs=[pl.BlockSpec(memory_space=pltpu.SMEM)],
        out_specs=pl.BlockSpec((1, 1, tq, tk), lambda h, w: (h, w, 0, 0)),
        out_shape=jax.ShapeDtypeStruct((DIFF_HEADS, 2, tq, tk), F32),
        compiler_params=_cparams(("parallel", "parallel")),
        name="relbias",
    )(rel_bias)


def _attn_kernel(*refs, diff, tq, tk):
    if diff:
        q_ref, k_ref, v_ref, nb_ref, lam_ref, sub_ref, o_ref, m_ref, l_ref, acc_ref = refs
    else:
        q_ref, k_ref, v_ref, o_ref, m_ref, l_ref, acc_ref = refs
    i = pl.program_id(2)
    hp = o_ref.shape[1] // LANES
    dk = k_ref.shape[1] // hp
    qs = []
    for a in range(hp):
        blk = q_ref[:, a * 2 * LANES:(a + 1) * 2 * LANES]
        qs.append(jnp.concatenate([blk[:, :LANES], blk[:, LANES:]], axis=0) if diff else blk)
    m_ref[...] = jnp.full(m_ref.shape, NEG, F32)
    l_ref[...] = jnp.zeros(l_ref.shape, F32)
    acc_ref[...] = jnp.zeros(acc_ref.shape, F32)

    def step(j, biases):
        start = pl.multiple_of(j * tk, tk)
        for a in range(hp):
            ks = k_ref[pl.ds(start, tk), a * dk:(a + 1) * dk]
            vs = v_ref[pl.ds(start, tk), a * LANES:(a + 1) * LANES]
            s = lax.dot_general(qs[a], ks, (((1,), (1,)), ((), ())), preferred_element_type=F32)
            if biases is not None:
                s = s + biases[a]
            m_prev = m_ref[a]
            m_new = jnp.maximum(m_prev, jnp.max(s, axis=1, keepdims=True))
            alpha = jnp.exp2(m_prev - m_new)
            p = jnp.exp2(s - jnp.concatenate([m_new] * (tk // LANES), axis=1))
            psum = p[:, 0:LANES]
            for c in range(1, tk // LANES):
                psum = psum + p[:, c * LANES:(c + 1) * LANES]
            l_ref[a] = alpha * l_ref[a] + psum
            acc_ref[a] = alpha * acc_ref[a] + jnp.dot(p.astype(BF16), vs, preferred_element_type=F32)
            m_ref[a] = m_new

    def far_body(j, c):
        step(j, None)
        return c

    if diff:
        lax.fori_loop(0, jnp.maximum(i - 1, 0), far_body, 0)

        @pl.when(i > 0)
        def _():
            step(i - 1, [jnp.concatenate([nb_ref[a, 1]] * 2, axis=0) for a in range(hp)])

        step(i, [jnp.concatenate([nb_ref[a, 0]] * 2, axis=0) for a in range(hp)])
    else:
        lax.fori_loop(0, i, far_body, 0)
        row = lax.broadcasted_iota(jnp.int32, (tq, tk), 0)
        col = lax.broadcasted_iota(jnp.int32, (tq, tk), 1)
        step(i, [jnp.where((col // CHUNK) <= (row // CHUNK), 0.0, NEG)] * hp)

    for a in range(hp):
        o = acc_ref[a] / jnp.sum(l_ref[a], axis=1, keepdims=True)
        if diff:
            lv = lam_ref[...]
            lam = (jnp.exp(jnp.sum(lv[0:1] * lv[1:2], axis=1, keepdims=True))
                   - jnp.exp(jnp.sum(lv[2:3] * lv[3:4], axis=1, keepdims=True)) + LAMBDA_INIT)
            od = o[:tq] - lam * o[tq:]
            od = od * lax.rsqrt(jnp.mean(od * od, axis=-1, keepdims=True) + EPS) * sub_ref[...]
            o_ref[:, a * LANES:(a + 1) * LANES] = (od * (1.0 - LAMBDA_INIT)).astype(BF16)
        else:
            o_ref[:, a * LANES:(a + 1) * LANES] = o.astype(BF16)


def _attention(q, k, v, batch, seq, heads, dk, diff, extra=()):
    tq, tk, hp = ATT_TQ, ATT_TK, ATT_HP
    nq = seq // tq
    rows = 2 * tq if diff else tq
    in_specs = [
        pl.BlockSpec((tq, hp * 2 * LANES), lambda b, h, i: (b * nq + i, h)),
        pl.BlockSpec((seq, hp * dk), lambda b, h, i: (b, h), pipeline_mode=pl.Buffered(1)),
        pl.BlockSpec((seq, hp * LANES), lambda b, h, i: (b, h), pipeline_mode=pl.Buffered(1)),
    ]
    if diff:
        nbias, lamv, subw = extra
        in_specs += [
            pl.BlockSpec((hp, 2, tq, tk), lambda b, h, i: (h, 0, 0, 0), pipeline_mode=pl.Buffered(1)),
            pl.BlockSpec(lamv.shape, lambda b, h, i: (0, 0)),
            pl.BlockSpec(subw.shape, lambda b, h, i: (0, 0)),
        ]
    return pl.pallas_call(
        functools.partial(_attn_kernel, diff=diff, tq=tq, tk=tk),
        grid=(batch, heads // hp, nq),
        in_specs=in_specs,
        out_specs=pl.BlockSpec((tq, hp * LANES), lambda b, h, i: (b * nq + i, h)),
        out_shape=jax.ShapeDtypeStruct((batch * seq, heads * LANES), BF16),
        scratch_shapes=[pltpu.VMEM((hp, rows, LANES), F32)] * 3,
        compiler_params=_cparams(("parallel", "parallel", "arbitrary")),
        name="attn_diff" if diff else "attn_mla",
    )(q, k, v, *extra)


def _router_kernel(om_ref, od_ref, x_ref, wo_ref, fw_ref, rw_ref, rb_ref,
                   x1_ref, h_ref, route_ref, cnt_ref, carry_ref):
    tm = x_ref.shape[0]
    half = om_ref.shape[1]

    @pl.when(pl.program_id(0) == 0)
    def _():
        carry_ref[...] = jnp.zeros(carry_ref.shape, F32)

    y = (jnp.dot(om_ref[...], wo_ref[0:half, :], preferred_element_type=F32)
         + jnp.dot(od_ref[...], wo_ref[half:2 * half, :], preferred_element_type=F32))
    x1 = x_ref[...] + y
    x1_ref[...] = x1
    hh = x1 * lax.rsqrt(jnp.mean(x1 * x1, axis=-1, keepdims=True) + EPS) * fw_ref[...]
    for c in range(D_MODEL // LANES):
        h_ref[pl.ds(c, tm, stride=D_MODEL // LANES), :] = hh[:, c * LANES:(c + 1) * LANES]
    rw = rw_ref[...]
    rw_hi = rw.astype(BF16)
    rw_lo = (rw - rw_hi.astype(F32)).astype(BF16)
    hh_hi = hh.astype(BF16)
    hh_lo = (hh - hh_hi.astype(F32)).astype(BF16)
    logits = (jnp.dot(hh_hi, rw_hi, preferred_element_type=F32) + jnp.dot(hh_lo, rw_hi, preferred_element_type=F32)
              + jnp.dot(hh_hi, rw_lo, preferred_element_type=F32)) + rb_ref[...]
    lane = lax.broadcasted_iota(jnp.int32, (tm, LANES), 1).astype(F32)
    work = jnp.where(lane < N_EXPERTS, logits, -jnp.inf)
    vals, idxs = [], []
    for _ in range(TOP_K):
        mx = jnp.max(work, axis=-1, keepdims=True)
        ix = jnp.min(jnp.where(work == mx, lane, float(LANES)), axis=-1, keepdims=True)
        vals.append(mx)
        idxs.append(ix)
        work = jnp.where(lane == ix, -jnp.inf, work)
    es = [jnp.exp(v - vals[0]) for v in vals]
    den = es[0] + es[1] + es[2] + es[3]
    onehot = jnp.zeros((tm, LANES), F32)
    for ix in idxs:
        onehot = onehot + jnp.where(lane == ix, 1.0, 0.0)
    r_i = lax.broadcasted_iota(jnp.int32, (tm, tm), 0)
    c_i = lax.broadcasted_iota(jnp.int32, (tm, tm), 1)
    tri = jnp.where(c_i < r_i, 1.0, 0.0).astype(BF16)
    prefix = jnp.dot(tri, onehot.astype(BF16), preferred_element_type=F32) + carry_ref[...]
    route = jnp.zeros((tm, LANES), F32)
    for kk in range(TOP_K):
        rank = jnp.sum(jnp.where(lane == idxs[kk], prefix, 0.0), axis=-1, keepdims=True)
        route = jnp.where(lane == kk, idxs[kk], route)
        route = jnp.where(lane == TOP_K + kk, es[kk] / den, route)
        route = jnp.where(lane == 2 * TOP_K + kk, rank, route)
    route_ref[...] = route
    carry = carry_ref[...] + jnp.sum(onehot, axis=0, keepdims=True)
    carry_ref[...] = carry
    cnt_ref[...] = carry


def _router(om, od, x2, wo_b, fw, rw_pad, rb_pad):
    T = x2.shape[0]
    tm = RT_TM
    row = lambda i: (i, 0)
    outs = [
        jax.ShapeDtypeStruct((T, D_MODEL), F32),
        jax.ShapeDtypeStruct((T * (D_MODEL // LANES), LANES), F32),
        jax.ShapeDtypeStruct((T, LANES), F32),
        jax.ShapeDtypeStruct((1, LANES), F32),
    ]
    return pl.pallas_call(
        _router_kernel,
        grid=(T // tm,),
        in_specs=[
            pl.BlockSpec((tm, om.shape[1]), row), pl.BlockSpec((tm, od.shape[1]), row),
            pl.BlockSpec((tm, D_MODEL), row),
            _const_spec(wo_b.shape), _const_spec(fw.shape), _const_spec(rw_pad.shape), _const_spec(rb_pad.shape),
        ],
        out_specs=[pl.BlockSpec((tm, D_MODEL), row), pl.BlockSpec((tm * (D_MODEL // LANES), LANES), row),
                   pl.BlockSpec((tm, LANES), row), pl.BlockSpec((1, LANES), lambda i: (0, 0))],
        out_shape=outs,
        scratch_shapes=[pltpu.VMEM((1, LANES), F32)],
        compiler_params=_cparams(("arbitrary",)),
        name="router",
    )(om, od, x2, wo_b, fw, rw_pad, rb_pad)


def _row_copy(src_hbm, idx, dst_ref, r, sem):
    return pltpu.make_async_copy(src_hbm.at[pl.ds(idx, 1), :], dst_ref.at[pl.ds(r, 1), :], sem)


def _token_copy(h_hbm, tok, dst_ref, r, sem):
    return pltpu.make_async_copy(h_hbm.at[pl.ds(tok * TOK_ROWS, TOK_ROWS), :],
                                 dst_ref.at[pl.ds(r * TOK_PITCH, TOK_ROWS), :], sem)


def _dispatch_kernel(act_ref, tok_ref, nxt_ref, h_hbm, o_ref, buf_ref, sem):
    i = pl.program_id(0)
    n = pl.num_programs(0)
    rows = o_ref.shape[0]

    def gather(idx_ref, slot):
        def issue(r, c):
            _token_copy(h_hbm, idx_ref[0, 0, r], buf_ref.at[slot], r, sem.at[slot]).start()
            return c
        lax.fori_loop(0, rows, issue, 0, unroll=8)

    @pl.when((i == 0) & (act_ref[0] > 0))
    def _():
        gather(tok_ref, 0)

    @pl.when((i + 1 < n) & (act_ref[jnp.minimum(i + 1, n - 1)] > 0))
    def _():
        gather(nxt_ref, (i + 1) % 2)

    @pl.when(act_ref[i] > 0)
    def _():
        slot = i % 2
        pltpu.make_async_copy(h_hbm.at[pl.ds(0, rows * TOK_ROWS), :],
                              buf_ref.at[slot, pl.ds(0, rows * TOK_ROWS), :], sem.at[slot]).wait()
        for c in range(TOK_ROWS):
            o_ref[:, c * LANES:(c + 1) * LANES] = buf_ref[slot, pl.ds(c, rows, stride=TOK_PITCH), :].astype(BF16)

    @pl.when(act_ref[i] == 0)
    def _():
        o_ref[...] = jnp.zeros(o_ref.shape, o_ref.dtype)


def _dispatch(blk_act, slot_tok3, h):
    nblk, _, rows = slot_tok3.shape
    grid_spec = pltpu.PrefetchScalarGridSpec(
        num_scalar_prefetch=1,
        grid=(nblk,),
        in_specs=[pl.BlockSpec((1, 1, rows), lambda i, act: (i, 0, 0), memory_space=pltpu.SMEM),
                  pl.BlockSpec((1, 1, rows), lambda i, act: (jnp.minimum(i + 1, nblk - 1), 0, 0),
                               memory_space=pltpu.SMEM),
                  pl.BlockSpec(memory_space=pl.ANY)],
        out_specs=pl.BlockSpec((rows, D_MODEL), lambda i, act: (i, 0)),
        scratch_shapes=[pltpu.VMEM((2, rows * TOK_PITCH, LANES), F32), pltpu.SemaphoreType.DMA((2,))],
    )
    return pl.pallas_call(
        _dispatch_kernel,
        grid_spec=grid_spec,
        out_shape=jax.ShapeDtypeStruct((nblk * rows, D_MODEL), BF16),
        compiler_params=_cparams(("arbitrary",)),
        name="dispatch",
    )(blk_act, slot_tok3, slot_tok3, h)


def _cache_weights(se_ref, m, w_ref, wb_ref):
    prev = se_ref[jnp.maximum(m - 1, 0)]

    @pl.when((m == 0) | (se_ref[m] != prev))
    def _():
        wb_ref[...] = w_ref[0].astype(BF16)


def _for_row_count(nsub, o_ref, compute):
    @pl.when(nsub == 0)
    def _():
        o_ref[...] = jnp.zeros(o_ref.shape, o_ref.dtype)

    for s in range(1, NSUB + 1):
        @pl.when(nsub == s)
        def _(s=s):
            rows = s * SUB
            compute(rows)
            if rows < o_ref.shape[0]:
                o_ref[rows:, :] = jnp.zeros((o_ref.shape[0] - rows, o_ref.shape[1]), o_ref.dtype)


def _expert_a_kernel(se_ref, ns_ref, nu_ref, x_ref, w_ref, b_ref, o_ref, wb_ref):
    m = pl.program_id(1)
    tn = w_ref.shape[2]
    nsub = ns_ref[m]

    @pl.when(nsub > 0)
    def _():
        _cache_weights(se_ref, m, w_ref, wb_ref)

    def compute(rows):
        r_i = lax.broadcasted_iota(jnp.int32, (MXU_DIM, LANES), 0)
        c_i = lax.broadcasted_iota(jnp.int32, (MXU_DIM, LANES), 1)
        sel = jnp.where(r_i == 2 * c_i, 1.0, 0.0).astype(BF16)
        gu = jnp.dot(x_ref[0:rows, :], wb_ref[...], preferred_element_type=F32) + b_ref[0]
        for c in range(tn // MXU_DIM):
            parts = []
            for s in range(MXU_DIM // LANES):
                g = gu[:, c * MXU_DIM + s * LANES:c * MXU_DIM + (s + 1) * LANES]
                gate = jnp.minimum(g, SWIGLU_LIMIT)
                up1 = jnp.clip(g, -SWIGLU_LIMIT, SWIGLU_LIMIT) + 1.0
                act = gate * jax.nn.sigmoid(SWIGLU_ALPHA * gate)
                parts.append((act * pltpu.roll(up1, LANES - 1, 1)).astype(BF16))
            pair = jnp.concatenate(parts, axis=1)
            o_ref[0:rows, c * LANES:(c + 1) * LANES] = jnp.dot(
                pair, sel, preferred_element_type=F32).astype(BF16)

    _for_row_count(nsub, o_ref, compute)


def _expert_b_kernel(se_ref, ns_ref, nu_ref, a_ref, w_ref, b_ref, o_ref, wb_ref):
    m = pl.program_id(1)
    nsub = ns_ref[m]

    @pl.when(nsub > 0)
    def _():
        _cache_weights(se_ref, m, w_ref, wb_ref)

    def compute(rows):
        o_ref[0:rows, :] = jnp.dot(a_ref[0:rows, :], wb_ref[...], preferred_element_type=F32) + b_ref[0]

    _for_row_count(nsub, o_ref, compute)


def _expert_call(kernel, name, sb_e, sb_nsub, n_used, xin, w, b3, out_cols, out_dtype, out_tn):
    n_sb = sb_e.shape[0]
    tm, tn = MOE_TM, MOE_TN
    kdim = w.shape[1]
    nj = w.shape[2] // tn

    def blk(m, nu):
        return jnp.maximum(jnp.minimum(m, nu[0] - 1), 0)

    grid_spec = pltpu.PrefetchScalarGridSpec(
        num_scalar_prefetch=3,
        grid=(nj, n_sb),
        in_specs=[
            pl.BlockSpec((tm, kdim), lambda j, m, se, ns, nu: (blk(m, nu), 0)),
            pl.BlockSpec((1, kdim, tn), lambda j, m, se, ns, nu: (se[blk(m, nu)], 0, j)),
            pl.BlockSpec((1, 1, tn), lambda j, m, se, ns, nu: (se[blk(m, nu)], 0, j)),
        ],
        out_specs=pl.BlockSpec((tm, out_tn), lambda j, m, se, ns, nu: (m, j)),
        scratch_shapes=[pltpu.VMEM((kdim, tn), BF16)],
    )
    return pl.pallas_call(
        kernel,
        grid_spec=grid_spec,
        out_shape=jax.ShapeDtypeStruct((n_sb * tm, out_cols), out_dtype),
        compiler_params=_cparams(("arbitrary", "arbitrary")),
        name=name,
    )(sb_e, sb_nsub, n_used, xin, w, b3)


def _combine_kernel(dest_ref, y_hbm, x1_ref, route_ref, o_ref, buf_ref, sem):
    tm = o_ref.shape[0]

    def issue(r, c):
        for kk in range(TOP_K):
            _row_copy(y_hbm, dest_ref[0, 0, r * TOP_K + kk], buf_ref.at[kk], r, sem).start()
        return c

    lax.fori_loop(0, tm, issue, 0, unroll=4)
    for kk in range(TOP_K):
        pltpu.make_async_copy(y_hbm.at[pl.ds(0, tm), :], buf_ref.at[kk], sem).wait()
    route = route_ref[...]
    out = x1_ref[...]
    for kk in range(TOP_K):
        out = out + route[:, TOP_K + kk:TOP_K + kk + 1] * buf_ref[kk]
    o_ref[...] = out


def _combine(dest3, y, x1, route):
    T = x1.shape[0]
    tm = CMB_TM
    row = lambda i: (i, 0)
    return pl.pallas_call(
        _combine_kernel,
        grid=(T // tm,),
        in_specs=[pl.BlockSpec((1, 1, tm * TOP_K), lambda i: (i, 0, 0), memory_space=pltpu.SMEM),
                  pl.BlockSpec(memory_space=pl.ANY),
                  pl.BlockSpec((tm, D_MODEL), row),
                  pl.BlockSpec((tm, LANES), row)],
        out_specs=pl.BlockSpec((tm, D_MODEL), row),
        out_shape=jax.ShapeDtypeStruct((T, D_MODEL), F32),
        scratch_shapes=[pltpu.VMEM((TOP_K, tm, D_MODEL), F32), pltpu.SemaphoreType.DMA(())],
        compiler_params=_cparams(("arbitrary",)),
        name="combine",
    )(dest3, y, x1, route)


def _pad_lanes(v, width=LANES):
    return jnp.pad(v, [(0, 0)] * (v.ndim - 1) + [(0, width - v.shape[-1])])


def _swap_halves(v):
    half = v.shape[-1] // 2
    return jnp.concatenate([v[..., half:], v[..., :half]], axis=-1)


def _token_mixers(x2, B, S, attn_norm_w, w_in, q_a_norm_w, w_uq, kv_a_norm_w, w_ukv, mla_q_norm_w, mla_k_norm_w,
                  diff_q_norm_w, diff_k_norm_w, lambda_q1, lambda_k1, lambda_q2, lambda_k2, diff_subln_w, rel_bias):
    L = 0
    win = w_in[L]
    sp = np.cumsum([MLA_Q_RANK, MLA_KV_RANK, MLA_ROPE])
    w_kr = win[:, sp[1]:sp[2]]
    win_r = jnp.concatenate(
        [win[:, :sp[1]], _pad_lanes(w_kr), _pad_lanes(_swap_halves(w_kr)), win[:, sp[2]:]], axis=1).astype(BF16)
    qk = MLA_NOPE + MLA_ROPE
    wuq3 = w_uq[L].reshape(MLA_Q_RANK, MLA_HEADS, qk)
    wuq_rope = wuq3[:, :, MLA_NOPE:]
    wuq_r = jnp.concatenate([
        wuq3[:, :, :MLA_NOPE].reshape(MLA_Q_RANK, -1),
        _pad_lanes(wuq_rope).reshape(MLA_Q_RANK, -1),
        _pad_lanes(_swap_halves(wuq_rope)).reshape(MLA_Q_RANK, -1)], axis=1).astype(BF16)
    wukv_b = w_ukv[L].astype(BF16)
    qn_w, kn_w = mla_q_norm_w[L], mla_k_norm_w[L]
    vecs = jnp.stack([
        qn_w[:MLA_NOPE], _pad_lanes(qn_w[MLA_NOPE:]), _pad_lanes(_swap_halves(qn_w[MLA_NOPE:])),
        kn_w[:MLA_NOPE], _pad_lanes(kn_w[MLA_NOPE:]), _pad_lanes(_swap_halves(kn_w[MLA_NOPE:])),
        jnp.tile(diff_q_norm_w[L], 2), jnp.tile(diff_k_norm_w[L], 2)])
    lamv = _pad_lanes(jnp.stack([lambda_q1[L], lambda_k1[L], lambda_q2[L], lambda_k2[L]]))

    inv_freq = np.float32(ROPE_THETA) ** (-np.arange(0, MLA_ROPE, 2, dtype=np.float32) / np.float32(MLA_ROPE))
    ang = np.arange(S, dtype=np.float32)[:, None] * inv_freq[None, :].astype(np.float32)
    cos, sin = np.cos(ang).astype(np.float32), np.sin(ang).astype(np.float32)
    zpad = np.zeros((S, LANES - MLA_ROPE), np.float32)
    cos128 = jnp.asarray(np.concatenate([cos, cos, zpad], axis=1))
    sin128 = jnp.asarray(np.concatenate([-sin, sin, zpad], axis=1))

    qm, km, vm, qd, kd, vd = _prep(x2, attn_norm_w[L][None], win_r, q_a_norm_w[L][None], wuq_r,
                                   kv_a_norm_w[L][None], wukv_b, vecs, cos128, sin128, S)
    nbias = _relbias(rel_bias, ATT_TQ, ATT_TK)
    om = _attention(qm, km, vm, B, S, MLA_HEADS, 2 * LANES, diff=False)
    od = _attention(qd, kd, vd, B, S, DIFF_HEADS, LANES, diff=True,
                    extra=(nbias, lamv, diff_subln_w[L][None]))
    return om, od


def _moe_block(om, od, x2, w_o, ffn_norm_w, router_w, router_b, w_gate_up, b_gate_up, w_down, b_down):
    L = 0
    T, D = x2.shape
    wo_b = w_o[L].astype(BF16)
    rw_pad = _pad_lanes(router_w[L])
    rb_pad = _pad_lanes(router_b[L][None, :])

    x1, h, route, cnt = _router(om, od, x2, wo_b, ffn_norm_w[L][None], rw_pad, rb_pad)

    tm = MOE_TM
    n_sb = -(-T * TOP_K // tm) + N_EXPERTS
    idx = route[:, 0:TOP_K].astype(jnp.int32)
    rank = route[:, 2 * TOP_K:3 * TOP_K].astype(jnp.int32)
    counts = cnt[0, :N_EXPERTS].astype(jnp.int32)
    padded = (counts + tm - 1) // tm * tm
    pad_ends = jnp.cumsum(padded)
    pad_starts = pad_ends - padded
    dest = pad_starts[idx] + rank
    tok = jnp.broadcast_to(jnp.arange(T, dtype=jnp.int32)[:, None], (T, TOP_K))
    slot_tok = jnp.zeros((n_sb * tm,), jnp.int32).at[dest.reshape(-1)].set(tok.reshape(-1))
    sb_start = jnp.arange(n_sb, dtype=jnp.int32) * tm
    sb_e = jnp.minimum(jnp.sum((pad_ends[None, :] <= sb_start[:, None]).astype(jnp.int32), axis=1), N_EXPERTS - 1)
    sb_valid = jnp.clip(pad_starts[sb_e] + counts[sb_e] - sb_start, 0, tm)
    sb_valid = jnp.where(sb_start < pad_ends[-1], sb_valid, 0)
    sb_nsub = (sb_valid + SUB - 1) // SUB
    n_used = (pad_ends[-1] // tm).astype(jnp.int32)[None]
    blk_act = (jnp.arange(n_sb * NSUB, dtype=jnp.int32) % NSUB < jnp.repeat(sb_nsub, NSUB)).astype(jnp.int32)

    xs = _dispatch(blk_act, slot_tok.reshape(n_sb * NSUB, 1, SUB), h)
    act = _expert_call(_expert_a_kernel, "expert_a", sb_e, sb_nsub, n_used, xs, w_gate_up[L],
                       b_gate_up[L][:, None, :], D_FF, BF16, MOE_TN // 2)
    y = _expert_call(_expert_b_kernel, "expert_b", sb_e, sb_nsub, n_used, act, w_down[L],
                     b_down[L][:, None, :], D, F32, MOE_TN)
    return _combine(dest.reshape(T // CMB_TM, 1, CMB_TM * TOP_K), y, x1, route)


def kernel(x, attn_norm_w, w_in, q_a_norm_w, w_uq, kv_a_norm_w, w_ukv, mla_q_norm_w, mla_k_norm_w, diff_q_norm_w, diff_k_norm_w, lambda_q1, lambda_k1, lambda_q2, lambda_k2, diff_subln_w, w_o, ffn_norm_w, router_w, router_b, w_gate_up, b_gate_up, w_down, b_down, rel_bias):
    B, S, D = x.shape
    x2 = x.reshape(B * S, D)
    om, od = _token_mixers(x2, B, S, attn_norm_w, w_in, q_a_norm_w, w_uq, kv_a_norm_w, w_ukv, mla_q_norm_w,
                           mla_k_norm_w, diff_q_norm_w, diff_k_norm_w, lambda_q1, lambda_k1, lambda_q2, lambda_k2,
                           diff_subln_w, rel_bias)
    out = _moe_block(om, od, x2, w_o, ffn_norm_w, router_w, router_b, w_gate_up, b_gate_up, w_down, b_down)
    return out.reshape(B, S, D)
```

```python
import functools
import math

import jax
import jax.numpy as jnp
import numpy as np
from jax import lax
from jax.experimental import pallas as pl
from jax.experimental.pallas import tpu as pltpu

D_MODEL = 2048
CHUNK = 64
MLA_HEADS = 8
MLA_NOPE = 128
MLA_ROPE = 64
MLA_V = 128
MLA_Q_RANK = 384
MLA_KV_RANK = 256
ROPE_THETA = 10000.0
DIFF_HEADS = 8
DIFF_DH = 64
DIFF_V = 128
REL_BUCKETS = 32
REL_MAX_DIST = 128
N_EXPERTS = 32
TOP_K = 4
D_FF = 2048
SWIGLU_LIMIT = 7.0
SWIGLU_ALPHA = 1.702
EPS = 1e-6
LAMBDA_INIT = 0.8 - 0.6 * math.exp(-0.3 * 0)

LANES = 128
MXU_DIM = 256
VMEM_LIMIT = 48 * 1024 * 1024

PREP_TM = 256
ATT_TQ = 512
ATT_TK = 512
ATT_HP = 4
RT_TM = 512
SUB = 256
MOE_TM = 1024
NSUB = MOE_TM // SUB
MOE_TN = 1024
CMB_TM = 128
HALF_D = D_MODEL // 2
TOK_ROWS = HALF_D // LANES

LOG2E = math.log2(math.e)
NEG = -1e30
F32 = jnp.float32
BF16 = jnp.bfloat16

_C_CQ = 0
_C_CKV = MLA_Q_RANK
_C_KR = _C_CKV + MLA_KV_RANK
_C_KRR = _C_KR + LANES
_C_MLA_END = _C_KRR + LANES
_C_DQ = _C_MLA_END
_C_DK = _C_DQ + DIFF_HEADS * 2 * DIFF_DH
_C_DV = _C_DK + DIFF_HEADS * 2 * DIFF_DH
_C_END = _C_DV + DIFF_HEADS * DIFF_V


def _cparams(sem):
    return pltpu.CompilerParams(dimension_semantics=sem, vmem_limit_bytes=VMEM_LIMIT)


def _const_spec(shape):
    nd = len(shape)
    return pl.BlockSpec(shape, lambda *_: (0,) * nd, pipeline_mode=pl.Buffered(1))


def _pack_bf16_pair(lo, hi):
    lo_b = lax.bitcast_convert_type(lo.astype(BF16).astype(F32), jnp.uint32)
    hi_b = lax.bitcast_convert_type(hi.astype(BF16).astype(F32), jnp.uint32)
    return (lo_b >> 16) | (hi_b & jnp.uint32(0xFFFF0000))


def _unpack_bf16_pair(w):
    return (lax.bitcast_convert_type(w << 16, F32),
            lax.bitcast_convert_type(w & jnp.uint32(0xFFFF0000), F32))


def _prep_kernel(x_ref, anw_ref, win_ref, qaw_ref, wuq_ref, kvaw_ref, wukv_ref, vec_ref, cos_ref, sin_ref,
                 qm_ref, km_ref, vm_ref, qd_ref, kd_ref, vd_ref):
    xf = x_ref[...]
    inv = lax.rsqrt(jnp.mean(xf * xf, axis=-1, keepdims=True) + EPS)
    h = (xf * inv * anw_ref[...]).astype(BF16)

    def rms(v, w):
        return v * lax.rsqrt(jnp.mean(v * v, axis=-1, keepdims=True) + EPS) * w

    pm = jnp.dot(h, win_ref[:, _C_CQ:_C_MLA_END], preferred_element_type=F32)
    cqn = rms(pm[:, _C_CQ:_C_CKV], qaw_ref[...]).astype(BF16)
    ckn = rms(pm[:, _C_CKV:_C_KR], kvaw_ref[...]).astype(BF16)
    kr = pm[:, _C_KR:_C_KRR]
    krr = pm[:, _C_KRR:_C_MLA_END]
    qall = jnp.dot(cqn, wuq_ref[...], preferred_element_type=F32)
    kvall = jnp.dot(ckn, wukv_ref[...], preferred_element_type=F32)

    vec = vec_ref[...]
    wqn, wqr, wqrr = vec[0:1], vec[1:2], vec[2:3]
    wkn, wkr, wkrr = vec[3:4], vec[4:5], vec[5:6]
    wdq, wdk = vec[6:7], vec[7:8]
    cos = cos_ref[...]
    sin = sin_ref[...]
    cq_r, sq_r = wqr * cos, wqrr * sin
    ck_r, sk_r = wkr * cos, wkrr * sin
    kr2 = kr * kr
    k_rope_base = kr * ck_r + krr * sk_r
    q_scale = (MLA_NOPE + MLA_ROPE) ** -0.5 * LOG2E
    inv_qk = 1.0 / (MLA_NOPE + MLA_ROPE)
    nq = MLA_HEADS * LANES
    for hh in range(MLA_HEADS):
        a = hh * LANES
        qn = qall[:, a:a + LANES]
        qr = qall[:, nq + a:nq + a + LANES]
        qrr = qall[:, 2 * nq + a:2 * nq + a + LANES]
        iq = lax.rsqrt(jnp.sum(qn * qn + qr * qr, axis=-1, keepdims=True) * inv_qk + EPS) * q_scale
        qm_ref[:, 2 * a:2 * a + LANES] = (qn * iq * wqn).astype(BF16)
        qm_ref[:, 2 * a + LANES:2 * a + 2 * LANES] = ((qr * cq_r + qrr * sq_r) * iq).astype(BF16)
        kn = kvall[:, 2 * a:2 * a + LANES]
        ik = lax.rsqrt(jnp.sum(kn * kn + kr2, axis=-1, keepdims=True) * inv_qk + EPS)
        km_ref[:, 2 * a:2 * a + LANES] = (kn * ik * wkn).astype(BF16)
        km_ref[:, 2 * a + LANES:2 * a + 2 * LANES] = (k_rope_base * ik).astype(BF16)
        vm_ref[:, a:a + LANES] = kvall[:, 2 * a + LANES:2 * a + 2 * LANES].astype(BF16)

    dq = jnp.dot(h, win_ref[:, _C_DQ:_C_DK], preferred_element_type=F32)
    dk = jnp.dot(h, win_ref[:, _C_DK:_C_DV], preferred_element_type=F32)
    vd_ref[...] = jnp.dot(h, win_ref[:, _C_DV:_C_END], preferred_element_type=F32).astype(BF16)
    lane = lax.broadcasted_iota(jnp.int32, (xf.shape[0], LANES), 1)
    lo = lane < DIFF_DH
    d_scale = DIFF_DH ** -0.5 * LOG2E
    inv_dh = 1.0 / DIFF_DH

    def half_norm(v):
        sq = v * v
        s1 = jnp.sum(jnp.where(lo, sq, 0.0), axis=-1, keepdims=True)
        s2 = jnp.sum(jnp.where(lo, 0.0, sq), axis=-1, keepdims=True)
        return v * jnp.where(lo, lax.rsqrt(s1 * inv_dh + EPS), lax.rsqrt(s2 * inv_dh + EPS))

    for hh in range(DIFF_HEADS):
        a = hh * LANES
        qn = half_norm(dq[:, a:a + LANES]) * (wdq * d_scale)
        qd_ref[:, 2 * a:2 * a + LANES] = jnp.where(lo, qn, 0.0).astype(BF16)
        qd_ref[:, 2 * a + LANES:2 * a + 2 * LANES] = jnp.where(lo, 0.0, qn).astype(BF16)
        kd_ref[:, a:a + LANES] = (half_norm(dk[:, a:a + LANES]) * wdk).astype(BF16)


def _prep(x2, anw, win_r, qaw, wuq_r, kvaw, wukv_b, vecs, cos128, sin128, seq):
    T = x2.shape[0]
    tm = PREP_TM
    nseq = seq // tm
    row = lambda i: (i, 0)
    outs = [
        jax.ShapeDtypeStruct((T, MLA_HEADS * 2 * LANES), BF16),
        jax.ShapeDtypeStruct((T, MLA_HEADS * 2 * LANES), BF16),
        jax.ShapeDtypeStruct((T, MLA_HEADS * MLA_V), BF16),
        jax.ShapeDtypeStruct((T, DIFF_HEADS * 2 * LANES), BF16),
        jax.ShapeDtypeStruct((T, DIFF_HEADS * LANES), BF16),
        jax.ShapeDtypeStruct((T, DIFF_HEADS * DIFF_V), BF16),
    ]
    return pl.pallas_call(
        _prep_kernel,
        grid=(T // tm,),
        in_specs=[
            pl.BlockSpec((tm, D_MODEL), row),
            _const_spec(anw.shape), _const_spec(win_r.shape), _const_spec(qaw.shape), _const_spec(wuq_r.shape),
            _const_spec(kvaw.shape), _const_spec(wukv_b.shape), _const_spec(vecs.shape),
            pl.BlockSpec((tm, LANES), lambda i: (i % nseq, 0)),
            pl.BlockSpec((tm, LANES), lambda i: (i % nseq, 0)),
        ],
        out_specs=[pl.BlockSpec((tm, o.shape[1]), row) for o in outs],
        out_shape=outs,
        compiler_params=_cparams(("parallel",)),
        name="prep",
    )(x2, anw, win_r, qaw, wuq_r, kvaw, wukv_b, vecs, cos128, sin128)


def _relbias_kernel(rb_ref, o_ref, *, tq, tk):
    hh = pl.program_id(0)
    which = pl.program_id(1)
    row = lax.broadcasted_iota(jnp.int32, (tq, tk), 0)
    col = lax.broadcasted_iota(jnp.int32, (tq, tk), 1)
    rel = col - which * tk - row
    nb = REL_BUCKETS // 2
    max_exact = nb // 2
    ret = jnp.where(rel > 0, nb, 0)
    n = jnp.abs(rel)
    nf = jnp.maximum(n, 1).astype(F32)
    large = max_exact + (jnp.log(nf / max_exact) / math.log(REL_MAX_DIST / max_exact)
                         * (nb - max_exact)).astype(jnp.int32)
    large = jnp.minimum(large, nb - 1)
    bucket = ret + jnp.where(n < max_exact, n, large)
    bias = jnp.zeros((tq, tk), F32)
    for b in range(REL_BUCKETS):
        bias = jnp.where(bucket == b, rb_ref[b, hh], bias)
    bias = (bias - rb_ref[nb - 1, hh]) * LOG2E
    allowed = ((col // CHUNK) <= (row // CHUNK)) | (which > 0)
    o_ref[0, 0] = jnp.where(allowed, bias, NEG)


def _relbias(rel_bias, tq, tk):
    return pl.pallas_call(
        functools.partial(_relbias_kernel, tq=tq, tk=tk),
        grid=(DIFF_HEADS, 2),
        in_specs=[pl.BlockSpec(memory_space=pltpu.SMEM)],
        out_specs=pl.BlockSpec((1, 1, tq, tk), lambda h, w: (h, w, 0, 0)),
        out_shape=jax.ShapeDtypeStruct((DIFF_HEADS, 2, tq, tk), F32),
        compiler_params=_cparams(("parallel", "parallel")),
        name="relbias",
    )(rel_bias)


def _attn_kernel(*refs, diff, tq, tk):
    if diff:
        q_ref, k_ref, v_ref, nb_ref, lam_ref, sub_ref, o_ref, m_ref, l_ref, acc_ref = refs
    else:
        q_ref, k_ref, v_ref, o_ref, m_ref, l_ref, acc_ref = refs
    i = pl.program_id(2)
    hp = o_ref.shape[1] // LANES
    dk = k_ref.shape[1] // hp
    qs = []
    for a in range(hp):
        blk = q_ref[:, a * 2 * LANES:(a + 1) * 2 * LANES]
        qs.append(jnp.concatenate([blk[:, :LANES], blk[:, LANES:]], axis=0) if diff else blk)
    m_ref[...] = jnp.full(m_ref.shape, NEG, F32)
    l_ref[...] = jnp.zeros(l_ref.shape, F32)
    acc_ref[...] = jnp.zeros(acc_ref.shape, F32)

    def step(j, biases):
        start = pl.multiple_of(j * tk, tk)
        for a in range(hp):
            ks = k_ref[pl.ds(start, tk), a * dk:(a + 1) * dk]
            vs = v_ref[pl.ds(start, tk), a * LANES:(a + 1) * LANES]
            s = lax.dot_general(qs[a], ks, (((1,), (1,)), ((), ())), preferred_element_type=F32)
            if biases is not None:
                s = s + biases[a]
            m_prev = m_ref[a]
            m_new = jnp.maximum(m_prev, jnp.max(s, axis=1, keepdims=True))
            alpha = jnp.exp2(m_prev - m_new)
            p = jnp.exp2(s - jnp.concatenate([m_new] * (tk // LANES), axis=1))
            psum = p[:, 0:LANES]
            for c in range(1, tk // LANES):
                psum = psum + p[:, c * LANES:(c + 1) * LANES]
            l_ref[a] = alpha * l_ref[a] + psum
            acc_ref[a] = alpha * acc_ref[a] + jnp.dot(p.astype(BF16), vs, preferred_element_type=F32)
            m_ref[a] = m_new

    def far_body(j, c):
        step(j, None)
        return c

    if diff:
        lax.fori_loop(0, jnp.maximum(i - 1, 0), far_body, 0)

        @pl.when(i > 0)
        def _():
            step(i - 1, [jnp.concatenate([nb_ref[a, 1]] * 2, axis=0) for a in range(hp)])

        step(i, [jnp.concatenate([nb_ref[a, 0]] * 2, axis=0) for a in range(hp)])
    else:
        lax.fori_loop(0, i, far_body, 0)
        row = lax.broadcasted_iota(jnp.int32, (tq, tk), 0)
        col = lax.broadcasted_iota(jnp.int32, (tq, tk), 1)
        step(i, [jnp.where((col // CHUNK) <= (row // CHUNK), 0.0, NEG)] * hp)

    for a in range(hp):
        o = acc_ref[a] / jnp.sum(l_ref[a], axis=1, keepdims=True)
        if diff:
            lv = lam_ref[...]
            lam = (jnp.exp(jnp.sum(lv[0:1] * lv[1:2], axis=1, keepdims=True))
                   - jnp.exp(jnp.sum(lv[2:3] * lv[3:4], axis=1, keepdims=True)) + LAMBDA_INIT)
            od = o[:tq] - lam * o[tq:]
            od = od * lax.rsqrt(jnp.mean(od * od, axis=-1, keepdims=True) + EPS) * sub_ref[...]
            o_ref[:, a * LANES:(a + 1) * LANES] = (od * (1.0 - LAMBDA_INIT)).astype(BF16)
        else:
            o_ref[:, a * LANES:(a + 1) * LANES] = o.astype(BF16)


def _attention(q, k, v, batch, seq, heads, dk, diff, extra=()):
    tq, tk, hp = ATT_TQ, ATT_TK, ATT_HP
    nq = seq // tq
    rows = 2 * tq if diff else tq
    in_specs = [
        pl.BlockSpec((tq, hp * 2 * LANES), lambda b, h, i: (b * nq + i, h)),
        pl.BlockSpec((seq, hp * dk), lambda b, h, i: (b, h), pipeline_mode=pl.Buffered(1)),
        pl.BlockSpec((seq, hp * LANES), lambda b, h, i: (b, h), pipeline_mode=pl.Buffered(1)),
    ]
    if diff:
        nbias, lamv, subw = extra
        in_specs += [
            pl.BlockSpec((hp, 2, tq, tk), lambda b, h, i: (h, 0, 0, 0), pipeline_mode=pl.Buffered(1)),
            pl.BlockSpec(lamv.shape, lambda b, h, i: (0, 0)),
            pl.BlockSpec(subw.shape, lambda b, h, i: (0, 0)),
        ]
    return pl.pallas_call(
        functools.partial(_attn_kernel, diff=diff, tq=tq, tk=tk),
        grid=(batch, heads // hp, nq),
        in_specs=in_specs,
        out_specs=pl.BlockSpec((tq, hp * LANES), lambda b, h, i: (b * nq + i, h)),
        out_shape=jax.ShapeDtypeStruct((batch * seq, heads * LANES), BF16),
        scratch_shapes=[pltpu.VMEM((hp, rows, LANES), F32)] * 3,
        compiler_params=_cparams(("parallel", "parallel", "arbitrary")),
        name="attn_diff" if diff else "attn_mla",
    )(q, k, v, *extra)


def _router_kernel(om_ref, od_ref, x_ref, wo_ref, fw_ref, rw_ref, rb_ref,
                   x1_ref, h_ref, route_ref, cnt_ref, carry_ref):
    tm = x_ref.shape[0]
    half = om_ref.shape[1]

    @pl.when(pl.program_id(0) == 0)
    def _():
        carry_ref[...] = jnp.zeros(carry_ref.shape, F32)

    y = (jnp.dot(om_ref[...], wo_ref[0:half, :], preferred_element_type=F32)
         + jnp.dot(od_ref[...], wo_ref[half:2 * half, :], preferred_element_type=F32))
    x1 = x_ref[...] + y
    x1_ref[...] = x1
    hh = x1 * lax.rsqrt(jnp.mean(x1 * x1, axis=-1, keepdims=True) + EPS) * fw_ref[...]
    for c in range(TOK_ROWS):
        h_ref[pl.ds(c, tm, stride=TOK_ROWS), :] = _pack_bf16_pair(
            hh[:, c * LANES:(c + 1) * LANES], hh[:, HALF_D + c * LANES:HALF_D + (c + 1) * LANES])
    rw = rw_ref[...]
    rw_hi = rw.astype(BF16)
    rw_lo = (rw - rw_hi.astype(F32)).astype(BF16)
    hh_hi = hh.astype(BF16)
    hh_lo = (hh - hh_hi.astype(F32)).astype(BF16)
    logits = (jnp.dot(hh_hi, rw_hi, preferred_element_type=F32) + jnp.dot(hh_lo, rw_hi, preferred_element_type=F32)
              + jnp.dot(hh_hi, rw_lo, preferred_element_type=F32)) + rb_ref[...]
    lane = lax.broadcasted_iota(jnp.int32, (tm, LANES), 1).astype(F32)
    work = jnp.where(lane < N_EXPERTS, logits, -jnp.inf)
    vals, idxs = [], []
    for _ in range(TOP_K):
        mx = jnp.max(work, axis=-1, keepdims=True)
        ix = jnp.min(jnp.where(work == mx, lane, float(LANES)), axis=-1, keepdims=True)
        vals.append(mx)
        idxs.append(ix)
        work = jnp.where(lane == ix, -jnp.inf, work)
    es = [jnp.exp(v - vals[0]) for v in vals]
    den = es[0] + es[1] + es[2] + es[3]
    onehot = jnp.zeros((tm, LANES), F32)
    for ix in idxs:
        onehot = onehot + jnp.where(lane == ix, 1.0, 0.0)
    r_i = lax.broadcasted_iota(jnp.int32, (tm, tm), 0)
    c_i = lax.broadcasted_iota(jnp.int32, (tm, tm), 1)
    tri = jnp.where(c_i < r_i, 1.0, 0.0).astype(BF16)
    prefix = jnp.dot(tri, onehot.astype(BF16), preferred_element_type=F32) + carry_ref[...]
    route = jnp.zeros((tm, LANES), F32)
    for kk in range(TOP_K):
        rank = jnp.sum(jnp.where(lane == idxs[kk], prefix, 0.0), axis=-1, keepdims=True)
        route = jnp.where(lane == kk, idxs[kk], route)
        route = jnp.where(lane == TOP_K + kk, es[kk] / den, route)
        route = jnp.where(lane == 2 * TOP_K + kk, rank, route)
    route_ref[...] = route
    carry = carry_ref[...] + jnp.sum(onehot, axis=0, keepdims=True)
    carry_ref[...] = carry
    cnt_ref[...] = carry


def _router(om, od, x2, wo_b, fw, rw_pad, rb_pad):
    T = x2.shape[0]
    tm = RT_TM
    row = lambda i: (i, 0)
    outs = [
        jax.ShapeDtypeStruct((T, D_MODEL), F32),
        jax.ShapeDtypeStruct((T * TOK_ROWS, LANES), jnp.uint32),
        jax.ShapeDtypeStruct((T, LANES), F32),
        jax.ShapeDtypeStruct((1, LANES), F32),
    ]
    return pl.pallas_call(
        _router_kernel,
        grid=(T // tm,),
        in_specs=[
            pl.BlockSpec((tm, om.shape[1]), row), pl.BlockSpec((tm, od.shape[1]), row),
            pl.BlockSpec((tm, D_MODEL), row),
            _const_spec(wo_b.shape), _const_spec(fw.shape), _const_spec(rw_pad.shape), _const_spec(rb_pad.shape),
        ],
        out_specs=[pl.BlockSpec((tm, D_MODEL), row), pl.BlockSpec((tm * TOK_ROWS, LANES), row),
                   pl.BlockSpec((tm, LANES), row), pl.BlockSpec((1, LANES), lambda i: (0, 0))],
        out_shape=outs,
        scratch_shapes=[pltpu.VMEM((1, LANES), F32)],
        compiler_params=_cparams(("arbitrary",)),
        name="router",
    )(om, od, x2, wo_b, fw, rw_pad, rb_pad)


def _row_copy(src_hbm, idx, dst_ref, r, sem):
    return pltpu.make_async_copy(src_hbm.at[pl.ds(idx, 1), :], dst_ref.at[pl.ds(r, 1), :], sem)


def _token_copy(h_hbm, tok, dst_ref, r, sem):
    return pltpu.make_async_copy(h_hbm.at[pl.ds(tok * TOK_ROWS, TOK_ROWS), :],
                                 dst_ref.at[pl.ds(r * TOK_ROWS, TOK_ROWS), :], sem)


def _dispatch_kernel(act_ref, tok_ref, nxt_ref, h_hbm, o_ref, buf_ref, sem):
    i = pl.program_id(0)
    n = pl.num_programs(0)
    rows = o_ref.shape[0]

    def gather(idx_ref, slot):
        def issue(r2, c):
            for p in range(2):
                r = 2 * r2 + p
                _token_copy(h_hbm, idx_ref[0, 0, r], buf_ref.at[slot], r, sem.at[slot]).start(priority=p)
            return c
        lax.fori_loop(0, rows // 2, issue, 0, unroll=4)

    @pl.when((i == 0) & (act_ref[0] > 0))
    def _():
        gather(tok_ref, 0)

    @pl.when((i + 1 < n) & (act_ref[jnp.minimum(i + 1, n - 1)] > 0))
    def _():
        gather(nxt_ref, (i + 1) % 2)

    @pl.when(act_ref[i] > 0)
    def _():
        slot = i % 2
        pltpu.make_async_copy(h_hbm.at[pl.ds(0, rows * TOK_ROWS), :], buf_ref.at[slot], sem.at[slot]).wait()
        for c in range(TOK_ROWS):
            lo, hi = _unpack_bf16_pair(buf_ref[slot, pl.ds(c, rows, stride=TOK_ROWS), :])
            o_ref[:, c * LANES:(c + 1) * LANES] = lo.astype(BF16)
            o_ref[:, HALF_D + c * LANES:HALF_D + (c + 1) * LANES] = hi.astype(BF16)

    @pl.when(act_ref[i] == 0)
    def _():
        o_ref[...] = jnp.zeros(o_ref.shape, o_ref.dtype)


def _dispatch(blk_act, slot_tok3, h):
    nblk, _, rows = slot_tok3.shape
    grid_spec = pltpu.PrefetchScalarGridSpec(
        num_scalar_prefetch=1,
        grid=(nblk,),
        in_specs=[pl.BlockSpec((1, 1, rows), lambda i, act: (i, 0, 0), memory_space=pltpu.SMEM),
                  pl.BlockSpec((1, 1, rows), lambda i, act: (jnp.minimum(i + 1, nblk - 1), 0, 0),
                               memory_space=pltpu.SMEM),
                  pl.BlockSpec(memory_space=pl.ANY)],
        out_specs=pl.BlockSpec((rows, D_MODEL), lambda i, act: (i, 0)),
        scratch_shapes=[pltpu.VMEM((2, rows * TOK_ROWS, LANES), jnp.uint32), pltpu.SemaphoreType.DMA((2,))],
    )
    return pl.pallas_call(
        _dispatch_kernel,
        grid_spec=grid_spec,
        out_shape=jax.ShapeDtypeStruct((nblk * rows, D_MODEL), BF16),
        compiler_params=_cparams(("arbitrary",)),
        name="dispatch",
    )(blk_act, slot_tok3, slot_tok3, h)


def _cache_weights(se_ref, m, w_ref, wb_ref):
    prev = se_ref[jnp.maximum(m - 1, 0)]

    @pl.when((m == 0) | (se_ref[m] != prev))
    def _():
        wb_ref[...] = w_ref[0].astype(BF16)


def _for_row_count(nsub, o_ref, compute):
    @pl.when(nsub == 0)
    def _():
        o_ref[...] = jnp.zeros(o_ref.shape, o_ref.dtype)

    for s in range(1, NSUB + 1):
        @pl.when(nsub == s)
        def _(s=s):
            rows = s * SUB
            compute(rows)
            if rows < o_ref.shape[0]:
                o_ref[rows:, :] = jnp.zeros((o_ref.shape[0] - rows, o_ref.shape[1]), o_ref.dtype)


def _expert_a_kernel(se_ref, ns_ref, nu_ref, x_ref, w_ref, b_ref, o_ref, wb_ref):
    m = pl.program_id(1)
    tn = w_ref.shape[2]
    nsub = ns_ref[m]

    @pl.when(nsub > 0)
    def _():
        _cache_weights(se_ref, m, w_ref, wb_ref)

    def compute(rows):
        r_i = lax.broadcasted_iota(jnp.int32, (MXU_DIM, LANES), 0)
        c_i = lax.broadcasted_iota(jnp.int32, (MXU_DIM, LANES), 1)
        sel = jnp.where(r_i == 2 * c_i, 1.0, 0.0).astype(BF16)
        gu = jnp.dot(x_ref[0:rows, :], wb_ref[...], preferred_element_type=F32) + b_ref[0]
        for c in range(tn // MXU_DIM):
            parts = []
            for s in range(MXU_DIM // LANES):
                g = gu[:, c * MXU_DIM + s * LANES:c * MXU_DIM + (s + 1) * LANES]
                gate = jnp.minimum(g, SWIGLU_LIMIT)
                up1 = jnp.clip(g, -SWIGLU_LIMIT, SWIGLU_LIMIT) + 1.0
                act = gate * jax.nn.sigmoid(SWIGLU_ALPHA * gate)
                parts.append((act * pltpu.roll(up1, LANES - 1, 1)).astype(BF16))
            pair = jnp.concatenate(parts, axis=1)
            o_ref[0:rows, c * LANES:(c + 1) * LANES] = jnp.dot(
                pair, sel, preferred_element_type=F32).astype(BF16)

    _for_row_count(nsub, o_ref, compute)


def _expert_b_kernel(se_ref, ns_ref, nu_ref, a_ref, w_ref, b_ref, o_ref, wb_ref):
    m = pl.program_id(1)
    nsub = ns_ref[m]

    @pl.when(nsub > 0)
    def _():
        _cache_weights(se_ref, m, w_ref, wb_ref)

    def compute(rows):
        y = jnp.dot(a_ref[0:rows, :], wb_ref[...], preferred_element_type=F32) + b_ref[0]
        half = y.shape[1] // 2
        o_ref[0:rows, :] = _pack_bf16_pair(y[:, :half], y[:, half:])

    _for_row_count(nsub, o_ref, compute)


def _expert_call(kernel, name, sb_e, sb_nsub, n_used, xin, w, b3, out_cols, out_dtype, out_tn):
    n_sb = sb_e.shape[0]
    tm, tn = MOE_TM, MOE_TN
    kdim = w.shape[1]
    nj = w.shape[2] // tn

    def blk(m, nu):
        return jnp.maximum(jnp.minimum(m, nu[0] - 1), 0)

    grid_spec = pltpu.PrefetchScalarGridSpec(
        num_scalar_prefetch=3,
        grid=(nj, n_sb),
        in_specs=[
            pl.BlockSpec((tm, kdim), lambda j, m, se, ns, nu: (blk(m, nu), 0)),
            pl.BlockSpec((1, kdim, tn), lambda j, m, se, ns, nu: (se[blk(m, nu)], 0, j)),
            pl.BlockSpec((1, 1, tn), lambda j, m, se, ns, nu: (se[blk(m, nu)], 0, j)),
        ],
        out_specs=pl.BlockSpec((tm, out_tn), lambda j, m, se, ns, nu: (m, j)),
        scratch_shapes=[pltpu.VMEM((kdim, tn), BF16)],
    )
    return pl.pallas_call(
        kernel,
        grid_spec=grid_spec,
        out_shape=jax.ShapeDtypeStruct((n_sb * tm, out_cols), out_dtype),
        compiler_params=_cparams(("arbitrary", "arbitrary")),
        name=name,
    )(sb_e, sb_nsub, n_used, xin, w, b3)


def _combine_kernel(dest_ref, y_hbm, x1_ref, route_ref, o_ref, buf_ref, sem):
    tm = o_ref.shape[0]

    def issue(r, c):
        for kk in range(TOP_K):
            _row_copy(y_hbm, dest_ref[0, 0, r * TOP_K + kk], buf_ref.at[kk], r, sem).start(priority=kk % 2)
        return c

    lax.fori_loop(0, tm, issue, 0, unroll=4)
    for kk in range(TOP_K):
        pltpu.make_async_copy(y_hbm.at[pl.ds(0, tm), :], buf_ref.at[kk], sem).wait()
    route = route_ref[...]
    hw = MOE_TN // 2
    for j in range(D_MODEL // MOE_TN):
        lo_acc = x1_ref[:, j * MOE_TN:j * MOE_TN + hw]
        hi_acc = x1_ref[:, j * MOE_TN + hw:(j + 1) * MOE_TN]
        for kk in range(TOP_K):
            gate = route[:, TOP_K + kk:TOP_K + kk + 1]
            lo, hi = _unpack_bf16_pair(buf_ref[kk, :, j * hw:(j + 1) * hw])
            lo_acc = lo_acc + gate * lo
            hi_acc = hi_acc + gate * hi
        o_ref[:, j * MOE_TN:j * MOE_TN + hw] = lo_acc
        o_ref[:, j * MOE_TN + hw:(j + 1) * MOE_TN] = hi_acc


def _combine(dest3, y, x1, route):
    T = x1.shape[0]
    tm = CMB_TM
    row = lambda i: (i, 0)
    return pl.pallas_call(
        _combine_kernel,
        grid=(T // tm,),
        in_specs=[pl.BlockSpec((1, 1, tm * TOP_K), lambda i: (i, 0, 0), memory_space=pltpu.SMEM),
                  pl.BlockSpec(memory_space=pl.ANY),
                  pl.BlockSpec((tm, D_MODEL), row),
                  pl.BlockSpec((tm, LANES), row)],
        out_specs=pl.BlockSpec((tm, D_MODEL), row),
        out_shape=jax.ShapeDtypeStruct((T, D_MODEL), F32),
        scratch_shapes=[pltpu.VMEM((TOP_K, tm, HALF_D), jnp.uint32), pltpu.SemaphoreType.DMA(())],
        compiler_params=_cparams(("arbitrary",)),
        name="combine",
    )(dest3, y, x1, route)


def _pad_lanes(v, width=LANES):
    return jnp.pad(v, [(0, 0)] * (v.ndim - 1) + [(0, width - v.shape[-1])])


def _swap_halves(v):
    half = v.shape[-1] // 2
    return jnp.concatenate([v[..., half:], v[..., :half]], axis=-1)


def _token_mixers(x2, B, S, attn_norm_w, w_in, q_a_norm_w, w_uq, kv_a_norm_w, w_ukv, mla_q_norm_w, mla_k_norm_w,
                  diff_q_norm_w, diff_k_norm_w, lambda_q1, lambda_k1, lambda_q2, lambda_k2, diff_subln_w, rel_bias):
    L = 0
    win = w_in[L]
    sp = np.cumsum([MLA_Q_RANK, MLA_KV_RANK, MLA_ROPE])
    w_kr = win[:, sp[1]:sp[2]]
    win_r = jnp.concatenate(
        [win[:, :sp[1]], _pad_lanes(w_kr), _pad_lanes(_swap_halves(w_kr)), win[:, sp[2]:]], axis=1).astype(BF16)
    qk = MLA_NOPE + MLA_ROPE
    wuq3 = w_uq[L].reshape(MLA_Q_RANK, MLA_HEADS, qk)
    wuq_rope = wuq3[:, :, MLA_NOPE:]
    wuq_r = jnp.concatenate([
        wuq3[:, :, :MLA_NOPE].reshape(MLA_Q_RANK, -1),
        _pad_lanes(wuq_rope).reshape(MLA_Q_RANK, -1),
        _pad_lanes(_swap_halves(wuq_rope)).reshape(MLA_Q_RANK, -1)], axis=1).astype(BF16)
    wukv_b = w_ukv[L].astype(BF16)
    qn_w, kn_w = mla_q_norm_w[L], mla_k_norm_w[L]
    vecs = jnp.stack([
        qn_w[:MLA_NOPE], _pad_lanes(qn_w[MLA_NOPE:]), _pad_lanes(_swap_halves(qn_w[MLA_NOPE:])),
        kn_w[:MLA_NOPE], _pad_lanes(kn_w[MLA_NOPE:]), _pad_lanes(_swap_halves(kn_w[MLA_NOPE:])),
        jnp.tile(diff_q_norm_w[L], 2), jnp.tile(diff_k_norm_w[L], 2)])
    lamv = _pad_lanes(jnp.stack([lambda_q1[L], lambda_k1[L], lambda_q2[L], lambda_k2[L]]))

    inv_freq = np.float32(ROPE_THETA) ** (-np.arange(0, MLA_ROPE, 2, dtype=np.float32) / np.float32(MLA_ROPE))
    ang = np.arange(S, dtype=np.float32)[:, None] * inv_freq[None, :].astype(np.float32)
    cos, sin = np.cos(ang).astype(np.float32), np.sin(ang).astype(np.float32)
    zpad = np.zeros((S, LANES - MLA_ROPE), np.float32)
    cos128 = jnp.asarray(np.concatenate([cos, cos, zpad], axis=1))
    sin128 = jnp.asarray(np.concatenate([-sin, sin, zpad], axis=1))

    qm, km, vm, qd, kd, vd = _prep(x2, attn_norm_w[L][None], win_r, q_a_norm_w[L][None], wuq_r,
                                   kv_a_norm_w[L][None], wukv_b, vecs, cos128, sin128, S)
    nbias = _relbias(rel_bias, ATT_TQ, ATT_TK)
    om = _attention(qm, km, vm, B, S, MLA_HEADS, 2 * LANES, diff=False)
    od = _attention(qd, kd, vd, B, S, DIFF_HEADS, LANES, diff=True,
                    extra=(nbias, lamv, diff_subln_w[L][None]))
    return om, od


def _moe_block(om, od, x2, w_o, ffn_norm_w, router_w, router_b, w_gate_up, b_gate_up, w_down, b_down):
    L = 0
    T, D = x2.shape
    wo_b = w_o[L].astype(BF16)
    rw_pad = _pad_lanes(router_w[L])
    rb_pad = _pad_lanes(router_b[L][None, :])

    x1, h, route, cnt = _router(om, od, x2, wo_b, ffn_norm_w[L][None], rw_pad, rb_pad)

    tm = MOE_TM
    n_sb = -(-T * TOP_K // tm) + N_EXPERTS
    idx = route[:, 0:TOP_K].astype(jnp.int32)
    rank = route[:, 2 * TOP_K:3 * TOP_K].astype(jnp.int32)
    counts = cnt[0, :N_EXPERTS].astype(jnp.int32)
    padded = (counts + tm - 1) // tm * tm
    pad_ends = jnp.cumsum(padded)
    pad_starts = pad_ends - padded
    dest = pad_starts[idx] + rank
    tok = jnp.broadcast_to(jnp.arange(T, dtype=jnp.int32)[:, None], (T, TOP_K))
    slot_tok = jnp.zeros((n_sb * tm,), jnp.int32).at[dest.reshape(-1)].set(tok.reshape(-1), unique_indices=True)
    sb_start = jnp.arange(n_sb, dtype=jnp.int32) * tm
    sb_e = jnp.minimum(jnp.sum((pad_ends[None, :] <= sb_start[:, None]).astype(jnp.int32), axis=1), N_EXPERTS - 1)
    sb_valid = jnp.clip(pad_starts[sb_e] + counts[sb_e] - sb_start, 0, tm)
    sb_valid = jnp.where(sb_start < pad_ends[-1], sb_valid, 0)
    sb_nsub = (sb_valid + SUB - 1) // SUB
    n_used = (pad_ends[-1] // tm).astype(jnp.int32)[None]
    blk_act = (jnp.arange(n_sb * NSUB, dtype=jnp.int32) % NSUB < jnp.repeat(sb_nsub, NSUB)).astype(jnp.int32)

    xs = _dispatch(blk_act, slot_tok.reshape(n_sb * NSUB, 1, SUB), h)
    act = _expert_call(_expert_a_kernel, "expert_a", sb_e, sb_nsub, n_used, xs, w_gate_up[L],
                       b_gate_up[L][:, None, :], D_FF, BF16, MOE_TN // 2)
    y = _expert_call(_expert_b_kernel, "expert_b", sb_e, sb_nsub, n_used, act, w_down[L],
                     b_down[L][:, None, :], D // 2, jnp.uint32, MOE_TN // 2)
    return _combine(dest.reshape(T // CMB_TM, 1, CMB_TM * TOP_K), y, x1, route)


def kernel(x, attn_norm_w, w_in, q_a_norm_w, w_uq, kv_a_norm_w, w_ukv, mla_q_norm_w, mla_k_norm_w, diff_q_norm_w, diff_k_norm_w, lambda_q1, lambda_k1, lambda_q2, lambda_k2, diff_subln_w, w_o, ffn_norm_w, router_w, router_b, w_gate_up, b_gate_up, w_down, b_down, rel_bias):
    B, S, D = x.shape
    x2 = x.reshape(B * S, D)
    om, od = _token_mixers(x2, B, S, attn_norm_w, w_in, q_a_norm_w, w_uq, kv_a_norm_w, w_ukv, mla_q_norm_w,
                           mla_k_norm_w, diff_q_norm_w, diff_k_norm_w, lambda_q1, lambda_k1, lambda_q2, lambda_k2,
                           diff_subln_w, rel_bias)
    out = _moe_block(om, od, x2, w_o, ffn_norm_w, router_w, router_b, w_gate_up, b_gate_up, w_down, b_down)
    return out.reshape(B, S, D)
```

```python
import functools
import math

import jax
import jax.numpy as jnp
import numpy as np
from jax import lax
from jax.experimental import pallas as pl
from jax.experimental.pallas import tpu as pltpu

D_MODEL = 2048
CHUNK = 64
MLA_HEADS = 8
MLA_NOPE = 128
MLA_ROPE = 64
MLA_V = 128
MLA_Q_RANK = 384
MLA_KV_RANK = 256
ROPE_THETA = 10000.0
DIFF_HEADS = 8
DIFF_DH = 64
DIFF_V = 128
REL_BUCKETS = 32
REL_MAX_DIST = 128
N_EXPERTS = 32
TOP_K = 4
D_FF = 2048
SWIGLU_LIMIT = 7.0
SWIGLU_ALPHA = 1.702
EPS = 1e-6
LAMBDA_INIT = 0.8 - 0.6 * math.exp(-0.3 * 0)

LANES = 128
MXU_DIM = 256
VMEM_LIMIT = 48 * 1024 * 1024

PREP_TM = 256
ATT_TQ = 512
ATT_TK = 512
ATT_HP = 4
RT_TM = 512
SUB = 256
MOE_TM = 1024
NSUB = MOE_TM // SUB
MOE_TN = 1024
CMB_TM = 128
HALF_D = D_MODEL // 2
TOK_ROWS = HALF_D // LANES

LOG2E = math.log2(math.e)
NEG = -1e30
F32 = jnp.float32
BF16 = jnp.bfloat16

_C_CQ = 0
_C_CKV = MLA_Q_RANK
_C_KR = _C_CKV + MLA_KV_RANK
_C_KRR = _C_KR + LANES
_C_MLA_END = _C_KRR + LANES
_C_DQ = _C_MLA_END
_C_DK = _C_DQ + DIFF_HEADS * 2 * DIFF_DH
_C_DV = _C_DK + DIFF_HEADS * 2 * DIFF_DH
_C_END = _C_DV + DIFF_HEADS * DIFF_V


def _cparams(sem):
    return pltpu.CompilerParams(dimension_semantics=sem, vmem_limit_bytes=VMEM_LIMIT)


def _const_spec(shape):
    nd = len(shape)
    return pl.BlockSpec(shape, lambda *_: (0,) * nd, pipeline_mode=pl.Buffered(1))


def _pack_bf16_pair(lo, hi):
    lo_b = lax.bitcast_convert_type(lo.astype(BF16).astype(F32), jnp.uint32)
    hi_b = lax.bitcast_convert_type(hi.astype(BF16).astype(F32), jnp.uint32)
    return (lo_b >> 16) | (hi_b & jnp.uint32(0xFFFF0000))


def _unpack_bf16_pair(w):
    return (lax.bitcast_convert_type(w << 16, F32),
            lax.bitcast_convert_type(w & jnp.uint32(0xFFFF0000), F32))


def _prep_kernel(x_ref, anw_ref, win_ref, qaw_ref, wuq_ref, kvaw_ref, wukv_ref, vec_ref, cos_ref, sin_ref,
                 qm_ref, km_ref, vm_ref, qd_ref, kd_ref, vd_ref):
    xf = x_ref[...]
    inv = lax.rsqrt(jnp.mean(xf * xf, axis=-1, keepdims=True) + EPS)
    h = (xf * inv * anw_ref[...]).astype(BF16)

    def rms(v, w):
        return v * lax.rsqrt(jnp.mean(v * v, axis=-1, keepdims=True) + EPS) * w

    pm = jnp.dot(h, win_ref[:, _C_CQ:_C_MLA_END], preferred_element_type=F32)
    cqn = rms(pm[:, _C_CQ:_C_CKV], qaw_ref[...]).astype(BF16)
    ckn = rms(pm[:, _C_CKV:_C_KR], kvaw_ref[...]).astype(BF16)
    kr = pm[:, _C_KR:_C_KRR]
    krr = pm[:, _C_KRR:_C_MLA_END]
    qall = jnp.dot(cqn, wuq_ref[...], preferred_element_type=F32)
    kvall = jnp.dot(ckn, wukv_ref[...], preferred_element_type=F32)

    vec = vec_ref[...]
    wqn, wqr, wqrr = vec[0:1], vec[1:2], vec[2:3]
    wkn, wkr, wkrr = vec[3:4], vec[4:5], vec[5:6]
    wdq, wdk = vec[6:7], vec[7:8]
    cos = cos_ref[...]
    sin = sin_ref[...]
    cq_r, sq_r = wqr * cos, wqrr * sin
    ck_r, sk_r = wkr * cos, wkrr * sin
    kr2 = kr * kr
    k_rope_base = kr * ck_r + krr * sk_r
    q_scale = (MLA_NOPE + MLA_ROPE) ** -0.5 * LOG2E
    inv_qk = 1.0 / (MLA_NOPE + MLA_ROPE)
    nq = MLA_HEADS * LANES
    for hh in range(MLA_HEADS):
        a = hh * LANES
        qn = qall[:, a:a + LANES]
        qr = qall[:, nq + a:nq + a + LANES]
        qrr = qall[:, 2 * nq + a:2 * nq + a + LANES]
        iq = lax.rsqrt(jnp.sum(qn * qn + qr * qr, axis=-1, keepdims=True) * inv_qk + EPS) * q_scale
        qm_ref[:, 2 * a:2 * a + LANES] = (qn * iq * wqn).astype(BF16)
        qm_ref[:, 2 * a + LANES:2 * a + 2 * LANES] = ((qr * cq_r + qrr * sq_r) * iq).astype(BF16)
        kn = kvall[:, 2 * a:2 * a + LANES]
        ik = lax.rsqrt(jnp.sum(kn * kn + kr2, axis=-1, keepdims=True) * inv_qk + EPS)
        km_ref[:, 2 * a:2 * a + LANES] = (kn * ik * wkn).astype(BF16)
        km_ref[:, 2 * a + LANES:2 * a + 2 * LANES] = (k_rope_base * ik).astype(BF16)
        vm_ref[:, a:a + LANES] = kvall[:, 2 * a + LANES:2 * a + 2 * LANES].astype(BF16)

    dq = jnp.dot(h, win_ref[:, _C_DQ:_C_DK], preferred_element_type=F32)
    dk = jnp.dot(h, win_ref[:, _C_DK:_C_DV], preferred_element_type=F32)
    vd_ref[...] = jnp.dot(h, win_ref[:, _C_DV:_C_END], preferred_element_type=F32).astype(BF16)
    lane = lax.broadcasted_iota(jnp.int32, (xf.shape[0], LANES), 1)
    lo = lane < DIFF_DH
    d_scale = DIFF_DH ** -0.5 * LOG2E
    inv_dh = 1.0 / DIFF_DH

    def half_norm(v):
        sq = v * v
        s1 = jnp.sum(jnp.where(lo, sq, 0.0), axis=-1, keepdims=True)
        s2 = jnp.sum(jnp.where(lo, 0.0, sq), axis=-1, keepdims=True)
        return v * jnp.where(lo, lax.rsqrt(s1 * inv_dh + EPS), lax.rsqrt(s2 * inv_dh + EPS))

    for hh in range(DIFF_HEADS):
        a = hh * LANES
        qn = half_norm(dq[:, a:a + LANES]) * (wdq * d_scale)
        qd_ref[:, 2 * a:2 * a + LANES] = jnp.where(lo, qn, 0.0).astype(BF16)
        qd_ref[:, 2 * a + LANES:2 * a + 2 * LANES] = jnp.where(lo, 0.0, qn).astype(BF16)
        kd_ref[:, a:a + LANES] = (half_norm(dk[:, a:a + LANES]) * wdk).astype(BF16)


def _prep(x2, anw, win_r, qaw, wuq_r, kvaw, wukv_b, vecs, cos128, sin128, seq):
    T = x2.shape[0]
    tm = PREP_TM
    nseq = seq // tm
    row = lambda i: (i, 0)
    outs = [
        jax.ShapeDtypeStruct((T, MLA_HEADS * 2 * LANES), BF16),
        jax.ShapeDtypeStruct((T, MLA_HEADS * 2 * LANES), BF16),
        jax.ShapeDtypeStruct((T, MLA_HEADS * MLA_V), BF16),
        jax.ShapeDtypeStruct((T, DIFF_HEADS * 2 * LANES), BF16),
        jax.ShapeDtypeStruct((T, DIFF_HEADS * LANES), BF16),
        jax.ShapeDtypeStruct((T, DIFF_HEADS * DIFF_V), BF16),
    ]
    return pl.pallas_call(
        _prep_kernel,
        grid=(T // tm,),
        in_specs=[
            pl.BlockSpec((tm, D_MODEL), row),
            _const_spec(anw.shape), _const_spec(win_r.shape), _const_spec(qaw.shape), _const_spec(wuq_r.shape),
            _const_spec(kvaw.shape), _const_spec(wukv_b.shape), _const_spec(vecs.shape),
            pl.BlockSpec((tm, LANES), lambda i: (i % nseq, 0)),
            pl.BlockSpec((tm, LANES), lambda i: (i % nseq, 0)),
        ],
        out_specs=[pl.BlockSpec((tm, o.shape[1]), row) for o in outs],
        out_shape=outs,
        compiler_params=_cparams(("parallel",)),
        name="prep",
    )(x2, anw, win_r, qaw, wuq_r, kvaw, wukv_b, vecs, cos128, sin128)


def _relbias_kernel(rb_ref, o_ref, *, tq, tk):
    hh = pl.program_id(0)
    which = pl.program_id(1)
    row = lax.broadcasted_iota(jnp.int32, (tq, tk), 0)
    col = lax.broadcasted_iota(jnp.int32, (tq, tk), 1)
    rel = col - which * tk - row
    nb = REL_BUCKETS // 2
    max_exact = nb // 2
    ret = jnp.where(rel > 0, nb, 0)
    n = jnp.abs(rel)
    nf = jnp.maximum(n, 1).astype(F32)
    large = max_exact + (jnp.log(nf / max_exact) / math.log(REL_MAX_DIST / max_exact)
                         * (nb - max_exact)).astype(jnp.int32)
    large = jnp.minimum(large, nb - 1)
    bucket = ret + jnp.where(n < max_exact, n, large)
    bias = jnp.zeros((tq, tk), F32)
    for b in range(REL_BUCKETS):
        bias = jnp.where(bucket == b, rb_ref[b, hh], bias)
    bias = (bias - rb_ref[nb - 1, hh]) * LOG2E
    allowed = ((col // CHUNK) <= (row // CHUNK)) | (which > 0)
    o_ref[0, 0] = jnp.where(allowed, bias, NEG)


def _relbias(rel_bias, tq, tk):
    return pl.pallas_call(
        functools.partial(_relbias_kernel, tq=tq, tk=tk),
        grid=(DIFF_HEADS, 2),
        in_specs=[pl.BlockSpec(memory_space=pltpu.SMEM)],
        out_specs=pl.BlockSpec((1, 1, tq, tk), lambda h, w: (h, w, 0, 0)),
        out_shape=jax.ShapeDtypeStruct((DIFF_HEADS, 2, tq, tk), F32),
        compiler_params=_cparams(("parallel", "parallel")),
        name="relbias",
    )(rel_bias)


def _attn_kernel(*refs, diff, tq, tk):
    if diff:
        q_ref, k_ref, v_ref, nb_ref, lam_ref, sub_ref, o_ref, m_ref, acc_ref = refs
    else:
        q_ref, k_ref, v_ref, o_ref, m_ref, acc_ref = refs
    i = pl.program_id(2)
    hp = o_ref.shape[1] // LANES
    dk = k_ref.shape[1] // hp
    qs = []
    for a in range(hp):
        blk = q_ref[:, a * 2 * LANES:(a + 1) * 2 * LANES]
        qs.append(jnp.concatenate([blk[:, :LANES], blk[:, LANES:]], axis=0) if diff else blk)
    m_ref[...] = jnp.full(m_ref.shape, NEG, F32)
    acc_ref[...] = jnp.zeros(acc_ref.shape, F32)
    ones = jnp.ones((tk, LANES), BF16)

    def step(j, biases):
        start = pl.multiple_of(j * tk, tk)
        for a in range(hp):
            ks = k_ref[pl.ds(start, tk), a * dk:(a + 1) * dk]
            vs = jnp.concatenate([v_ref[pl.ds(start, tk), a * LANES:(a + 1) * LANES], ones], axis=1)
            s = lax.dot_general(qs[a], ks, (((1,), (1,)), ((), ())), preferred_element_type=F32)
            if biases is not None:
                s = s + biases[a]
            m_prev = m_ref[a]
            m_new = jnp.maximum(m_prev, jnp.max(s, axis=1, keepdims=True))
            alpha = jnp.exp2(m_prev - m_new)
            p = jnp.exp2((s - jnp.concatenate([m_new] * (tk // LANES), axis=1)).astype(BF16))
            acc_ref[a] = (jnp.concatenate([alpha, alpha], axis=1) * acc_ref[a]
                          + jnp.dot(p, vs, preferred_element_type=F32))
            m_ref[a] = m_new

    def far_tiles(n):
        def pair(j2, c):
            step(2 * j2, None)
            step(2 * j2 + 1, None)
            return c

        lax.fori_loop(0, n // 2, pair, 0)

        @pl.when(n % 2 == 1)
        def _():
            step(n - 1, None)

    far_tiles(jnp.maximum(i - 1, 0))

    def last_tiles(with_prev):
        if diff:
            if with_prev:
                step(i - 1, [jnp.concatenate([nb_ref[a, 1]] * 2, axis=0) for a in range(hp)])
            step(i, [jnp.concatenate([nb_ref[a, 0]] * 2, axis=0) for a in range(hp)])
        else:
            if with_prev:
                step(i - 1, None)
            row = lax.broadcasted_iota(jnp.int32, (tq, tk), 0)
            col = lax.broadcasted_iota(jnp.int32, (tq, tk), 1)
            step(i, [jnp.where((col // CHUNK) <= (row // CHUNK), 0.0, NEG)] * hp)

    @pl.when(i > 0)
    def _():
        last_tiles(True)

    @pl.when(i == 0)
    def _():
        last_tiles(False)

    for a in range(hp):
        o = acc_ref[a, :, 0:LANES] / acc_ref[a, :, LANES:2 * LANES]
        if diff:
            lv = lam_ref[...]
            lam = (jnp.exp(jnp.sum(lv[0:1] * lv[1:2], axis=1, keepdims=True))
                   - jnp.exp(jnp.sum(lv[2:3] * lv[3:4], axis=1, keepdims=True)) + LAMBDA_INIT)
            od = o[:tq] - lam * o[tq:]
            od = od * lax.rsqrt(jnp.mean(od * od, axis=-1, keepdims=True) + EPS) * sub_ref[...]
            o_ref[:, a * LANES:(a + 1) * LANES] = (od * (1.0 - LAMBDA_INIT)).astype(BF16)
        else:
            o_ref[:, a * LANES:(a + 1) * LANES] = o.astype(BF16)


def _attention(q, k, v, batch, seq, heads, dk, diff, extra=()):
    tq, tk, hp = ATT_TQ, ATT_TK, ATT_HP
    nq = seq // tq
    rows = 2 * tq if diff else tq
    in_specs = [
        pl.BlockSpec((tq, hp * 2 * LANES), lambda b, h, i: (b * nq + i, h)),
        pl.BlockSpec((seq, hp * dk), lambda b, h, i: (b, h), pipeline_mode=pl.Buffered(1)),
        pl.BlockSpec((seq, hp * LANES), lambda b, h, i: (b, h), pipeline_mode=pl.Buffered(1)),
    ]
    if diff:
        nbias, lamv, subw = extra
        in_specs += [
            pl.BlockSpec((hp, 2, tq, tk), lambda b, h, i: (h, 0, 0, 0), pipeline_mode=pl.Buffered(1)),
            pl.BlockSpec(lamv.shape, lambda b, h, i: (0, 0)),
            pl.BlockSpec(subw.shape, lambda b, h, i: (0, 0)),
        ]
    return pl.pallas_call(
        functools.partial(_attn_kernel, diff=diff, tq=tq, tk=tk),
        grid=(batch, heads // hp, nq),
        in_specs=in_specs,
        out_specs=pl.BlockSpec((tq, hp * LANES), lambda b, h, i: (b * nq + i, h)),
        out_shape=jax.ShapeDtypeStruct((batch * seq, heads * LANES), BF16),
        scratch_shapes=[pltpu.VMEM((hp, rows, LANES), F32),
                        pltpu.VMEM((hp, rows, 2 * LANES), F32)],
        compiler_params=_cparams(("parallel", "parallel", "arbitrary")),
        name="attn_diff" if diff else "attn_mla",
    )(q, k, v, *extra)


def _router_kernel(om_ref, od_ref, x_ref, wo_ref, fw_ref, rw_ref, rb_ref,
                   x1_ref, h_ref, route_ref, cnt_ref, carry_ref):
    tm = x_ref.shape[0]
    half = om_ref.shape[1]

    @pl.when(pl.program_id(0) == 0)
    def _():
        carry_ref[...] = jnp.zeros(carry_ref.shape, F32)

    y = (jnp.dot(om_ref[...], wo_ref[0:half, :], preferred_element_type=F32)
         + jnp.dot(od_ref[...], wo_ref[half:2 * half, :], preferred_element_type=F32))
    x1 = x_ref[...] + y
    x1_ref[...] = x1
    hh = x1 * lax.rsqrt(jnp.mean(x1 * x1, axis=-1, keepdims=True) + EPS) * fw_ref[...]
    for c in range(TOK_ROWS):
        h_ref[pl.ds(c, tm, stride=TOK_ROWS), :] = _pack_bf16_pair(
            hh[:, c * LANES:(c + 1) * LANES], hh[:, HALF_D + c * LANES:HALF_D + (c + 1) * LANES])
    rw = rw_ref[...]
    rw_hi = rw.astype(BF16)
    rw_lo = (rw - rw_hi.astype(F32)).astype(BF16)
    hh_hi = hh.astype(BF16)
    hh_lo = (hh - hh_hi.astype(F32)).astype(BF16)
    logits = (jnp.dot(hh_hi, rw_hi, preferred_element_type=F32) + jnp.dot(hh_lo, rw_hi, preferred_element_type=F32)
              + jnp.dot(hh_hi, rw_lo, preferred_element_type=F32)) + rb_ref[...]
    lane = lax.broadcasted_iota(jnp.int32, (tm, LANES), 1).astype(F32)
    work = jnp.where(lane < N_EXPERTS, logits, -jnp.inf)
    vals, idxs = [], []
    for _ in range(TOP_K):
        mx = jnp.max(work, axis=-1, keepdims=True)
        ix = jnp.min(jnp.where(work == mx, lane, float(LANES)), axis=-1, keepdims=True)
        vals.append(mx)
        idxs.append(ix)
        work = jnp.where(lane == ix, -jnp.inf, work)
    es = [jnp.exp(v - vals[0]) for v in vals]
    den = es[0] + es[1] + es[2] + es[3]
    onehot = jnp.zeros((tm, LANES), F32)
    for ix in idxs:
        onehot = onehot + jnp.where(lane == ix, 1.0, 0.0)
    r_i = lax.broadcasted_iota(jnp.int32, (tm, tm), 0)
    c_i = lax.broadcasted_iota(jnp.int32, (tm, tm), 1)
    tri = jnp.where(c_i < r_i, 1.0, 0.0).astype(BF16)
    prefix = jnp.dot(tri, onehot.astype(BF16), preferred_element_type=F32) + carry_ref[...]
    route = jnp.zeros((tm, LANES), F32)
    for kk in range(TOP_K):
        rank = jnp.sum(jnp.where(lane == idxs[kk], prefix, 0.0), axis=-1, keepdims=True)
        route = jnp.where(lane == kk, idxs[kk], route)
        route = jnp.where(lane == TOP_K + kk, es[kk] / den, route)
        route = jnp.where(lane == 2 * TOP_K + kk, rank, route)
    route_ref[...] = route
    carry = carry_ref[...] + jnp.sum(onehot, axis=0, keepdims=True)
    carry_ref[...] = carry
    cnt_ref[...] = carry


def _router(om, od, x2, wo_b, fw, rw_pad, rb_pad):
    T = x2.shape[0]
    tm = RT_TM
    row = lambda i: (i, 0)
    outs = [
        jax.ShapeDtypeStruct((T, D_MODEL), F32),
        jax.ShapeDtypeStruct((T * TOK_ROWS, LANES), jnp.uint32),
        jax.ShapeDtypeStruct((T, LANES), F32),
        jax.ShapeDtypeStruct((1, LANES), F32),
    ]
    return pl.pallas_call(
        _router_kernel,
        grid=(T // tm,),
        in_specs=[
            pl.BlockSpec((tm, om.shape[1]), row), pl.BlockSpec((tm, od.shape[1]), row),
            pl.BlockSpec((tm, D_MODEL), row),
            _const_spec(wo_b.shape), _const_spec(fw.shape), _const_spec(rw_pad.shape), _const_spec(rb_pad.shape),
        ],
        out_specs=[pl.BlockSpec((tm, D_MODEL), row), pl.BlockSpec((tm * TOK_ROWS, LANES), row),
                   pl.BlockSpec((tm, LANES), row), pl.BlockSpec((1, LANES), lambda i: (0, 0))],
        out_shape=outs,
        scratch_shapes=[pltpu.VMEM((1, LANES), F32)],
        compiler_params=_cparams(("arbitrary",)),
        name="router",
    )(om, od, x2, wo_b, fw, rw_pad, rb_pad)


def _row_copy(src_hbm, idx, dst_ref, r, sem):
    return pltpu.make_async_copy(src_hbm.at[pl.ds(idx, 1), :], dst_ref.at[pl.ds(r, 1), :], sem)


def _token_copy(h_hbm, tok, dst_ref, r, sem):
    return pltpu.make_async_copy(h_hbm.at[pl.ds(tok * TOK_ROWS, TOK_ROWS), :],
                                 dst_ref.at[pl.ds(r * TOK_ROWS, TOK_ROWS), :], sem)


def _dispatch_kernel(act_ref, tok_ref, nxt_ref, h_hbm, o_ref, buf_ref, sem):
    i = pl.program_id(0)
    n = pl.num_programs(0)
    rows = o_ref.shape[0]

    def gather(idx_ref, slot):
        def issue(r2, c):
            for p in range(2):
                r = 2 * r2 + p
                _token_copy(h_hbm, idx_ref[0, 0, r], buf_ref.at[slot], r, sem.at[slot]).start(priority=p)
            return c
        lax.fori_loop(0, rows // 2, issue, 0, unroll=4)

    @pl.when((i == 0) & (act_ref[0] > 0))
    def _():
        gather(tok_ref, 0)

    @pl.when((i + 1 < n) & (act_ref[jnp.minimum(i + 1, n - 1)] > 0))
    def _():
        gather(nxt_ref, (i + 1) % 2)

    @pl.when(act_ref[i] > 0)
    def _():
        slot = i % 2
        pltpu.make_async_copy(h_hbm.at[pl.ds(0, rows * TOK_ROWS), :], buf_ref.at[slot], sem.at[slot]).wait()
        for c in range(TOK_ROWS):
            lo, hi = _unpack_bf16_pair(buf_ref[slot, pl.ds(c, rows, stride=TOK_ROWS), :])
            o_ref[:, c * LANES:(c + 1) * LANES] = lo.astype(BF16)
            o_ref[:, HALF_D + c * LANES:HALF_D + (c + 1) * LANES] = hi.astype(BF16)

    @pl.when(act_ref[i] == 0)
    def _():
        o_ref[...] = jnp.zeros(o_ref.shape, o_ref.dtype)


def _dispatch(blk_act, slot_tok3, h):
    nblk, _, rows = slot_tok3.shape
    grid_spec = pltpu.PrefetchScalarGridSpec(
        num_scalar_prefetch=1,
        grid=(nblk,),
        in_specs=[pl.BlockSpec((1, 1, rows), lambda i, act: (i, 0, 0), memory_space=pltpu.SMEM),
                  pl.BlockSpec((1, 1, rows), lambda i, act: (jnp.minimum(i + 1, nblk - 1), 0, 0),
                               memory_space=pltpu.SMEM),
                  pl.BlockSpec(memory_space=pl.ANY)],
        out_specs=pl.BlockSpec((rows, D_MODEL), lambda i, act: (i, 0)),
        scratch_shapes=[pltpu.VMEM((2, rows * TOK_ROWS, LANES), jnp.uint32), pltpu.SemaphoreType.DMA((2,))],
    )
    return pl.pallas_call(
        _dispatch_kernel,
        grid_spec=grid_spec,
        out_shape=jax.ShapeDtypeStruct((nblk * rows, D_MODEL), BF16),
        compiler_params=_cparams(("arbitrary",)),
        name="dispatch",
    )(blk_act, slot_tok3, slot_tok3, h)


def _cache_weights(se_ref, m, w_ref, wb_ref):
    prev = se_ref[jnp.maximum(m - 1, 0)]

    @pl.when((m == 0) | (se_ref[m] != prev))
    def _():
        wb_ref[...] = w_ref[0].astype(BF16)


def _for_row_count(nsub, o_ref, compute):
    @pl.when(nsub == 0)
    def _():
        o_ref[...] = jnp.zeros(o_ref.shape, o_ref.dtype)

    for s in range(1, NSUB + 1):
        @pl.when(nsub == s)
        def _(s=s):
            rows = s * SUB
            compute(rows)
            if rows < o_ref.shape[0]:
                o_ref[rows:, :] = jnp.zeros((o_ref.shape[0] - rows, o_ref.shape[1]), o_ref.dtype)


def _expert_a_kernel(se_ref, ns_ref, nu_ref, x_ref, w_ref, b_ref, o_ref, wb_ref):
    m = pl.program_id(1)
    tn = w_ref.shape[2]
    nsub = ns_ref[m]

    @pl.when(nsub > 0)
    def _():
        _cache_weights(se_ref, m, w_ref, wb_ref)

    def compute(rows):
        r_i = lax.broadcasted_iota(jnp.int32, (MXU_DIM, LANES), 0)
        c_i = lax.broadcasted_iota(jnp.int32, (MXU_DIM, LANES), 1)
        sel = jnp.where(r_i == 2 * c_i, 1.0, 0.0).astype(BF16)
        gu = jnp.dot(x_ref[0:rows, :], wb_ref[...], preferred_element_type=F32) + b_ref[0]
        for c in range(tn // MXU_DIM):
            parts = []
            for s in range(MXU_DIM // LANES):
                g = gu[:, c * MXU_DIM + s * LANES:c * MXU_DIM + (s + 1) * LANES]
                gate = jnp.minimum(g, SWIGLU_LIMIT)
                up1 = jnp.clip(g, -SWIGLU_LIMIT, SWIGLU_LIMIT) + 1.0
                act = gate * jax.nn.sigmoid(SWIGLU_ALPHA * gate)
                parts.append((act * pltpu.roll(up1, LANES - 1, 1)).astype(BF16))
            pair = jnp.concatenate(parts, axis=1)
            o_ref[0:rows, c * LANES:(c + 1) * LANES] = jnp.dot(
                pair, sel, preferred_element_type=F32).astype(BF16)

    _for_row_count(nsub, o_ref, compute)


def _expert_b_kernel(se_ref, ns_ref, nu_ref, a_ref, w_ref, b_ref, o_ref, wb_ref):
    m = pl.program_id(1)
    nsub = ns_ref[m]

    @pl.when(nsub > 0)
    def _():
        _cache_weights(se_ref, m, w_ref, wb_ref)

    def compute(rows):
        y = jnp.dot(a_ref[0:rows, :], wb_ref[...], preferred_element_type=F32) + b_ref[0]
        half = y.shape[1] // 2
        o_ref[0:rows, :] = _pack_bf16_pair(y[:, :half], y[:, half:])

    _for_row_count(nsub, o_ref, compute)


def _expert_call(kernel, name, sb_e, sb_nsub, n_used, xin, w, b3, out_cols, out_dtype, out_tn):
    n_sb = sb_e.shape[0]
    tm, tn = MOE_TM, MOE_TN
    kdim = w.shape[1]
    nj = w.shape[2] // tn

    def blk(m, nu):
        return jnp.maximum(jnp.minimum(m, nu[0] - 1), 0)

    grid_spec = pltpu.PrefetchScalarGridSpec(
        num_scalar_prefetch=3,
        grid=(nj, n_sb),
        in_specs=[
            pl.BlockSpec((tm, kdim), lambda j, m, se, ns, nu: (blk(m, nu), 0)),
            pl.BlockSpec((1, kdim, tn), lambda j, m, se, ns, nu: (se[blk(m, nu)], 0, j)),
            pl.BlockSpec((1, 1, tn), lambda j, m, se, ns, nu: (se[blk(m, nu)], 0, j)),
        ],
        out_specs=pl.BlockSpec((tm, out_tn), lambda j, m, se, ns, nu: (m, j)),
        scratch_shapes=[pltpu.VMEM((kdim, tn), BF16)],
    )
    return pl.pallas_call(
        kernel,
        grid_spec=grid_spec,
        out_shape=jax.ShapeDtypeStruct((n_sb * tm, out_cols), out_dtype),
        compiler_params=_cparams(("arbitrary", "arbitrary")),
        name=name,
    )(sb_e, sb_nsub, n_used, xin, w, b3)


def _combine_kernel(dest_ref, y_hbm, x1_ref, route_ref, o_ref, buf_ref, sem):
    tm = o_ref.shape[0]

    def issue(r, c):
        for kk in range(TOP_K):
            _row_copy(y_hbm, dest_ref[0, 0, r * TOP_K + kk], buf_ref.at[kk], r, sem).start(priority=kk % 2)
        return c

    lax.fori_loop(0, tm, issue, 0, unroll=4)
    for kk in range(TOP_K):
        pltpu.make_async_copy(y_hbm.at[pl.ds(0, tm), :], buf_ref.at[kk], sem).wait()
    route = route_ref[...]
    hw = MOE_TN // 2
    for j in range(D_MODEL // MOE_TN):
        lo_acc = x1_ref[:, j * MOE_TN:j * MOE_TN + hw]
        hi_acc = x1_ref[:, j * MOE_TN + hw:(j + 1) * MOE_TN]
        for kk in range(TOP_K):
            gate = route[:, TOP_K + kk:TOP_K + kk + 1]
            lo, hi = _unpack_bf16_pair(buf_ref[kk, :, j * hw:(j + 1) * hw])
            lo_acc = lo_acc + gate * lo
            hi_acc = hi_acc + gate * hi
        o_ref[:, j * MOE_TN:j * MOE_TN + hw] = lo_acc
        o_ref[:, j * MOE_TN + hw:(j + 1) * MOE_TN] = hi_acc


def _combine(dest3, y, x1, route):
    T = x1.shape[0]
    tm = CMB_TM
    row = lambda i: (i, 0)
    return pl.pallas_call(
        _combine_kernel,
        grid=(T // tm,),
        in_specs=[pl.BlockSpec((1, 1, tm * TOP_K), lambda i: (i, 0, 0), memory_space=pltpu.SMEM),
                  pl.BlockSpec(memory_space=pl.ANY),
                  pl.BlockSpec((tm, D_MODEL), row),
                  pl.BlockSpec((tm, LANES), row)],
        out_specs=pl.BlockSpec((tm, D_MODEL), row),
        out_shape=jax.ShapeDtypeStruct((T, D_MODEL), F32),
        scratch_shapes=[pltpu.VMEM((TOP_K, tm, HALF_D), jnp.uint32), pltpu.SemaphoreType.DMA(())],
        compiler_params=_cparams(("arbitrary",)),
        name="combine",
    )(dest3, y, x1, route)


def _pad_lanes(v, width=LANES):
    return jnp.pad(v, [(0, 0)] * (v.ndim - 1) + [(0, width - v.shape[-1])])


def _swap_halves(v):
    half = v.shape[-1] // 2
    return jnp.concatenate([v[..., half:], v[..., :half]], axis=-1)


def _token_mixers(x2, B, S, attn_norm_w, w_in, q_a_norm_w, w_uq, kv_a_norm_w, w_ukv, mla_q_norm_w, mla_k_norm_w,
                  diff_q_norm_w, diff_k_norm_w, lambda_q1, lambda_k1, lambda_q2, lambda_k2, diff_subln_w, rel_bias):
    L = 0
    win = w_in[L]
    sp = np.cumsum([MLA_Q_RANK, MLA_KV_RANK, MLA_ROPE])
    w_kr = win[:, sp[1]:sp[2]]
    win_r = jnp.concatenate(
        [win[:, :sp[1]], _pad_lanes(w_kr), _pad_lanes(_swap_halves(w_kr)), win[:, sp[2]:]], axis=1).astype(BF16)
    qk = MLA_NOPE + MLA_ROPE
    wuq3 = w_uq[L].reshape(MLA_Q_RANK, MLA_HEADS, qk)
    wuq_rope = wuq3[:, :, MLA_NOPE:]
    wuq_r = jnp.concatenate([
        wuq3[:, :, :MLA_NOPE].reshape(MLA_Q_RANK, -1),
        _pad_lanes(wuq_rope).reshape(MLA_Q_RANK, -1),
        _pad_lanes(_swap_halves(wuq_rope)).reshape(MLA_Q_RANK, -1)], axis=1).astype(BF16)
    wukv_b = w_ukv[L].astype(BF16)
    qn_w, kn_w = mla_q_norm_w[L], mla_k_norm_w[L]
    vecs = jnp.stack([
        qn_w[:MLA_NOPE], _pad_lanes(qn_w[MLA_NOPE:]), _pad_lanes(_swap_halves(qn_w[MLA_NOPE:])),
        kn_w[:MLA_NOPE], _pad_lanes(kn_w[MLA_NOPE:]), _pad_lanes(_swap_halves(kn_w[MLA_NOPE:])),
        jnp.tile(diff_q_norm_w[L], 2), jnp.tile(diff_k_norm_w[L], 2)])
    lamv = _pad_lanes(jnp.stack([lambda_q1[L], lambda_k1[L], lambda_q2[L], lambda_k2[L]]))

    inv_freq = np.float32(ROPE_THETA) ** (-np.arange(0, MLA_ROPE, 2, dtype=np.float32) / np.float32(MLA_ROPE))
    ang = np.arange(S, dtype=np.float32)[:, None] * inv_freq[None, :].astype(np.float32)
    cos, sin = np.cos(ang).astype(np.float32), np.sin(ang).astype(np.float32)
    zpad = np.zeros((S, LANES - MLA_ROPE), np.float32)
    cos128 = jnp.asarray(np.concatenate([cos, cos, zpad], axis=1))
    sin128 = jnp.asarray(np.concatenate([-sin, sin, zpad], axis=1))

    qm, km, vm, qd, kd, vd = _prep(x2, attn_norm_w[L][None], win_r, q_a_norm_w[L][None], wuq_r,
                                   kv_a_norm_w[L][None], wukv_b, vecs, cos128, sin128, S)
    nbias = _relbias(rel_bias, ATT_TQ, ATT_TK)
    om = _attention(qm, km, vm, B, S, MLA_HEADS, 2 * LANES, diff=False)
    od = _attention(qd, kd, vd, B, S, DIFF_HEADS, LANES, diff=True,
                    extra=(nbias, lamv, diff_subln_w[L][None]))
    return om, od


def _moe_block(om, od, x2, w_o, ffn_norm_w, router_w, router_b, w_gate_up, b_gate_up, w_down, b_down):
    L = 0
    T, D = x2.shape
    wo_b = w_o[L].astype(BF16)
    rw_pad = _pad_lanes(router_w[L])
    rb_pad = _pad_lanes(router_b[L][None, :])

    x1, h, route, cnt = _router(om, od, x2, wo_b, ffn_norm_w[L][None], rw_pad, rb_pad)

    tm = MOE_TM
    n_sb = -(-T * TOP_K // tm) + N_EXPERTS
    idx = route[:, 0:TOP_K].astype(jnp.int32)
    rank = route[:, 2 * TOP_K:3 * TOP_K].astype(jnp.int32)
    counts = cnt[0, :N_EXPERTS].astype(jnp.int32)
    padded = (counts + tm - 1) // tm * tm
    pad_ends = jnp.cumsum(padded)
    pad_starts = pad_ends - padded
    dest = pad_starts[idx] + rank
    tok = jnp.broadcast_to(jnp.arange(T, dtype=jnp.int32)[:, None], (T, TOP_K))
    slot_tok = jnp.zeros((n_sb * tm,), jnp.int32).at[dest.reshape(-1)].set(tok.reshape(-1), unique_indices=True)
    sb_start = jnp.arange(n_sb, dtype=jnp.int32) * tm
    sb_e = jnp.minimum(jnp.sum((pad_ends[None, :] <= sb_start[:, None]).astype(jnp.int32), axis=1), N_EXPERTS - 1)
    sb_valid = jnp.clip(pad_starts[sb_e] + counts[sb_e] - sb_start, 0, tm)
    sb_valid = jnp.where(sb_start < pad_ends[-1], sb_valid, 0)
    sb_nsub = (sb_valid + SUB - 1) // SUB
    n_used = (pad_ends[-1] // tm).astype(jnp.int32)[None]
    blk_act = (jnp.arange(n_sb * NSUB, dtype=jnp.int32) % NSUB < jnp.repeat(sb_nsub, NSUB)).astype(jnp.int32)

    xs = _dispatch(blk_act, slot_tok.reshape(n_sb * NSUB, 1, SUB), h)
    act = _expert_call(_expert_a_kernel, "expert_a", sb_e, sb_nsub, n_used, xs, w_gate_up[L],
                       b_gate_up[L][:, None, :], D_FF, BF16, MOE_TN // 2)
    y = _expert_call(_expert_b_kernel, "expert_b", sb_e, sb_nsub, n_used, act, w_down[L],
                     b_down[L][:, None, :], D // 2, jnp.uint32, MOE_TN // 2)
    return _combine(dest.reshape(T // CMB_TM, 1, CMB_TM * TOP_K), y, x1, route)


def kernel(x, attn_norm_w, w_in, q_a_norm_w, w_uq, kv_a_norm_w, w_ukv, mla_q_norm_w, mla_k_norm_w, diff_q_norm_w, diff_k_norm_w, lambda_q1, lambda_k1, lambda_q2, lambda_k2, diff_subln_w, w_o, ffn_norm_w, router_w, router_b, w_gate_up, b_gate_up, w_down, b_down, rel_bias):
    B, S, D = x.shape
    x2 = x.reshape(B * S, D)
    om, od = _token_mixers(x2, B, S, attn_norm_w, w_in, q_a_norm_w, w_uq, kv_a_norm_w, w_ukv, mla_q_norm_w,
                           mla_k_norm_w, diff_q_norm_w, diff_k_norm_w, lambda_q1, lambda_k1, lambda_q2, lambda_k2,
                           diff_subln_w, rel_bias)
    out = _moe_block(om, od, x2, w_o, ffn_norm_w, router_w, router_b, w_gate_up, b_gate_up, w_down, b_down)
    return out.reshape(B, S, D)
```

```python
import functools
import math

import jax
import jax.numpy as jnp
import numpy as np
from jax import lax
from jax.experimental import pallas as pl
from jax.experimental.pallas import tpu as pltpu

D_MODEL = 2048
CHUNK = 64
MLA_HEADS = 8
MLA_NOPE = 128
MLA_ROPE = 64
MLA_V = 128
MLA_Q_RANK = 384
MLA_KV_RANK = 256
ROPE_THETA = 10000.0
DIFF_HEADS = 8
DIFF_DH = 64
DIFF_V = 128
REL_BUCKETS = 32
REL_MAX_DIST = 128
N_EXPERTS = 32
TOP_K = 4
D_FF = 2048
SWIGLU_LIMIT = 7.0
SWIGLU_ALPHA = 1.702
EPS = 1e-6
LAMBDA_INIT = 0.8 - 0.6 * math.exp(-0.3 * 0)

LANES = 128
MXU_DIM = 256
VMEM_LIMIT = 48 * 1024 * 1024

PREP_TM = 256
ATT_TQ = 512
ATT_TK = 512
ATT_HP = 4
RT_TM = 512
SUB = 256
MOE_TM = 1024
NSUB = MOE_TM // SUB
MOE_TN = 1024
CMB_TM = 128
HALF_D = D_MODEL // 2
DSP_TM = 256

LOG2E = math.log2(math.e)
NEG = -1e30
F32 = jnp.float32
BF16 = jnp.bfloat16

_C_CQ = 0
_C_CKV = MLA_Q_RANK
_C_KR = _C_CKV + MLA_KV_RANK
_C_KRR = _C_KR + LANES
_C_MLA_END = _C_KRR + LANES
_C_DQ = _C_MLA_END
_C_DK = _C_DQ + DIFF_HEADS * 2 * DIFF_DH
_C_DV = _C_DK + DIFF_HEADS * 2 * DIFF_DH
_C_END = _C_DV + DIFF_HEADS * DIFF_V


def _cparams(sem):
    return pltpu.CompilerParams(dimension_semantics=sem, vmem_limit_bytes=VMEM_LIMIT)


def _const_spec(shape):
    nd = len(shape)
    return pl.BlockSpec(shape, lambda *_: (0,) * nd, pipeline_mode=pl.Buffered(1))


def _pack_bf16_pair(lo, hi):
    lo_b = lax.bitcast_convert_type(lo.astype(BF16).astype(F32), jnp.uint32)
    hi_b = lax.bitcast_convert_type(hi.astype(BF16).astype(F32), jnp.uint32)
    return (lo_b >> 16) | (hi_b & jnp.uint32(0xFFFF0000))


def _unpack_bf16_pair(w):
    return (lax.bitcast_convert_type(w << 16, F32),
            lax.bitcast_convert_type(w & jnp.uint32(0xFFFF0000), F32))


def _prep_kernel(x_ref, anw_ref, win_ref, qaw_ref, wuq_ref, kvaw_ref, wukv_ref, vec_ref, cos_ref, sin_ref,
                 qm_ref, km_ref, vm_ref, qd_ref, kd_ref, vd_ref):
    xf = x_ref[...]
    inv = lax.rsqrt(jnp.mean(xf * xf, axis=-1, keepdims=True) + EPS)
    h = (xf * inv * anw_ref[...]).astype(BF16)

    def rms(v, w):
        return v * lax.rsqrt(jnp.mean(v * v, axis=-1, keepdims=True) + EPS) * w

    pm = jnp.dot(h, win_ref[:, _C_CQ:_C_MLA_END], preferred_element_type=F32)
    cqn = rms(pm[:, _C_CQ:_C_CKV], qaw_ref[...]).astype(BF16)
    ckn = rms(pm[:, _C_CKV:_C_KR], kvaw_ref[...]).astype(BF16)
    kr = pm[:, _C_KR:_C_KRR]
    krr = pm[:, _C_KRR:_C_MLA_END]
    qall = jnp.dot(cqn, wuq_ref[...], preferred_element_type=F32)
    kvall = jnp.dot(ckn, wukv_ref[...], preferred_element_type=F32)

    vec = vec_ref[...]
    wqn, wqr, wqrr = vec[0:1], vec[1:2], vec[2:3]
    wkn, wkr, wkrr = vec[3:4], vec[4:5], vec[5:6]
    wdq, wdk = vec[6:7], vec[7:8]
    cos = cos_ref[...]
    sin = sin_ref[...]
    cq_r, sq_r = wqr * cos, wqrr * sin
    ck_r, sk_r = wkr * cos, wkrr * sin
    kr2 = kr * kr
    k_rope_base = kr * ck_r + krr * sk_r
    q_scale = (MLA_NOPE + MLA_ROPE) ** -0.5 * LOG2E
    inv_qk = 1.0 / (MLA_NOPE + MLA_ROPE)
    nq = MLA_HEADS * LANES
    for hh in range(MLA_HEADS):
        a = hh * LANES
        qn = qall[:, a:a + LANES]
        qr = qall[:, nq + a:nq + a + LANES]
        qrr = qall[:, 2 * nq + a:2 * nq + a + LANES]
        iq = lax.rsqrt(jnp.sum(qn * qn + qr * qr, axis=-1, keepdims=True) * inv_qk + EPS) * q_scale
        qm_ref[:, 2 * a:2 * a + LANES] = (qn * iq * wqn).astype(BF16)
        qm_ref[:, 2 * a + LANES:2 * a + 2 * LANES] = ((qr * cq_r + qrr * sq_r) * iq).astype(BF16)
        kn = kvall[:, 2 * a:2 * a + LANES]
        ik = lax.rsqrt(jnp.sum(kn * kn + kr2, axis=-1, keepdims=True) * inv_qk + EPS)
        km_ref[:, 2 * a:2 * a + LANES] = (kn * ik * wkn).astype(BF16)
        km_ref[:, 2 * a + LANES:2 * a + 2 * LANES] = (k_rope_base * ik).astype(BF16)
        vm_ref[:, a:a + LANES] = kvall[:, 2 * a + LANES:2 * a + 2 * LANES].astype(BF16)

    dq = jnp.dot(h, win_ref[:, _C_DQ:_C_DK], preferred_element_type=F32)
    dk = jnp.dot(h, win_ref[:, _C_DK:_C_DV], preferred_element_type=F32)
    vd_ref[...] = jnp.dot(h, win_ref[:, _C_DV:_C_END], preferred_element_type=F32).astype(BF16)
    lane = lax.broadcasted_iota(jnp.int32, (xf.shape[0], LANES), 1)
    lo = lane < DIFF_DH
    d_scale = DIFF_DH ** -0.5 * LOG2E
    inv_dh = 1.0 / DIFF_DH

    def half_norm(v):
        sq = v * v
        s1 = jnp.sum(jnp.where(lo, sq, 0.0), axis=-1, keepdims=True)
        s2 = jnp.sum(jnp.where(lo, 0.0, sq), axis=-1, keepdims=True)
        return v * jnp.where(lo, lax.rsqrt(s1 * inv_dh + EPS), lax.rsqrt(s2 * inv_dh + EPS))

    for hh in range(DIFF_HEADS):
        a = hh * LANES
        qn = half_norm(dq[:, a:a + LANES]) * (wdq * d_scale)
        qd_ref[:, 2 * a:2 * a + LANES] = jnp.where(lo, qn, 0.0).astype(BF16)
        qd_ref[:, 2 * a + LANES:2 * a + 2 * LANES] = jnp.where(lo, 0.0, qn).astype(BF16)
        kd_ref[:, a:a + LANES] = (half_norm(dk[:, a:a + LANES]) * wdk).astype(BF16)


def _prep(x2, anw, win_r, qaw, wuq_r, kvaw, wukv_b, vecs, cos128, sin128, seq):
    T = x2.shape[0]
    tm = PREP_TM
    nseq = seq // tm
    row = lambda i: (i, 0)
    outs = [
        jax.ShapeDtypeStruct((T, MLA_HEADS * 2 * LANES), BF16),
        jax.ShapeDtypeStruct((T, MLA_HEADS * 2 * LANES), BF16),
        jax.ShapeDtypeStruct((T, MLA_HEADS * MLA_V), BF16),
        jax.ShapeDtypeStruct((T, DIFF_HEADS * 2 * LANES), BF16),
        jax.ShapeDtypeStruct((T, DIFF_HEADS * LANES), BF16),
        jax.ShapeDtypeStruct((T, DIFF_HEADS * DIFF_V), BF16),
    ]
    return pl.pallas_call(
        _prep_kernel,
        grid=(T // tm,),
        in_specs=[
            pl.BlockSpec((tm, D_MODEL), row),
            _const_spec(anw.shape), _const_spec(win_r.shape), _const_spec(qaw.shape), _const_spec(wuq_r.shape),
            _const_spec(kvaw.shape), _const_spec(wukv_b.shape), _const_spec(vecs.shape),
            pl.BlockSpec((tm, LANES), lambda i: (i % nseq, 0)),
            pl.BlockSpec((tm, LANES), lambda i: (i % nseq, 0)),
        ],
        out_specs=[pl.BlockSpec((tm, o.shape[1]), row) for o in outs],
        out_shape=outs,
        compiler_params=_cparams(("parallel",)),
        name="prep",
    )(x2, anw, win_r, qaw, wuq_r, kvaw, wukv_b, vecs, cos128, sin128)


def _relbias_kernel(rb_ref, o_ref, *, tq, tk):
    hh = pl.program_id(0)
    which = pl.program_id(1)
    row = lax.broadcasted_iota(jnp.int32, (tq, tk), 0)
    col = lax.broadcasted_iota(jnp.int32, (tq, tk), 1)
    rel = col - which * tk - row
    nb = REL_BUCKETS // 2
    max_exact = nb // 2
    ret = jnp.where(rel > 0, nb, 0)
    n = jnp.abs(rel)
    nf = jnp.maximum(n, 1).astype(F32)
    large = max_exact + (jnp.log(nf / max_exact) / math.log(REL_MAX_DIST / max_exact)
                         * (nb - max_exact)).astype(jnp.int32)
    large = jnp.minimum(large, nb - 1)
    bucket = ret + jnp.where(n < max_exact, n, large)
    bias = jnp.zeros((tq, tk), F32)
    for b in range(REL_BUCKETS):
        bias = jnp.where(bucket == b, rb_ref[b, hh], bias)
    bias = (bias - rb_ref[nb - 1, hh]) * LOG2E
    allowed = ((col // CHUNK) <= (row // CHUNK)) | (which > 0)
    o_ref[0, 0] = jnp.where(allowed, bias, NEG)


def _relbias(rel_bias, tq, tk):
    return pl.pallas_call(
        functools.partial(_relbias_kernel, tq=tq, tk=tk),
        grid=(DIFF_HEADS, 2),
        in_specs=[pl.BlockSpec(memory_space=pltpu.SMEM)],
        out_specs=pl.BlockSpec((1, 1, tq, tk), lambda h, w: (h, w, 0, 0)),
        out_shape=jax.ShapeDtypeStruct((DIFF_HEADS, 2, tq, tk), F32),
        compiler_params=_cparams(("parallel", "parallel")),
        name="relbias",
    )(rel_bias)


def _attn_kernel(*refs, diff, tq, tk):
    if diff:
        q_ref, k_ref, v_ref, nb_ref, lam_ref, sub_ref, o_ref, m_ref, acc_ref = refs
    else:
        q_ref, k_ref, v_ref, o_ref, m_ref, acc_ref = refs
    i = pl.program_id(2)
    hp = o_ref.shape[1] // LANES
    dk = k_ref.shape[1] // hp
    qs = []
    for a in range(hp):
        blk = q_ref[:, a * 2 * LANES:(a + 1) * 2 * LANES]
        qs.append(jnp.concatenate([blk[:, :LANES], blk[:, LANES:]], axis=0) if diff else blk)
    m_ref[...] = jnp.full(m_ref.shape, NEG, F32)
    acc_ref[...] = jnp.zeros(acc_ref.shape, F32)
    ones = jnp.ones((tk, LANES), BF16)

    def step(j, biases):
        start = pl.multiple_of(j * tk, tk)
        for a in range(hp):
            ks = k_ref[pl.ds(start, tk), a * dk:(a + 1) * dk]
            vs = jnp.concatenate([v_ref[pl.ds(start, tk), a * LANES:(a + 1) * LANES], ones], axis=1)
            s = lax.dot_general(qs[a], ks, (((1,), (1,)), ((), ())), preferred_element_type=F32)
            if biases is not None:
                s = s + biases[a]
            m_prev = m_ref[a]
            m_new = jnp.maximum(m_prev, jnp.max(s, axis=1, keepdims=True))
            alpha = jnp.exp2(m_prev - m_new)
            p = jnp.exp2((s - jnp.concatenate([m_new] * (tk // LANES), axis=1)).astype(BF16))
            acc_ref[a] = (jnp.concatenate([alpha, alpha], axis=1) * acc_ref[a]
                          + jnp.dot(p, vs, preferred_element_type=F32))
            m_ref[a] = m_new

    def far_tiles(n):
        def pair(j2, c):
            step(2 * j2, None)
            step(2 * j2 + 1, None)
            return c

        lax.fori_loop(0, n // 2, pair, 0)

        @pl.when(n % 2 == 1)
        def _():
            step(n - 1, None)

    far_tiles(jnp.maximum(i - 1, 0))

    def last_tiles(with_prev):
        if diff:
            if with_prev:
                step(i - 1, [jnp.concatenate([nb_ref[a, 1]] * 2, axis=0) for a in range(hp)])
            step(i, [jnp.concatenate([nb_ref[a, 0]] * 2, axis=0) for a in range(hp)])
        else:
            if with_prev:
                step(i - 1, None)
            row = lax.broadcasted_iota(jnp.int32, (tq, tk), 0)
            col = lax.broadcasted_iota(jnp.int32, (tq, tk), 1)
            step(i, [jnp.where((col // CHUNK) <= (row // CHUNK), 0.0, NEG)] * hp)

    @pl.when(i > 0)
    def _():
        last_tiles(True)

    @pl.when(i == 0)
    def _():
        last_tiles(False)

    for a in range(hp):
        o = acc_ref[a, :, 0:LANES] / acc_ref[a, :, LANES:2 * LANES]
        if diff:
            lv = lam_ref[...]
            lam = (jnp.exp(jnp.sum(lv[0:1] * lv[1:2], axis=1, keepdims=True))
                   - jnp.exp(jnp.sum(lv[2:3] * lv[3:4], axis=1, keepdims=True)) + LAMBDA_INIT)
            od = o[:tq] - lam * o[tq:]
            od = od * lax.rsqrt(jnp.mean(od * od, axis=-1, keepdims=True) + EPS) * sub_ref[...]
            o_ref[:, a * LANES:(a + 1) * LANES] = (od * (1.0 - LAMBDA_INIT)).astype(BF16)
        else:
            o_ref[:, a * LANES:(a + 1) * LANES] = o.astype(BF16)


def _attention(q, k, v, batch, seq, heads, dk, diff, extra=()):
    tq, tk, hp = ATT_TQ, ATT_TK, ATT_HP
    nq = seq // tq
    rows = 2 * tq if diff else tq
    in_specs = [
        pl.BlockSpec((tq, hp * 2 * LANES), lambda b, h, i: (b * nq + i, h)),
        pl.BlockSpec((seq, hp * dk), lambda b, h, i: (b, h), pipeline_mode=pl.Buffered(1)),
        pl.BlockSpec((seq, hp * LANES), lambda b, h, i: (b, h), pipeline_mode=pl.Buffered(1)),
    ]
    if diff:
        nbias, lamv, subw = extra
        in_specs += [
            pl.BlockSpec((hp, 2, tq, tk), lambda b, h, i: (h, 0, 0, 0), pipeline_mode=pl.Buffered(1)),
            pl.BlockSpec(lamv.shape, lambda b, h, i: (0, 0)),
            pl.BlockSpec(subw.shape, lambda b, h, i: (0, 0)),
        ]
    return pl.pallas_call(
        functools.partial(_attn_kernel, diff=diff, tq=tq, tk=tk),
        grid=(batch, heads // hp, nq),
        in_specs=in_specs,
        out_specs=pl.BlockSpec((tq, hp * LANES), lambda b, h, i: (b * nq + i, h)),
        out_shape=jax.ShapeDtypeStruct((batch * seq, heads * LANES), BF16),
        scratch_shapes=[pltpu.VMEM((hp, rows, LANES), F32),
                        pltpu.VMEM((hp, rows, 2 * LANES), F32)],
        compiler_params=_cparams(("parallel", "parallel", "arbitrary")),
        name="attn_diff" if diff else "attn_mla",
    )(q, k, v, *extra)


def _router_kernel(om_ref, od_ref, x_ref, wo_ref, fw_ref, rw_ref, rb_ref,
                   x1_ref, h_ref, route_ref, cnt_ref, carry_ref):
    tm = x_ref.shape[0]
    half = om_ref.shape[1]

    @pl.when(pl.program_id(0) == 0)
    def _():
        carry_ref[...] = jnp.zeros(carry_ref.shape, F32)

    y = (jnp.dot(om_ref[...], wo_ref[0:half, :], preferred_element_type=F32)
         + jnp.dot(od_ref[...], wo_ref[half:2 * half, :], preferred_element_type=F32))
    x1 = x_ref[...] + y
    x1_ref[...] = x1
    hh = x1 * lax.rsqrt(jnp.mean(x1 * x1, axis=-1, keepdims=True) + EPS) * fw_ref[...]
    h_ref[...] = _pack_bf16_pair(hh[:, :HALF_D], hh[:, HALF_D:])
    rw = rw_ref[...]
    rw_hi = rw.astype(BF16)
    rw_lo = (rw - rw_hi.astype(F32)).astype(BF16)
    hh_hi = hh.astype(BF16)
    hh_lo = (hh - hh_hi.astype(F32)).astype(BF16)
    logits = (jnp.dot(hh_hi, rw_hi, preferred_element_type=F32) + jnp.dot(hh_lo, rw_hi, preferred_element_type=F32)
              + jnp.dot(hh_hi, rw_lo, preferred_element_type=F32)) + rb_ref[...]
    lane = lax.broadcasted_iota(jnp.int32, (tm, LANES), 1).astype(F32)
    work = jnp.where(lane < N_EXPERTS, logits, -jnp.inf)
    vals, idxs = [], []
    for _ in range(TOP_K):
        mx = jnp.max(work, axis=-1, keepdims=True)
        ix = jnp.min(jnp.where(work == mx, lane, float(LANES)), axis=-1, keepdims=True)
        vals.append(mx)
        idxs.append(ix)
        work = jnp.where(lane == ix, -jnp.inf, work)
    es = [jnp.exp(v - vals[0]) for v in vals]
    den = es[0] + es[1] + es[2] + es[3]
    onehot = jnp.zeros((tm, LANES), F32)
    for ix in idxs:
        onehot = onehot + jnp.where(lane == ix, 1.0, 0.0)
    r_i = lax.broadcasted_iota(jnp.int32, (tm, tm), 0)
    c_i = lax.broadcasted_iota(jnp.int32, (tm, tm), 1)
    tri = jnp.where(c_i < r_i, 1.0, 0.0).astype(BF16)
    prefix = jnp.dot(tri, onehot.astype(BF16), preferred_element_type=F32) + carry_ref[...]
    route = jnp.zeros((tm, LANES), F32)
    for kk in range(TOP_K):
        rank = jnp.sum(jnp.where(lane == idxs[kk], prefix, 0.0), axis=-1, keepdims=True)
        route = jnp.where(lane == kk, idxs[kk], route)
        route = jnp.where(lane == TOP_K + kk, es[kk] / den, route)
        route = jnp.where(lane == 2 * TOP_K + kk, rank, route)
    route_ref[...] = route
    carry = carry_ref[...] + jnp.sum(onehot, axis=0, keepdims=True)
    carry_ref[...] = carry
    cnt_ref[...] = carry


def _router(om, od, x2, wo_b, fw, rw_pad, rb_pad):
    T = x2.shape[0]
    tm = RT_TM
    row = lambda i: (i, 0)
    outs = [
        jax.ShapeDtypeStruct((T, D_MODEL), F32),
        jax.ShapeDtypeStruct((T, HALF_D), jnp.uint32),
        jax.ShapeDtypeStruct((T, LANES), F32),
        jax.ShapeDtypeStruct((1, LANES), F32),
    ]
    return pl.pallas_call(
        _router_kernel,
        grid=(T // tm,),
        in_specs=[
            pl.BlockSpec((tm, om.shape[1]), row), pl.BlockSpec((tm, od.shape[1]), row),
            pl.BlockSpec((tm, D_MODEL), row),
            _const_spec(wo_b.shape), _const_spec(fw.shape), _const_spec(rw_pad.shape), _const_spec(rb_pad.shape),
        ],
        out_specs=[pl.BlockSpec((tm, D_MODEL), row), pl.BlockSpec((tm, HALF_D), row),
                   pl.BlockSpec((tm, LANES), row), pl.BlockSpec((1, LANES), lambda i: (0, 0))],
        out_shape=outs,
        scratch_shapes=[pltpu.VMEM((1, LANES), F32)],
        compiler_params=_cparams(("arbitrary",)),
        name="router",
    )(om, od, x2, wo_b, fw, rw_pad, rb_pad)


def _row_copy(src_ref, src_row, dst_ref, dst_row, sem):
    return pltpu.make_async_copy(src_ref.at[pl.ds(src_row, 1), :], dst_ref.at[pl.ds(dst_row, 1), :], sem)


def _dispatch_kernel(sv_ref, dest_ref, h_ref, xs_hbm, zero_ref, sem):
    tm = h_ref.shape[0]

    @pl.when(pl.program_id(0) == 0)
    def _():
        zero_ref[...] = jnp.zeros(zero_ref.shape, zero_ref.dtype)

        def zero_copy(sb):
            return pltpu.make_async_copy(zero_ref, xs_hbm.at[pl.ds(sb * MOE_TM, MOE_TM), :], sem)

        def start(sb, c):
            @pl.when(sv_ref[sb] < MOE_TM)
            def _():
                zero_copy(sb).start()
            return c

        def wait(sb, c):
            @pl.when(sv_ref[sb] < MOE_TM)
            def _():
                zero_copy(sb).wait()
            return c

        lax.fori_loop(0, sv_ref.shape[0], start, 0)
        lax.fori_loop(0, sv_ref.shape[0], wait, 0)

    def issue(r, c):
        for kk in range(TOP_K):
            _row_copy(h_ref, r, xs_hbm, dest_ref[0, 0, r * TOP_K + kk], sem).start(priority=kk % 2)
        return c

    lax.fori_loop(0, tm, issue, 0, unroll=4)
    for kk in range(TOP_K):
        pltpu.make_async_copy(h_ref, xs_hbm.at[pl.ds(0, tm), :], sem).wait()


def _dispatch(sb_valid, dest3, h, n_slots):
    T = h.shape[0]
    tm = DSP_TM
    grid_spec = pltpu.PrefetchScalarGridSpec(
        num_scalar_prefetch=1,
        grid=(T // tm,),
        in_specs=[pl.BlockSpec((1, 1, tm * TOP_K), lambda i, sv: (i, 0, 0), memory_space=pltpu.SMEM),
                  pl.BlockSpec((tm, HALF_D), lambda i, sv: (i, 0))],
        out_specs=pl.BlockSpec(memory_space=pl.ANY),
        scratch_shapes=[pltpu.VMEM((MOE_TM, HALF_D), jnp.uint32), pltpu.SemaphoreType.DMA(())],
    )
    return pl.pallas_call(
        _dispatch_kernel,
        grid_spec=grid_spec,
        out_shape=jax.ShapeDtypeStruct((n_slots, HALF_D), jnp.uint32),
        compiler_params=_cparams(("arbitrary",)),
        name="dispatch",
    )(sb_valid, dest3, h)


def _cache_weights(se_ref, m, w_ref, wb_ref):
    prev = se_ref[jnp.maximum(m - 1, 0)]

    @pl.when((m == 0) | (se_ref[m] != prev))
    def _():
        wb_ref[...] = w_ref[0].astype(BF16)


def _for_row_count(nsub, o_ref, compute):
    @pl.when(nsub == 0)
    def _():
        o_ref[...] = jnp.zeros(o_ref.shape, o_ref.dtype)

    for s in range(1, NSUB + 1):
        @pl.when(nsub == s)
        def _(s=s):
            rows = s * SUB
            compute(rows)
            if rows < o_ref.shape[0]:
                o_ref[rows:, :] = jnp.zeros((o_ref.shape[0] - rows, o_ref.shape[1]), o_ref.dtype)


def _expert_a_kernel(se_ref, ns_ref, nu_ref, x_ref, w_ref, b_ref, o_ref, wb_ref):
    m = pl.program_id(1)
    tn = w_ref.shape[2]
    nsub = ns_ref[m]

    @pl.when(nsub > 0)
    def _():
        _cache_weights(se_ref, m, w_ref, wb_ref)

    def compute(rows):
        r_i = lax.broadcasted_iota(jnp.int32, (MXU_DIM, LANES), 0)
        c_i = lax.broadcasted_iota(jnp.int32, (MXU_DIM, LANES), 1)
        sel = jnp.where(r_i == 2 * c_i, 1.0, 0.0).astype(BF16)
        lo, hi = _unpack_bf16_pair(x_ref[0:rows, :])
        x = jnp.concatenate([lo.astype(BF16), hi.astype(BF16)], axis=1)
        gu = jnp.dot(x, wb_ref[...], preferred_element_type=F32) + b_ref[0]
        for c in range(tn // MXU_DIM):
            parts = []
            for s in range(MXU_DIM // LANES):
                g = gu[:, c * MXU_DIM + s * LANES:c * MXU_DIM + (s + 1) * LANES]
                gate = jnp.minimum(g, SWIGLU_LIMIT)
                up1 = jnp.clip(g, -SWIGLU_LIMIT, SWIGLU_LIMIT) + 1.0
                act = gate * jax.nn.sigmoid(SWIGLU_ALPHA * gate)
                parts.append((act * pltpu.roll(up1, LANES - 1, 1)).astype(BF16))
            pair = jnp.concatenate(parts, axis=1)
            o_ref[0:rows, c * LANES:(c + 1) * LANES] = jnp.dot(
                pair, sel, preferred_element_type=F32).astype(BF16)

    _for_row_count(nsub, o_ref, compute)


def _expert_b_kernel(se_ref, ns_ref, nu_ref, a_ref, w_ref, b_ref, o_ref, wb_ref):
    m = pl.program_id(1)
    nsub = ns_ref[m]

    @pl.when(nsub > 0)
    def _():
        _cache_weights(se_ref, m, w_ref, wb_ref)

    def compute(rows):
        y = jnp.dot(a_ref[0:rows, :], wb_ref[...], preferred_element_type=F32) + b_ref[0]
        half = y.shape[1] // 2
        o_ref[0:rows, :] = _pack_bf16_pair(y[:, :half], y[:, half:])

    _for_row_count(nsub, o_ref, compute)


def _expert_call(kernel, name, sb_e, sb_nsub, n_used, xin, w, b3, out_cols, out_dtype, out_tn):
    n_sb = sb_e.shape[0]
    tm, tn = MOE_TM, MOE_TN
    kdim = w.shape[1]
    nj = w.shape[2] // tn

    def blk(m, nu):
        return jnp.maximum(jnp.minimum(m, nu[0] - 1), 0)

    grid_spec = pltpu.PrefetchScalarGridSpec(
        num_scalar_prefetch=3,
        grid=(nj, n_sb),
        in_specs=[
            pl.BlockSpec((tm, xin.shape[1]), lambda j, m, se, ns, nu: (blk(m, nu), 0)),
            pl.BlockSpec((1, kdim, tn), lambda j, m, se, ns, nu: (se[blk(m, nu)], 0, j)),
            pl.BlockSpec((1, 1, tn), lambda j, m, se, ns, nu: (se[blk(m, nu)], 0, j)),
        ],
        out_specs=pl.BlockSpec((tm, out_tn), lambda j, m, se, ns, nu: (m, j)),
        scratch_shapes=[pltpu.VMEM((kdim, tn), BF16)],
    )
    return pl.pallas_call(
        kernel,
        grid_spec=grid_spec,
        out_shape=jax.ShapeDtypeStruct((n_sb * tm, out_cols), out_dtype),
        compiler_params=_cparams(("arbitrary", "arbitrary")),
        name=name,
    )(sb_e, sb_nsub, n_used, xin, w, b3)


def _combine_kernel(dest_ref, y_hbm, x1_ref, route_ref, o_ref, buf_ref, sem):
    tm = o_ref.shape[0]

    def issue(r, c):
        for kk in range(TOP_K):
            _row_copy(y_hbm, dest_ref[0, 0, r * TOP_K + kk], buf_ref.at[kk], r, sem).start(priority=kk % 2)
        return c

    lax.fori_loop(0, tm, issue, 0, unroll=4)
    for kk in range(TOP_K):
        pltpu.make_async_copy(y_hbm.at[pl.ds(0, tm), :], buf_ref.at[kk], sem).wait()
    route = route_ref[...]
    hw = MOE_TN // 2
    for j in range(D_MODEL // MOE_TN):
        lo_acc = x1_ref[:, j * MOE_TN:j * MOE_TN + hw]
        hi_acc = x1_ref[:, j * MOE_TN + hw:(j + 1) * MOE_TN]
        for kk in range(TOP_K):
            gate = route[:, TOP_K + kk:TOP_K + kk + 1]
            lo, hi = _unpack_bf16_pair(buf_ref[kk, :, j * hw:(j + 1) * hw])
            lo_acc = lo_acc + gate * lo
            hi_acc = hi_acc + gate * hi
        o_ref[:, j * MOE_TN:j * MOE_TN + hw] = lo_acc
        o_ref[:, j * MOE_TN + hw:(j + 1) * MOE_TN] = hi_acc


def _combine(dest3, y, x1, route):
    T = x1.shape[0]
    tm = CMB_TM
    row = lambda i: (i, 0)
    return pl.pallas_call(
        _combine_kernel,
        grid=(T // tm,),
        in_specs=[pl.BlockSpec((1, 1, tm * TOP_K), lambda i: (i, 0, 0), memory_space=pltpu.SMEM),
                  pl.BlockSpec(memory_space=pl.ANY),
                  pl.BlockSpec((tm, D_MODEL), row),
                  pl.BlockSpec((tm, LANES), row)],
        out_specs=pl.BlockSpec((tm, D_MODEL), row),
        out_shape=jax.ShapeDtypeStruct((T, D_MODEL), F32),
        scratch_shapes=[pltpu.VMEM((TOP_K, tm, HALF_D), jnp.uint32), pltpu.SemaphoreType.DMA(())],
        compiler_params=_cparams(("arbitrary",)),
        name="combine",
    )(dest3, y, x1, route)


def _pad_lanes(v, width=LANES):
    return jnp.pad(v, [(0, 0)] * (v.ndim - 1) + [(0, width - v.shape[-1])])


def _swap_halves(v):
    half = v.shape[-1] // 2
    return jnp.concatenate([v[..., half:], v[..., :half]], axis=-1)


def _token_mixers(x2, B, S, attn_norm_w, w_in, q_a_norm_w, w_uq, kv_a_norm_w, w_ukv, mla_q_norm_w, mla_k_norm_w,
                  diff_q_norm_w, diff_k_norm_w, lambda_q1, lambda_k1, lambda_q2, lambda_k2, diff_subln_w, rel_bias):
    L = 0
    win = w_in[L]
    sp = np.cumsum([MLA_Q_RANK, MLA_KV_RANK, MLA_ROPE])
    w_kr = win[:, sp[1]:sp[2]]
    win_r = jnp.concatenate(
        [win[:, :sp[1]], _pad_lanes(w_kr), _pad_lanes(_swap_halves(w_kr)), win[:, sp[2]:]], axis=1).astype(BF16)
    qk = MLA_NOPE + MLA_ROPE
    wuq3 = w_uq[L].reshape(MLA_Q_RANK, MLA_HEADS, qk)
    wuq_rope = wuq3[:, :, MLA_NOPE:]
    wuq_r = jnp.concatenate([
        wuq3[:, :, :MLA_NOPE].reshape(MLA_Q_RANK, -1),
        _pad_lanes(wuq_rope).reshape(MLA_Q_RANK, -1),
        _pad_lanes(_swap_halves(wuq_rope)).reshape(MLA_Q_RANK, -1)], axis=1).astype(BF16)
    wukv_b = w_ukv[L].astype(BF16)
    qn_w, kn_w = mla_q_norm_w[L], mla_k_norm_w[L]
    vecs = jnp.stack([
        qn_w[:MLA_NOPE], _pad_lanes(qn_w[MLA_NOPE:]), _pad_lanes(_swap_halves(qn_w[MLA_NOPE:])),
        kn_w[:MLA_NOPE], _pad_lanes(kn_w[MLA_NOPE:]), _pad_lanes(_swap_halves(kn_w[MLA_NOPE:])),
        jnp.tile(diff_q_norm_w[L], 2), jnp.tile(diff_k_norm_w[L], 2)])
    lamv = _pad_lanes(jnp.stack([lambda_q1[L], lambda_k1[L], lambda_q2[L], lambda_k2[L]]))

    inv_freq = np.float32(ROPE_THETA) ** (-np.arange(0, MLA_ROPE, 2, dtype=np.float32) / np.float32(MLA_ROPE))
    ang = np.arange(S, dtype=np.float32)[:, None] * inv_freq[None, :].astype(np.float32)
    cos, sin = np.cos(ang).astype(np.float32), np.sin(ang).astype(np.float32)
    zpad = np.zeros((S, LANES - MLA_ROPE), np.float32)
    cos128 = jnp.asarray(np.concatenate([cos, cos, zpad], axis=1))
    sin128 = jnp.asarray(np.concatenate([-sin, sin, zpad], axis=1))

    qm, km, vm, qd, kd, vd = _prep(x2, attn_norm_w[L][None], win_r, q_a_norm_w[L][None], wuq_r,
                                   kv_a_norm_w[L][None], wukv_b, vecs, cos128, sin128, S)
    nbias = _relbias(rel_bias, ATT_TQ, ATT_TK)
    om = _attention(qm, km, vm, B, S, MLA_HEADS, 2 * LANES, diff=False)
    od = _attention(qd, kd, vd, B, S, DIFF_HEADS, LANES, diff=True,
                    extra=(nbias, lamv, diff_subln_w[L][None]))
    return om, od


def _moe_block(om, od, x2, w_o, ffn_norm_w, router_w, router_b, w_gate_up, b_gate_up, w_down, b_down):
    L = 0
    T, D = x2.shape
    wo_b = w_o[L].astype(BF16)
    rw_pad = _pad_lanes(router_w[L])
    rb_pad = _pad_lanes(router_b[L][None, :])

    x1, h, route, cnt = _router(om, od, x2, wo_b, ffn_norm_w[L][None], rw_pad, rb_pad)

    tm = MOE_TM
    n_sb = -(-T * TOP_K // tm) + N_EXPERTS
    idx = route[:, 0:TOP_K].astype(jnp.int32)
    rank = route[:, 2 * TOP_K:3 * TOP_K].astype(jnp.int32)
    counts = cnt[0, :N_EXPERTS].astype(jnp.int32)
    padded = (counts + tm - 1) // tm * tm
    pad_ends = jnp.cumsum(padded)
    pad_starts = pad_ends - padded
    dest = pad_starts[idx] + rank
    sb_start = jnp.arange(n_sb, dtype=jnp.int32) * tm
    sb_e = jnp.minimum(jnp.sum((pad_ends[None, :] <= sb_start[:, None]).astype(jnp.int32), axis=1), N_EXPERTS - 1)
    sb_valid = jnp.clip(pad_starts[sb_e] + counts[sb_e] - sb_start, 0, tm)
    sb_valid = jnp.where(sb_start < pad_ends[-1], sb_valid, 0)
    sb_nsub = (sb_valid + SUB - 1) // SUB
    n_used = (pad_ends[-1] // tm).astype(jnp.int32)[None]

    xs = _dispatch(sb_valid.astype(jnp.int32), dest.reshape(T // DSP_TM, 1, DSP_TM * TOP_K), h, n_sb * tm)
    act = _expert_call(_expert_a_kernel, "expert_a", sb_e, sb_nsub, n_used, xs, w_gate_up[L],
                       b_gate_up[L][:, None, :], D_FF, BF16, MOE_TN // 2)
    y = _expert_call(_expert_b_kernel, "expert_b", sb_e, sb_nsub, n_used, act, w_down[L],
                     b_down[L][:, None, :], D // 2, jnp.uint32, MOE_TN // 2)
    return _combine(dest.reshape(T // CMB_TM, 1, CMB_TM * TOP_K), y, x1, route)


def kernel(x, attn_norm_w, w_in, q_a_norm_w, w_uq, kv_a_norm_w, w_ukv, mla_q_norm_w, mla_k_norm_w, diff_q_norm_w, diff_k_norm_w, lambda_q1, lambda_k1, lambda_q2, lambda_k2, diff_subln_w, w_o, ffn_norm_w, router_w, router_b, w_gate_up, b_gate_up, w_down, b_down, rel_bias):
    B, S, D = x.shape
    x2 = x.reshape(B * S, D)
    om, od = _token_mixers(x2, B, S, attn_norm_w, w_in, q_a_norm_w, w_uq, kv_a_norm_w, w_ukv, mla_q_norm_w,
                           mla_k_norm_w, diff_q_norm_w, diff_k_norm_w, lambda_q1, lambda_k1, lambda_q2, lambda_k2,
                           diff_subln_w, rel_bias)
    out = _moe_block(om, od, x2, w_o, ffn_norm_w, router_w, router_b, w_gate_up, b_gate_up, w_down, b_down)
    return out.reshape(B, S, D)
```

```python
import functools
import math

import jax
import jax.numpy as jnp
import numpy as np
from jax import lax
from jax.experimental import pallas as pl
from jax.experimental.pallas import tpu as pltpu

D_MODEL = 2048
CHUNK = 64
MLA_HEADS = 8
MLA_NOPE = 128
MLA_ROPE = 64
MLA_V = 128
MLA_Q_RANK = 384
MLA_KV_RANK = 256
ROPE_THETA = 10000.0
DIFF_HEADS = 8
DIFF_DH = 64
DIFF_V = 128
REL_BUCKETS = 32
REL_MAX_DIST = 128
N_EXPERTS = 32
TOP_K = 4
D_FF = 2048
SWIGLU_LIMIT = 7.0
SWIGLU_ALPHA = 1.702
EPS = 1e-6
LAMBDA_INIT = 0.8 - 0.6 * math.exp(-0.3 * 0)

LANES = 128
MXU_DIM = 256
VMEM_LIMIT = 48 * 1024 * 1024

PREP_TM = 256
ATT_TQ = 512
ATT_TK = 512
ATT_HP = 4
RT_TM = 512
SUB = 256
MOE_TM = 1024
NSUB = MOE_TM // SUB
MOE_TN = 1024
CMB_TM = 128
HALF_D = D_MODEL // 2
DSP_TM = 256

LOG2E = math.log2(math.e)
NEG = -1e30
F32 = jnp.float32
BF16 = jnp.bfloat16

_C_CQ = 0
_C_CKV = MLA_Q_RANK
_C_KR = _C_CKV + MLA_KV_RANK
_C_KRR = _C_KR + LANES
_C_MLA_END = _C_KRR + LANES
_C_DQ = _C_MLA_END
_C_DK = _C_DQ + DIFF_HEADS * 2 * DIFF_DH
_C_DV = _C_DK + DIFF_HEADS * 2 * DIFF_DH
_C_END = _C_DV + DIFF_HEADS * DIFF_V


def _cparams(sem):
    return pltpu.CompilerParams(dimension_semantics=sem, vmem_limit_bytes=VMEM_LIMIT)


def _const_spec(shape):
    nd = len(shape)
    return pl.BlockSpec(shape, lambda *_: (0,) * nd, pipeline_mode=pl.Buffered(1))


def _pack_bf16_pair(lo, hi):
    lo_b = lax.bitcast_convert_type(lo.astype(BF16).astype(F32), jnp.uint32)
    hi_b = lax.bitcast_convert_type(hi.astype(BF16).astype(F32), jnp.uint32)
    return (lo_b >> 16) | (hi_b & jnp.uint32(0xFFFF0000))


def _unpack_bf16_pair(w):
    return (lax.bitcast_convert_type(w << 16, F32),
            lax.bitcast_convert_type(w & jnp.uint32(0xFFFF0000), F32))


def _prep_kernel(x_ref, anw_ref, win_ref, qaw_ref, wuq_ref, kvaw_ref, wukv_ref, vec_ref, cos_ref, sin_ref,
                 qm_ref, km_ref, vm_ref, qd_ref, kd_ref, vd_ref):
    xf = x_ref[...]
    inv = lax.rsqrt(jnp.mean(xf * xf, axis=-1, keepdims=True) + EPS)
    h = (xf * inv * anw_ref[...]).astype(BF16)

    def rms(v, w):
        return v * lax.rsqrt(jnp.mean(v * v, axis=-1, keepdims=True) + EPS) * w

    pm = jnp.dot(h, win_ref[:, _C_CQ:_C_MLA_END], preferred_element_type=F32)
    cqn = rms(pm[:, _C_CQ:_C_CKV], qaw_ref[...]).astype(BF16)
    ckn = rms(pm[:, _C_CKV:_C_KR], kvaw_ref[...]).astype(BF16)
    kr = pm[:, _C_KR:_C_KRR]
    krr = pm[:, _C_KRR:_C_MLA_END]
    qall = jnp.dot(cqn, wuq_ref[...], preferred_element_type=F32)
    kvall = jnp.dot(ckn, wukv_ref[...], preferred_element_type=F32)

    vec = vec_ref[...]
    wqn, wqr, wqrr = vec[0:1], vec[1:2], vec[2:3]
    wkn, wkr, wkrr = vec[3:4], vec[4:5], vec[5:6]
    wdq, wdk = vec[6:7], vec[7:8]
    cos = cos_ref[...]
    sin = sin_ref[...]
    cq_r, sq_r = wqr * cos, wqrr * sin
    ck_r, sk_r = wkr * cos, wkrr * sin
    kr2 = kr * kr
    k_rope_base = kr * ck_r + krr * sk_r
    q_scale = (MLA_NOPE + MLA_ROPE) ** -0.5 * LOG2E
    inv_qk = 1.0 / (MLA_NOPE + MLA_ROPE)
    nq = MLA_HEADS * LANES
    for hh in range(MLA_HEADS):
        a = hh * LANES
        qn = qall[:, a:a + LANES]
        qr = qall[:, nq + a:nq + a + LANES]
        qrr = qall[:, 2 * nq + a:2 * nq + a + LANES]
        iq = lax.rsqrt(jnp.sum(qn * qn + qr * qr, axis=-1, keepdims=True) * inv_qk + EPS) * q_scale
        qm_ref[:, 2 * a:2 * a + LANES] = (qn * iq * wqn).astype(BF16)
        qm_ref[:, 2 * a + LANES:2 * a + 2 * LANES] = ((qr * cq_r + qrr * sq_r) * iq).astype(BF16)
        kn = kvall[:, 2 * a:2 * a + LANES]
        ik = lax.rsqrt(jnp.sum(kn * kn + kr2, axis=-1, keepdims=True) * inv_qk + EPS)
        km_ref[:, 2 * a:2 * a + LANES] = (kn * ik * wkn).astype(BF16)
        km_ref[:, 2 * a + LANES:2 * a + 2 * LANES] = (k_rope_base * ik).astype(BF16)
        vm_ref[:, a:a + LANES] = kvall[:, 2 * a + LANES:2 * a + 2 * LANES].astype(BF16)

    dq = jnp.dot(h, win_ref[:, _C_DQ:_C_DK], preferred_element_type=F32)
    dk = jnp.dot(h, win_ref[:, _C_DK:_C_DV], preferred_element_type=F32)
    vd_ref[...] = jnp.dot(h, win_ref[:, _C_DV:_C_END], preferred_element_type=F32).astype(BF16)
    lane = lax.broadcasted_iota(jnp.int32, (xf.shape[0], LANES), 1)
    lo = lane < DIFF_DH
    d_scale = DIFF_DH ** -0.5 * LOG2E
    inv_dh = 1.0 / DIFF_DH

    def half_norm(v):
        sq = v * v
        s1 = jnp.sum(jnp.where(lo, sq, 0.0), axis=-1, keepdims=True)
        s2 = jnp.sum(jnp.where(lo, 0.0, sq), axis=-1, keepdims=True)
        return v * jnp.where(lo, lax.rsqrt(s1 * inv_dh + EPS), lax.rsqrt(s2 * inv_dh + EPS))

    for hh in range(DIFF_HEADS):
        a = hh * LANES
        qn = half_norm(dq[:, a:a + LANES]) * (wdq * d_scale)
        qd_ref[:, 2 * a:2 * a + LANES] = jnp.where(lo, qn, 0.0).astype(BF16)
        qd_ref[:, 2 * a + LANES:2 * a + 2 * LANES] = jnp.where(lo, 0.0, qn).astype(BF16)
        kd_ref[:, a:a + LANES] = (half_norm(dk[:, a:a + LANES]) * wdk).astype(BF16)


def _prep(x2, anw, win_r, qaw, wuq_r, kvaw, wukv_b, vecs, cos128, sin128, seq):
    T = x2.shape[0]
    tm = PREP_TM
    nseq = seq // tm
    row = lambda i: (i, 0)
    outs = [
        jax.ShapeDtypeStruct((T, MLA_HEADS * 2 * LANES), BF16),
        jax.ShapeDtypeStruct((T, MLA_HEADS * 2 * LANES), BF16),
        jax.ShapeDtypeStruct((T, MLA_HEADS * MLA_V), BF16),
        jax.ShapeDtypeStruct((T, DIFF_HEADS * 2 * LANES), BF16),
        jax.ShapeDtypeStruct((T, DIFF_HEADS * LANES), BF16),
        jax.ShapeDtypeStruct((T, DIFF_HEADS * DIFF_V), BF16),
    ]
    return pl.pallas_call(
        _prep_kernel,
        grid=(T // tm,),
        in_specs=[
            pl.BlockSpec((tm, D_MODEL), row),
            _const_spec(anw.shape), _const_spec(win_r.shape), _const_spec(qaw.shape), _const_spec(wuq_r.shape),
            _const_spec(kvaw.shape), _const_spec(wukv_b.shape), _const_spec(vecs.shape),
            pl.BlockSpec((tm, LANES), lambda i: (i % nseq, 0)),
            pl.BlockSpec((tm, LANES), lambda i: (i % nseq, 0)),
        ],
        out_specs=[pl.BlockSpec((tm, o.shape[1]), row) for o in outs],
        out_shape=outs,
        compiler_params=_cparams(("parallel",)),
        name="prep",
    )(x2, anw, win_r, qaw, wuq_r, kvaw, wukv_b, vecs, cos128, sin128)


def _relbias_kernel(rb_ref, o_ref, *, tq, tk):
    hh = pl.program_id(0)
    which = pl.program_id(1)
    row = lax.broadcasted_iota(jnp.int32, (tq, tk), 0)
    col = lax.broadcasted_iota(jnp.int32, (tq, tk), 1)
    rel = col - which * tk - row
    nb = REL_BUCKETS // 2
    max_exact = nb // 2
    ret = jnp.where(rel > 0, nb, 0)
    n = jnp.abs(rel)
    nf = jnp.maximum(n, 1).astype(F32)
    large = max_exact + (jnp.log(nf / max_exact) / math.log(REL_MAX_DIST / max_exact)
                         * (nb - max_exact)).astype(jnp.int32)
    large = jnp.minimum(large, nb - 1)
    bucket = ret + jnp.where(n < max_exact, n, large)
    bias = jnp.zeros((tq, tk), F32)
    for b in range(REL_BUCKETS):
        bias = jnp.where(bucket == b, rb_ref[b, hh], bias)
    bias = (bias - rb_ref[nb - 1, hh]) * LOG2E
    allowed = ((col // CHUNK) <= (row // CHUNK)) | (which > 0)
    o_ref[0, 0] = jnp.where(allowed, bias, NEG)


def _relbias(rel_bias, tq, tk):
    return pl.pallas_call(
        functools.partial(_relbias_kernel, tq=tq, tk=tk),
        grid=(DIFF_HEADS, 2),
        in_specs=[pl.BlockSpec(memory_space=pltpu.SMEM)],
        out_specs=pl.BlockSpec((1, 1, tq, tk), lambda h, w: (h, w, 0, 0)),
        out_shape=jax.ShapeDtypeStruct((DIFF_HEADS, 2, tq, tk), F32),
        compiler_params=_cparams(("parallel", "parallel")),
        name="relbias",
    )(rel_bias)


def _attn_kernel(*refs, diff, tq, tk):
    if diff:
        q_ref, k_ref, v_ref, nb_ref, lam_ref, sub_ref, o_ref, m_ref, acc_ref = refs
    else:
        q_ref, k_ref, v_ref, o_ref, m_ref, acc_ref = refs
    i = pl.program_id(2)
    hp = o_ref.shape[1] // LANES
    dk = k_ref.shape[1] // hp
    qs = []
    for a in range(hp):
        blk = q_ref[:, a * 2 * LANES:(a + 1) * 2 * LANES]
        qs.append(jnp.concatenate([blk[:, :LANES], blk[:, LANES:]], axis=0) if diff else blk)
    m_ref[...] = jnp.full(m_ref.shape, NEG, F32)
    acc_ref[...] = jnp.zeros(acc_ref.shape, F32)
    ones = jnp.ones((tk, LANES), BF16)

    def step(j, biases):
        start = pl.multiple_of(j * tk, tk)
        for a in range(hp):
            ks = k_ref[pl.ds(start, tk), a * dk:(a + 1) * dk]
            vs = jnp.concatenate([v_ref[pl.ds(start, tk), a * LANES:(a + 1) * LANES], ones], axis=1)
            s = lax.dot_general(qs[a], ks, (((1,), (1,)), ((), ())), preferred_element_type=F32)
            if biases is not None:
                s = s + biases[a]
            m_prev = m_ref[a]
            m_new = jnp.maximum(m_prev, jnp.max(s, axis=1, keepdims=True))
            alpha = jnp.exp2(m_prev - m_new)
            p = jnp.exp2((s - jnp.concatenate([m_new] * (tk // LANES), axis=1)).astype(BF16))
            acc_ref[a] = (jnp.concatenate([alpha, alpha], axis=1) * acc_ref[a]
                          + jnp.dot(p, vs, preferred_element_type=F32))
            m_ref[a] = m_new

    def far_tiles(n):
        def pair(j2, c):
            step(2 * j2, None)
            step(2 * j2 + 1, None)
            return c

        lax.fori_loop(0, n // 2, pair, 0)

        @pl.when(n % 2 == 1)
        def _():
            step(n - 1, None)

    far_tiles(jnp.maximum(i - 1, 0))

    def last_tiles(with_prev):
        if diff:
            if with_prev:
                step(i - 1, [jnp.concatenate([nb_ref[a, 1]] * 2, axis=0) for a in range(hp)])
            step(i, [jnp.concatenate([nb_ref[a, 0]] * 2, axis=0) for a in range(hp)])
        else:
            if with_prev:
                step(i - 1, None)
            row = lax.broadcasted_iota(jnp.int32, (tq, tk), 0)
            col = lax.broadcasted_iota(jnp.int32, (tq, tk), 1)
            step(i, [jnp.where((col // CHUNK) <= (row // CHUNK), 0.0, NEG)] * hp)

    @pl.when(i > 0)
    def _():
        last_tiles(True)

    @pl.when(i == 0)
    def _():
        last_tiles(False)

    for a in range(hp):
        o = acc_ref[a, :, 0:LANES] / acc_ref[a, :, LANES:2 * LANES]
        if diff:
            lv = lam_ref[...]
            lam = (jnp.exp(jnp.sum(lv[0:1] * lv[1:2], axis=1, keepdims=True))
                   - jnp.exp(jnp.sum(lv[2:3] * lv[3:4], axis=1, keepdims=True)) + LAMBDA_INIT)
            od = o[:tq] - lam * o[tq:]
            od = od * lax.rsqrt(jnp.mean(od * od, axis=-1, keepdims=True) + EPS) * sub_ref[...]
            o_ref[:, a * LANES:(a + 1) * LANES] = (od * (1.0 - LAMBDA_INIT)).astype(BF16)
        else:
            o_ref[:, a * LANES:(a + 1) * LANES] = o.astype(BF16)


def _attention(q, k, v, batch, seq, heads, dk, diff, extra=()):
    tq, tk, hp = ATT_TQ, ATT_TK, ATT_HP
    nq = seq // tq
    rows = 2 * tq if diff else tq
    in_specs = [
        pl.BlockSpec((tq, hp * 2 * LANES), lambda b, h, i: (b * nq + i, h)),
        pl.BlockSpec((seq, hp * dk), lambda b, h, i: (b, h), pipeline_mode=pl.Buffered(1)),
        pl.BlockSpec((seq, hp * LANES), lambda b, h, i: (b, h), pipeline_mode=pl.Buffered(1)),
    ]
    if diff:
        nbias, lamv, subw = extra
        in_specs += [
            pl.BlockSpec((hp, 2, tq, tk), lambda b, h, i: (h, 0, 0, 0), pipeline_mode=pl.Buffered(1)),
            pl.BlockSpec(lamv.shape, lambda b, h, i: (0, 0)),
            pl.BlockSpec(subw.shape, lambda b, h, i: (0, 0)),
        ]
    return pl.pallas_call(
        functools.partial(_attn_kernel, diff=diff, tq=tq, tk=tk),
        grid=(batch, heads // hp, nq),
        in_specs=in_specs,
        out_specs=pl.BlockSpec((tq, hp * LANES), lambda b, h, i: (b * nq + i, h)),
        out_shape=jax.ShapeDtypeStruct((batch * seq, heads * LANES), BF16),
        scratch_shapes=[pltpu.VMEM((hp, rows, LANES), F32),
                        pltpu.VMEM((hp, rows, 2 * LANES), F32)],
        compiler_params=_cparams(("parallel", "parallel", "arbitrary")),
        name="attn_diff" if diff else "attn_mla",
    )(q, k, v, *extra)


def _router_kernel(om_ref, od_ref, x_ref, wo_ref, fw_ref, rw_ref, rb_ref,
                   x1_ref, h_ref, route_ref, cnt_ref, carry_ref):
    tm = x_ref.shape[0]
    half = om_ref.shape[1]

    @pl.when(pl.program_id(0) == 0)
    def _():
        carry_ref[...] = jnp.zeros(carry_ref.shape, F32)

    y = (jnp.dot(om_ref[...], wo_ref[0:half, :], preferred_element_type=F32)
         + jnp.dot(od_ref[...], wo_ref[half:2 * half, :], preferred_element_type=F32))
    x1 = x_ref[...] + y
    x1_ref[...] = x1
    hh = x1 * lax.rsqrt(jnp.mean(x1 * x1, axis=-1, keepdims=True) + EPS) * fw_ref[...]
    h_ref[...] = _pack_bf16_pair(hh[:, :HALF_D], hh[:, HALF_D:])
    rw = rw_ref[...]
    rw_hi = rw.astype(BF16)
    rw_lo = (rw - rw_hi.astype(F32)).astype(BF16)
    hh_hi = hh.astype(BF16)
    hh_lo = (hh - hh_hi.astype(F32)).astype(BF16)
    logits = (jnp.dot(hh_hi, rw_hi, preferred_element_type=F32) + jnp.dot(hh_lo, rw_hi, preferred_element_type=F32)
              + jnp.dot(hh_hi, rw_lo, preferred_element_type=F32)) + rb_ref[...]
    lane = lax.broadcasted_iota(jnp.int32, (tm, LANES), 1).astype(F32)
    work = jnp.where(lane < N_EXPERTS, logits, -jnp.inf)
    vals, idxs = [], []
    for _ in range(TOP_K):
        mx = jnp.max(work, axis=-1, keepdims=True)
        ix = jnp.min(jnp.where(work == mx, lane, float(LANES)), axis=-1, keepdims=True)
        vals.append(mx)
        idxs.append(ix)
        work = jnp.where(lane == ix, -jnp.inf, work)
    es = [jnp.exp(v - vals[0]) for v in vals]
    den = es[0] + es[1] + es[2] + es[3]
    onehot = jnp.zeros((tm, LANES), F32)
    for ix in idxs:
        onehot = onehot + jnp.where(lane == ix, 1.0, 0.0)
    r_i = lax.broadcasted_iota(jnp.int32, (tm, tm), 0)
    c_i = lax.broadcasted_iota(jnp.int32, (tm, tm), 1)
    tri = jnp.where(c_i < r_i, 1.0, 0.0).astype(BF16)
    prefix = jnp.dot(tri, onehot.astype(BF16), preferred_element_type=F32) + carry_ref[...]
    route = jnp.zeros((tm, LANES), F32)
    for kk in range(TOP_K):
        rank = jnp.sum(jnp.where(lane == idxs[kk], prefix, 0.0), axis=-1, keepdims=True)
        route = jnp.where(lane == kk, idxs[kk], route)
        route = jnp.where(lane == TOP_K + kk, es[kk] / den, route)
        route = jnp.where(lane == 2 * TOP_K + kk, rank, route)
    route_ref[...] = route
    carry = carry_ref[...] + jnp.sum(onehot, axis=0, keepdims=True)
    carry_ref[...] = carry
    cnt_ref[...] = carry


def _router(om, od, x2, wo_b, fw, rw_pad, rb_pad):
    T = x2.shape[0]
    tm = RT_TM
    row = lambda i: (i, 0)
    outs = [
        jax.ShapeDtypeStruct((T, D_MODEL), F32),
        jax.ShapeDtypeStruct((T, HALF_D), jnp.uint32),
        jax.ShapeDtypeStruct((T, LANES), F32),
        jax.ShapeDtypeStruct((1, LANES), F32),
    ]
    return pl.pallas_call(
        _router_kernel,
        grid=(T // tm,),
        in_specs=[
            pl.BlockSpec((tm, om.shape[1]), row), pl.BlockSpec((tm, od.shape[1]), row),
            pl.BlockSpec((tm, D_MODEL), row),
            _const_spec(wo_b.shape), _const_spec(fw.shape), _const_spec(rw_pad.shape), _const_spec(rb_pad.shape),
        ],
        out_specs=[pl.BlockSpec((tm, D_MODEL), row), pl.BlockSpec((tm, HALF_D), row),
                   pl.BlockSpec((tm, LANES), row), pl.BlockSpec((1, LANES), lambda i: (0, 0))],
        out_shape=outs,
        scratch_shapes=[pltpu.VMEM((1, LANES), F32)],
        compiler_params=_cparams(("arbitrary",)),
        name="router",
    )(om, od, x2, wo_b, fw, rw_pad, rb_pad)


def _row_copy(src_ref, src_row, dst_ref, dst_row, sem):
    return pltpu.make_async_copy(src_ref.at[pl.ds(src_row, 1), :], dst_ref.at[pl.ds(dst_row, 1), :], sem)


def _dispatch_kernel(sv_ref, dest_ref, h_ref, xs_hbm, zero_ref, sem):
    tm = h_ref.shape[0]

    @pl.when(pl.program_id(0) == 0)
    def _():
        zero_ref[...] = jnp.zeros(zero_ref.shape, zero_ref.dtype)

        def zero_copy(sb):
            return pltpu.make_async_copy(zero_ref, xs_hbm.at[pl.ds(sb * MOE_TM, MOE_TM), :], sem)

        def start(sb, c):
            @pl.when(sv_ref[sb] < MOE_TM)
            def _():
                zero_copy(sb).start()
            return c

        def wait(sb, c):
            @pl.when(sv_ref[sb] < MOE_TM)
            def _():
                zero_copy(sb).wait()
            return c

        lax.fori_loop(0, sv_ref.shape[0], start, 0)
        lax.fori_loop(0, sv_ref.shape[0], wait, 0)

    def issue(r, c):
        for kk in range(TOP_K):
            _row_copy(h_ref, r, xs_hbm, dest_ref[0, 0, r * TOP_K + kk], sem).start(priority=kk % 2)
        return c

    lax.fori_loop(0, tm, issue, 0, unroll=4)
    for kk in range(TOP_K):
        pltpu.make_async_copy(h_ref, xs_hbm.at[pl.ds(0, tm), :], sem).wait()


def _dispatch(sb_valid, dest3, h, n_slots):
    T = h.shape[0]
    tm = DSP_TM
    grid_spec = pltpu.PrefetchScalarGridSpec(
        num_scalar_prefetch=1,
        grid=(T // tm,),
        in_specs=[pl.BlockSpec((1, 1, tm * TOP_K), lambda i, sv: (i, 0, 0), memory_space=pltpu.SMEM),
                  pl.BlockSpec((tm, HALF_D), lambda i, sv: (i, 0))],
        out_specs=pl.BlockSpec(memory_space=pl.ANY),
        scratch_shapes=[pltpu.VMEM((MOE_TM, HALF_D), jnp.uint32), pltpu.SemaphoreType.DMA(())],
    )
    return pl.pallas_call(
        _dispatch_kernel,
        grid_spec=grid_spec,
        out_shape=jax.ShapeDtypeStruct((n_slots, HALF_D), jnp.uint32),
        compiler_params=_cparams(("arbitrary",)),
        name="dispatch",
    )(sb_valid, dest3, h)


def _stream_weights(se_ref, nx_ref, m, w_hbm, wst_ref, wb_ref, sem):
    j = pl.program_id(0)
    nj = pl.num_programs(0)
    tn = wb_ref.shape[1]

    def tile_copy(e, jj):
        return pltpu.make_async_copy(w_hbm.at[e, :, pl.ds(pl.multiple_of(jj * tn, tn), tn)], wst_ref, sem)

    @pl.when((j == 0) & (m == 0))
    def _():
        tile_copy(se_ref[0], 0).start()

    prev = se_ref[jnp.maximum(m - 1, 0)]

    @pl.when((m == 0) | (se_ref[m] != prev))
    def _():
        tile_copy(se_ref[m], j).wait()
        wb_ref[...] = wst_ref[...].astype(BF16)
        nxt = nx_ref[m]

        @pl.when(nxt >= 0)
        def _():
            tile_copy(nxt, j).start()

        @pl.when((nxt < 0) & (j + 1 < nj))
        def _():
            tile_copy(se_ref[0], j + 1).start()


def _for_row_count(nsub, o_ref, compute):
    @pl.when(nsub == 0)
    def _():
        o_ref[...] = jnp.zeros(o_ref.shape, o_ref.dtype)

    for s in range(1, NSUB + 1):
        @pl.when(nsub == s)
        def _(s=s):
            rows = s * SUB
            compute(rows)
            if rows < o_ref.shape[0]:
                o_ref[rows:, :] = jnp.zeros((o_ref.shape[0] - rows, o_ref.shape[1]), o_ref.dtype)


def _expert_a_kernel(se_ref, ns_ref, nx_ref, nu_ref, x_ref, w_hbm, b_ref, o_ref, wst_ref, wb_ref, sem):
    m = pl.program_id(1)
    tn = wb_ref.shape[1]
    nsub = ns_ref[m]

    @pl.when(nsub > 0)
    def _():
        _stream_weights(se_ref, nx_ref, m, w_hbm, wst_ref, wb_ref, sem)

    def compute(rows):
        r_i = lax.broadcasted_iota(jnp.int32, (MXU_DIM, LANES), 0)
        c_i = lax.broadcasted_iota(jnp.int32, (MXU_DIM, LANES), 1)
        sel = jnp.where(r_i == 2 * c_i, 1.0, 0.0).astype(BF16)
        lo, hi = _unpack_bf16_pair(x_ref[0:rows, :])
        x = jnp.concatenate([lo.astype(BF16), hi.astype(BF16)], axis=1)
        gu = jnp.dot(x, wb_ref[...], preferred_element_type=F32) + b_ref[0]
        for c in range(tn // MXU_DIM):
            parts = []
            for s in range(MXU_DIM // LANES):
                g = gu[:, c * MXU_DIM + s * LANES:c * MXU_DIM + (s + 1) * LANES]
                gate = jnp.minimum(g, SWIGLU_LIMIT)
                up1 = jnp.clip(g, -SWIGLU_LIMIT, SWIGLU_LIMIT) + 1.0
                act = gate * jax.nn.sigmoid(SWIGLU_ALPHA * gate)
                parts.append((act * pltpu.roll(up1, LANES - 1, 1)).astype(BF16))
            pair = jnp.concatenate(parts, axis=1)
            o_ref[0:rows, c * LANES:(c + 1) * LANES] = jnp.dot(
                pair, sel, preferred_element_type=F32).astype(BF16)

    _for_row_count(nsub, o_ref, compute)


def _expert_b_kernel(se_ref, ns_ref, nx_ref, nu_ref, a_ref, w_hbm, b_ref, o_ref, wst_ref, wb_ref, sem):
    m = pl.program_id(1)
    nsub = ns_ref[m]

    @pl.when(nsub > 0)
    def _():
        _stream_weights(se_ref, nx_ref, m, w_hbm, wst_ref, wb_ref, sem)

    def compute(rows):
        y = jnp.dot(a_ref[0:rows, :], wb_ref[...], preferred_element_type=F32) + b_ref[0]
        half = y.shape[1] // 2
        o_ref[0:rows, :] = _pack_bf16_pair(y[:, :half], y[:, half:])

    _for_row_count(nsub, o_ref, compute)


def _expert_call(kernel, name, sb_e, sb_nsub, sb_next_e, n_used, xin, w, b3, out_cols, out_dtype, out_tn):
    n_sb = sb_e.shape[0]
    tm, tn = MOE_TM, MOE_TN
    kdim = w.shape[1]
    nj = w.shape[2] // tn

    def blk(m, nu):
        return jnp.maximum(jnp.minimum(m, nu[0] - 1), 0)

    grid_spec = pltpu.PrefetchScalarGridSpec(
        num_scalar_prefetch=4,
        grid=(nj, n_sb),
        in_specs=[
            pl.BlockSpec((tm, xin.shape[1]), lambda j, m, se, ns, nx, nu: (blk(m, nu), 0)),
            pl.BlockSpec(memory_space=pl.ANY),
            pl.BlockSpec((1, 1, tn), lambda j, m, se, ns, nx, nu: (se[blk(m, nu)], 0, j)),
        ],
        out_specs=pl.BlockSpec((tm, out_tn), lambda j, m, se, ns, nx, nu: (m, j)),
        scratch_shapes=[pltpu.VMEM((kdim, tn), F32), pltpu.VMEM((kdim, tn), BF16), pltpu.SemaphoreType.DMA(())],
    )
    return pl.pallas_call(
        kernel,
        grid_spec=grid_spec,
        out_shape=jax.ShapeDtypeStruct((n_sb * tm, out_cols), out_dtype),
        compiler_params=_cparams(("arbitrary", "arbitrary")),
        name=name,
    )(sb_e, sb_nsub, sb_next_e, n_used, xin, w, b3)


def _combine_kernel(dest_ref, nxt_ref, y_hbm, x1_ref, route_ref, o_ref, buf2_ref, sem):
    i = pl.program_id(0)
    n = pl.num_programs(0)
    tm = o_ref.shape[0]

    def gather(idx_ref, slot):
        def issue(r, c):
            for kk in range(TOP_K):
                _row_copy(y_hbm, idx_ref[0, 0, r * TOP_K + kk], buf2_ref.at[slot, kk], r,
                          sem.at[slot]).start(priority=kk % 2)
            return c
        lax.fori_loop(0, tm, issue, 0, unroll=4)

    @pl.when(i == 0)
    def _():
        gather(dest_ref, 0)

    @pl.when(i + 1 < n)
    def _():
        gather(nxt_ref, (i + 1) % 2)

    slot = i % 2
    buf_ref = buf2_ref.at[slot]
    for kk in range(TOP_K):
        pltpu.make_async_copy(y_hbm.at[pl.ds(0, tm), :], buf_ref.at[kk], sem.at[slot]).wait()
    route = route_ref[...]
    hw = MOE_TN // 2
    for j in range(D_MODEL // MOE_TN):
        lo_acc = x1_ref[:, j * MOE_TN:j * MOE_TN + hw]
        hi_acc = x1_ref[:, j * MOE_TN + hw:(j + 1) * MOE_TN]
        for kk in range(TOP_K):
            gate = route[:, TOP_K + kk:TOP_K + kk + 1]
            lo, hi = _unpack_bf16_pair(buf_ref[kk, :, j * hw:(j + 1) * hw])
            lo_acc = lo_acc + gate * lo
            hi_acc = hi_acc + gate * hi
        o_ref[:, j * MOE_TN:j * MOE_TN + hw] = lo_acc
        o_ref[:, j * MOE_TN + hw:(j + 1) * MOE_TN] = hi_acc


def _combine(dest3, y, x1, route):
    T = x1.shape[0]
    tm = CMB_TM
    nblk = T // tm
    row = lambda i: (i, 0)
    return pl.pallas_call(
        _combine_kernel,
        grid=(nblk,),
        in_specs=[pl.BlockSpec((1, 1, tm * TOP_K), lambda i: (i, 0, 0), memory_space=pltpu.SMEM),
                  pl.BlockSpec((1, 1, tm * TOP_K), lambda i: (jnp.minimum(i + 1, nblk - 1), 0, 0),
                               memory_space=pltpu.SMEM),
                  pl.BlockSpec(memory_space=pl.ANY),
                  pl.BlockSpec((tm, D_MODEL), row),
                  pl.BlockSpec((tm, LANES), row)],
        out_specs=pl.BlockSpec((tm, D_MODEL), row),
        out_shape=jax.ShapeDtypeStruct((T, D_MODEL), F32),
        scratch_shapes=[pltpu.VMEM((2, TOP_K, tm, HALF_D), jnp.uint32), pltpu.SemaphoreType.DMA((2,))],
        compiler_params=_cparams(("arbitrary",)),
        name="combine",
    )(dest3, dest3, y, x1, route)


def _pad_lanes(v, width=LANES):
    return jnp.pad(v, [(0, 0)] * (v.ndim - 1) + [(0, width - v.shape[-1])])


def _swap_halves(v):
    half = v.shape[-1] // 2
    return jnp.concatenate([v[..., half:], v[..., :half]], axis=-1)


def _token_mixers(x2, B, S, attn_norm_w, w_in, q_a_norm_w, w_uq, kv_a_norm_w, w_ukv, mla_q_norm_w, mla_k_norm_w,
                  diff_q_norm_w, diff_k_norm_w, lambda_q1, lambda_k1, lambda_q2, lambda_k2, diff_subln_w, rel_bias):
    L = 0
    win = w_in[L]
    sp = np.cumsum([MLA_Q_RANK, MLA_KV_RANK, MLA_ROPE])
    w_kr = win[:, sp[1]:sp[2]]
    win_r = jnp.concatenate(
        [win[:, :sp[1]], _pad_lanes(w_kr), _pad_lanes(_swap_halves(w_kr)), win[:, sp[2]:]], axis=1).astype(BF16)
    qk = MLA_NOPE + MLA_ROPE
    wuq3 = w_uq[L].reshape(MLA_Q_RANK, MLA_HEADS, qk)
    wuq_rope = wuq3[:, :, MLA_NOPE:]
    wuq_r = jnp.concatenate([
        wuq3[:, :, :MLA_NOPE].reshape(MLA_Q_RANK, -1),
        _pad_lanes(wuq_rope).reshape(MLA_Q_RANK, -1),
        _pad_lanes(_swap_halves(wuq_rope)).reshape(MLA_Q_RANK, -1)], axis=1).astype(BF16)
    wukv_b = w_ukv[L].astype(BF16)
    qn_w, kn_w = mla_q_norm_w[L], mla_k_norm_w[L]
    vecs = jnp.stack([
        qn_w[:MLA_NOPE], _pad_lanes(qn_w[MLA_NOPE:]), _pad_lanes(_swap_halves(qn_w[MLA_NOPE:])),
        kn_w[:MLA_NOPE], _pad_lanes(kn_w[MLA_NOPE:]), _pad_lanes(_swap_halves(kn_w[MLA_NOPE:])),
        jnp.tile(diff_q_norm_w[L], 2), jnp.tile(diff_k_norm_w[L], 2)])
    lamv = _pad_lanes(jnp.stack([lambda_q1[L], lambda_k1[L], lambda_q2[L], lambda_k2[L]]))

    inv_freq = np.float32(ROPE_THETA) ** (-np.arange(0, MLA_ROPE, 2, dtype=np.float32) / np.float32(MLA_ROPE))
    ang = np.arange(S, dtype=np.float32)[:, None] * inv_freq[None, :].astype(np.float32)
    cos, sin = np.cos(ang).astype(np.float32), np.sin(ang).astype(np.float32)
    zpad = np.zeros((S, LANES - MLA_ROPE), np.float32)
    cos128 = jnp.asarray(np.concatenate([cos, cos, zpad], axis=1))
    sin128 = jnp.asarray(np.concatenate([-sin, sin, zpad], axis=1))

    qm, km, vm, qd, kd, vd = _prep(x2, attn_norm_w[L][None], win_r, q_a_norm_w[L][None], wuq_r,
                                   kv_a_norm_w[L][None], wukv_b, vecs, cos128, sin128, S)
    nbias = _relbias(rel_bias, ATT_TQ, ATT_TK)
    om = _attention(qm, km, vm, B, S, MLA_HEADS, 2 * LANES, diff=False)
    od = _attention(qd, kd, vd, B, S, DIFF_HEADS, LANES, diff=True,
                    extra=(nbias, lamv, diff_subln_w[L][None]))
    return om, od


def _moe_block(om, od, x2, w_o, ffn_norm_w, router_w, router_b, w_gate_up, b_gate_up, w_down, b_down):
    L = 0
    T, D = x2.shape
    wo_b = w_o[L].astype(BF16)
    rw_pad = _pad_lanes(router_w[L])
    rb_pad = _pad_lanes(router_b[L][None, :])

    x1, h, route, cnt = _router(om, od, x2, wo_b, ffn_norm_w[L][None], rw_pad, rb_pad)

    tm = MOE_TM
    n_sb = -(-T * TOP_K // tm) + N_EXPERTS
    idx = route[:, 0:TOP_K].astype(jnp.int32)
    rank = route[:, 2 * TOP_K:3 * TOP_K].astype(jnp.int32)
    counts = cnt[0, :N_EXPERTS].astype(jnp.int32)
    padded = (counts + tm - 1) // tm * tm
    pad_ends = jnp.cumsum(padded)
    pad_starts = pad_ends - padded
    dest = pad_starts[idx] + rank
    sb_start = jnp.arange(n_sb, dtype=jnp.int32) * tm
    sb_e = jnp.minimum(jnp.sum((pad_ends[None, :] <= sb_start[:, None]).astype(jnp.int32), axis=1), N_EXPERTS - 1)
    sb_valid = jnp.clip(pad_starts[sb_e] + counts[sb_e] - sb_start, 0, tm)
    sb_valid = jnp.where(sb_start < pad_ends[-1], sb_valid, 0)
    sb_nsub = (sb_valid + SUB - 1) // SUB
    n_used = (pad_ends[-1] // tm).astype(jnp.int32)[None]
    nxt_sb = pad_ends[sb_e] // tm
    sb_next_e = jnp.where(nxt_sb < n_used[0], sb_e[jnp.minimum(nxt_sb, n_sb - 1)], -1).astype(jnp.int32)

    xs = _dispatch(sb_valid.astype(jnp.int32), dest.reshape(T // DSP_TM, 1, DSP_TM * TOP_K), h, n_sb * tm)
    act = _expert_call(_expert_a_kernel, "expert_a", sb_e, sb_nsub, sb_next_e, n_used, xs, w_gate_up[L],
                       b_gate_up[L][:, None, :], D_FF, BF16, MOE_TN // 2)
    y = _expert_call(_expert_b_kernel, "expert_b", sb_e, sb_nsub, sb_next_e, n_used, act, w_down[L],
                     b_down[L][:, None, :], D // 2, jnp.uint32, MOE_TN // 2)
    return _combine(dest.reshape(T // CMB_TM, 1, CMB_TM * TOP_K), y, x1, route)


def kernel(x, attn_norm_w, w_in, q_a_norm_w, w_uq, kv_a_norm_w, w_ukv, mla_q_norm_w, mla_k_norm_w, diff_q_norm_w, diff_k_norm_w, lambda_q1, lambda_k1, lambda_q2, lambda_k2, diff_subln_w, w_o, ffn_norm_w, router_w, router_b, w_gate_up, b_gate_up, w_down, b_down, rel_bias):
    B, S, D = x.shape
    x2 = x.reshape(B * S, D)
    om, od = _token_mixers(x2, B, S, attn_norm_w, w_in, q_a_norm_w, w_uq, kv_a_norm_w, w_ukv, mla_q_norm_w,
                           mla_k_norm_w, diff_q_norm_w, diff_k_norm_w, lambda_q1, lambda_k1, lambda_q2, lambda_k2,
                           diff_subln_w, rel_bias)
    out = _moe_block(om, od, x2, w_o, ffn_norm_w, router_w, router_b, w_gate_up, b_gate_up, w_down, b_down)
    return out.reshape(B, S, D)
```

```python
import functools
import math

import jax
import jax.numpy as jnp
import numpy as np
from jax import lax
from jax.experimental import pallas as pl
from jax.experimental.pallas import tpu as pltpu

D_MODEL = 2048
CHUNK = 64
MLA_HEADS = 8
MLA_NOPE = 128
MLA_ROPE = 64
MLA_V = 128
MLA_Q_RANK = 384
MLA_KV_RANK = 256
ROPE_THETA = 10000.0
DIFF_HEADS = 8
DIFF_DH = 64
DIFF_V = 128
REL_BUCKETS = 32
REL_MAX_DIST = 128
N_EXPERTS = 32
TOP_K = 4
D_FF = 2048
SWIGLU_LIMIT = 7.0
SWIGLU_ALPHA = 1.702
EPS = 1e-6
LAMBDA_INIT = 0.8 - 0.6 * math.exp(-0.3 * 0)

LANES = 128
MXU_DIM = 256
VMEM_LIMIT = 48 * 1024 * 1024

PREP_TM = 256
ATT_TQ = 512
ATT_TK = 512
ATT_HP = 4
RT_TM = 512
SUB = 256
MOE_TM = 1024
NSUB = MOE_TM // SUB
MOE_TN = 1024
CMB_TM = 128
HALF_D = D_MODEL // 2
DSP_TM = 256

LOG2E = math.log2(math.e)
NEG = -1e30
SCORE_LIMIT = 80.0
SCORE_SLACK = 1.05
F32 = jnp.float32
BF16 = jnp.bfloat16

_C_CQ = 0
_C_CKV = MLA_Q_RANK
_C_KR = _C_CKV + MLA_KV_RANK
_C_KRR = _C_KR + LANES
_C_MLA_END = _C_KRR + LANES
_C_DQ = _C_MLA_END
_C_DK = _C_DQ + DIFF_HEADS * 2 * DIFF_DH
_C_DV = _C_DK + DIFF_HEADS * 2 * DIFF_DH
_C_END = _C_DV + DIFF_HEADS * DIFF_V


def _cparams(sem):
    return pltpu.CompilerParams(dimension_semantics=sem, vmem_limit_bytes=VMEM_LIMIT)


def _const_spec(shape):
    nd = len(shape)
    return pl.BlockSpec(shape, lambda *_: (0,) * nd, pipeline_mode=pl.Buffered(1))


def _pack_bf16_pair(lo, hi):
    lo_b = lax.bitcast_convert_type(lo.astype(BF16).astype(F32), jnp.uint32)
    hi_b = lax.bitcast_convert_type(hi.astype(BF16).astype(F32), jnp.uint32)
    return (lo_b >> 16) | (hi_b & jnp.uint32(0xFFFF0000))


def _unpack_bf16_pair(w):
    return (lax.bitcast_convert_type(w << 16, F32),
            lax.bitcast_convert_type(w & jnp.uint32(0xFFFF0000), F32))


def _prep_kernel(x_ref, anw_ref, win_ref, qaw_ref, wuq_ref, kvaw_ref, wukv_ref, vec_ref, cos_ref, sin_ref,
                 qm_ref, km_ref, vm_ref, qd_ref, kd_ref, vd_ref):
    xf = x_ref[...]
    inv = lax.rsqrt(jnp.mean(xf * xf, axis=-1, keepdims=True) + EPS)
    h = (xf * inv * anw_ref[...]).astype(BF16)

    def rms(v, w):
        return v * lax.rsqrt(jnp.mean(v * v, axis=-1, keepdims=True) + EPS) * w

    pm = jnp.dot(h, win_ref[:, _C_CQ:_C_MLA_END], preferred_element_type=F32)
    cqn = rms(pm[:, _C_CQ:_C_CKV], qaw_ref[...]).astype(BF16)
    ckn = rms(pm[:, _C_CKV:_C_KR], kvaw_ref[...]).astype(BF16)
    kr = pm[:, _C_KR:_C_KRR]
    krr = pm[:, _C_KRR:_C_MLA_END]
    qall = jnp.dot(cqn, wuq_ref[...], preferred_element_type=F32)
    kvall = jnp.dot(ckn, wukv_ref[...], preferred_element_type=F32)

    vec = vec_ref[...]
    wqn, wqr, wqrr = vec[0:1], vec[1:2], vec[2:3]
    wkn, wkr, wkrr = vec[3:4], vec[4:5], vec[5:6]
    wdq, wdk = vec[6:7], vec[7:8]
    cos = cos_ref[...]
    sin = sin_ref[...]
    cq_r, sq_r = wqr * cos, wqrr * sin
    ck_r, sk_r = wkr * cos, wkrr * sin
    kr2 = kr * kr
    k_rope_base = kr * ck_r + krr * sk_r
    q_scale = (MLA_NOPE + MLA_ROPE) ** -0.5 * LOG2E
    inv_qk = 1.0 / (MLA_NOPE + MLA_ROPE)
    nq = MLA_HEADS * LANES
    for hh in range(MLA_HEADS):
        a = hh * LANES
        qn = qall[:, a:a + LANES]
        qr = qall[:, nq + a:nq + a + LANES]
        qrr = qall[:, 2 * nq + a:2 * nq + a + LANES]
        iq = lax.rsqrt(jnp.sum(qn * qn + qr * qr, axis=-1, keepdims=True) * inv_qk + EPS) * q_scale
        qm_ref[:, 2 * a:2 * a + LANES] = (qn * iq * wqn).astype(BF16)
        qm_ref[:, 2 * a + LANES:2 * a + 2 * LANES] = ((qr * cq_r + qrr * sq_r) * iq).astype(BF16)
        kn = kvall[:, 2 * a:2 * a + LANES]
        ik = lax.rsqrt(jnp.sum(kn * kn + kr2, axis=-1, keepdims=True) * inv_qk + EPS)
        km_ref[:, 2 * a:2 * a + LANES] = (kn * ik * wkn).astype(BF16)
        km_ref[:, 2 * a + LANES:2 * a + 2 * LANES] = (k_rope_base * ik).astype(BF16)
        vm_ref[:, a:a + LANES] = kvall[:, 2 * a + LANES:2 * a + 2 * LANES].astype(BF16)

    dq = jnp.dot(h, win_ref[:, _C_DQ:_C_DK], preferred_element_type=F32)
    dk = jnp.dot(h, win_ref[:, _C_DK:_C_DV], preferred_element_type=F32)
    vd_ref[...] = jnp.dot(h, win_ref[:, _C_DV:_C_END], preferred_element_type=F32).astype(BF16)
    lane = lax.broadcasted_iota(jnp.int32, (xf.shape[0], LANES), 1)
    lo = lane < DIFF_DH
    d_scale = DIFF_DH ** -0.5 * LOG2E
    inv_dh = 1.0 / DIFF_DH

    def half_norm(v):
        sq = v * v
        s1 = jnp.sum(jnp.where(lo, sq, 0.0), axis=-1, keepdims=True)
        s2 = jnp.sum(jnp.where(lo, 0.0, sq), axis=-1, keepdims=True)
        return v * jnp.where(lo, lax.rsqrt(s1 * inv_dh + EPS), lax.rsqrt(s2 * inv_dh + EPS))

    for hh in range(DIFF_HEADS):
        a = hh * LANES
        qn = half_norm(dq[:, a:a + LANES]) * (wdq * d_scale)
        qd_ref[:, 2 * a:2 * a + LANES] = jnp.where(lo, qn, 0.0).astype(BF16)
        qd_ref[:, 2 * a + LANES:2 * a + 2 * LANES] = jnp.where(lo, 0.0, qn).astype(BF16)
        kd_ref[:, a:a + LANES] = (half_norm(dk[:, a:a + LANES]) * wdk).astype(BF16)


def _prep(x2, anw, win_r, qaw, wuq_r, kvaw, wukv_b, vecs, cos128, sin128, seq):
    T = x2.shape[0]
    tm = PREP_TM
    nseq = seq // tm
    row = lambda i: (i, 0)
    outs = [
        jax.ShapeDtypeStruct((T, MLA_HEADS * 2 * LANES), BF16),
        jax.ShapeDtypeStruct((T, MLA_HEADS * 2 * LANES), BF16),
        jax.ShapeDtypeStruct((T, MLA_HEADS * MLA_V), BF16),
        jax.ShapeDtypeStruct((T, DIFF_HEADS * 2 * LANES), BF16),
        jax.ShapeDtypeStruct((T, DIFF_HEADS * LANES), BF16),
        jax.ShapeDtypeStruct((T, DIFF_HEADS * DIFF_V), BF16),
    ]
    return pl.pallas_call(
        _prep_kernel,
        grid=(T // tm,),
        in_specs=[
            pl.BlockSpec((tm, D_MODEL), row),
            _const_spec(anw.shape), _const_spec(win_r.shape), _const_spec(qaw.shape), _const_spec(wuq_r.shape),
            _const_spec(kvaw.shape), _const_spec(wukv_b.shape), _const_spec(vecs.shape),
            pl.BlockSpec((tm, LANES), lambda i: (i % nseq, 0)),
            pl.BlockSpec((tm, LANES), lambda i: (i % nseq, 0)),
        ],
        out_specs=[pl.BlockSpec((tm, o.shape[1]), row) for o in outs],
        out_shape=outs,
        compiler_params=_cparams(("parallel",)),
        name="prep",
    )(x2, anw, win_r, qaw, wuq_r, kvaw, wukv_b, vecs, cos128, sin128)


def _relbias_kernel(rb_ref, o_ref, *, tq, tk):
    hh = pl.program_id(0)
    which = pl.program_id(1)
    row = lax.broadcasted_iota(jnp.int32, (tq, tk), 0)
    col = lax.broadcasted_iota(jnp.int32, (tq, tk), 1)
    rel = col - which * tk - row
    nb = REL_BUCKETS // 2
    max_exact = nb // 2
    ret = jnp.where(rel > 0, nb, 0)
    n = jnp.abs(rel)
    nf = jnp.maximum(n, 1).astype(F32)
    large = max_exact + (jnp.log(nf / max_exact) / math.log(REL_MAX_DIST / max_exact)
                         * (nb - max_exact)).astype(jnp.int32)
    large = jnp.minimum(large, nb - 1)
    bucket = ret + jnp.where(n < max_exact, n, large)
    bias = jnp.zeros((tq, tk), F32)
    for b in range(REL_BUCKETS):
        bias = jnp.where(bucket == b, rb_ref[b, hh], bias)
    bias = (bias - rb_ref[nb - 1, hh]) * LOG2E
    allowed = ((col // CHUNK) <= (row // CHUNK)) | (which > 0)
    o_ref[0, 0] = jnp.where(allowed, bias, NEG)


def _relbias(rel_bias, tq, tk):
    return pl.pallas_call(
        functools.partial(_relbias_kernel, tq=tq, tk=tk),
        grid=(DIFF_HEADS, 2),
        in_specs=[pl.BlockSpec(memory_space=pltpu.SMEM)],
        out_specs=pl.BlockSpec((1, 1, tq, tk), lambda h, w: (h, w, 0, 0)),
        out_shape=jax.ShapeDtypeStruct((DIFF_HEADS, 2, tq, tk), F32),
        compiler_params=_cparams(("parallel", "parallel")),
        name="relbias",
    )(rel_bias)


def _attn_kernel(*refs, diff, tq, tk, bounded):
    if diff:
        q_ref, k_ref, v_ref, nb_ref, lam_ref, sub_ref, o_ref, m_ref, acc_ref = refs
    else:
        q_ref, k_ref, v_ref, o_ref, m_ref, acc_ref = refs
    i = pl.program_id(2)
    hp = o_ref.shape[1] // LANES
    dk = k_ref.shape[1] // hp
    qs = []
    for a in range(hp):
        blk = q_ref[:, a * 2 * LANES:(a + 1) * 2 * LANES]
        qs.append(jnp.concatenate([blk[:, :LANES], blk[:, LANES:]], axis=0) if diff else blk)
    m_ref[...] = jnp.full(m_ref.shape, NEG, F32)
    acc_ref[...] = jnp.zeros(acc_ref.shape, F32)
    ones = jnp.ones((tk, LANES), BF16)

    def step(j, biases):
        start = pl.multiple_of(j * tk, tk)
        for a in range(hp):
            ks = k_ref[pl.ds(start, tk), a * dk:(a + 1) * dk]
            vs = jnp.concatenate([v_ref[pl.ds(start, tk), a * LANES:(a + 1) * LANES], ones], axis=1)
            s = lax.dot_general(qs[a], ks, (((1,), (1,)), ((), ())), preferred_element_type=F32)
            if biases is not None:
                s = s + biases[a]
            if bounded:
                acc_ref[a] += jnp.dot(jnp.exp2(s.astype(BF16)), vs, preferred_element_type=F32)
                continue
            m_prev = m_ref[a]
            m_new = jnp.maximum(m_prev, jnp.max(s, axis=1, keepdims=True))
            alpha = jnp.exp2(m_prev - m_new)
            p = jnp.exp2((s - jnp.concatenate([m_new] * (tk // LANES), axis=1)).astype(BF16))
            acc_ref[a] = (jnp.concatenate([alpha, alpha], axis=1) * acc_ref[a]
                          + jnp.dot(p, vs, preferred_element_type=F32))
            m_ref[a] = m_new

    def far_tiles(n):
        def pair(j2, c):
            step(2 * j2, None)
            step(2 * j2 + 1, None)
            return c

        lax.fori_loop(0, n // 2, pair, 0)

        @pl.when(n % 2 == 1)
        def _():
            step(n - 1, None)

    far_tiles(jnp.maximum(i - 1, 0))

    def last_tiles(with_prev):
        if diff:
            if with_prev:
                step(i - 1, [jnp.concatenate([nb_ref[a, 1]] * 2, axis=0) for a in range(hp)])
            step(i, [jnp.concatenate([nb_ref[a, 0]] * 2, axis=0) for a in range(hp)])
        else:
            if with_prev:
                step(i - 1, None)
            row = lax.broadcasted_iota(jnp.int32, (tq, tk), 0)
            col = lax.broadcasted_iota(jnp.int32, (tq, tk), 1)
            step(i, [jnp.where((col // CHUNK) <= (row // CHUNK), 0.0, NEG)] * hp)

    @pl.when(i > 0)
    def _():
        last_tiles(True)

    @pl.when(i == 0)
    def _():
        last_tiles(False)

    for a in range(hp):
        o = acc_ref[a, :, 0:LANES] / acc_ref[a, :, LANES:2 * LANES]
        if diff:
            lv = lam_ref[...]
            lam = (jnp.exp(jnp.sum(lv[0:1] * lv[1:2], axis=1, keepdims=True))
                   - jnp.exp(jnp.sum(lv[2:3] * lv[3:4], axis=1, keepdims=True)) + LAMBDA_INIT)
            od = o[:tq] - lam * o[tq:]
            od = od * lax.rsqrt(jnp.mean(od * od, axis=-1, keepdims=True) + EPS) * sub_ref[...]
            o_ref[:, a * LANES:(a + 1) * LANES] = (od * (1.0 - LAMBDA_INIT)).astype(BF16)
        else:
            o_ref[:, a * LANES:(a + 1) * LANES] = o.astype(BF16)


def _attention(q, k, v, batch, seq, heads, dk, diff, bounded, extra=()):
    tq, tk, hp = ATT_TQ, ATT_TK, ATT_HP
    nq = seq // tq
    rows = 2 * tq if diff else tq
    in_specs = [
        pl.BlockSpec((tq, hp * 2 * LANES), lambda b, h, i: (b * nq + i, h)),
        pl.BlockSpec((seq, hp * dk), lambda b, h, i: (b, h), pipeline_mode=pl.Buffered(1)),
        pl.BlockSpec((seq, hp * LANES), lambda b, h, i: (b, h), pipeline_mode=pl.Buffered(1)),
    ]
    if diff:
        nbias, lamv, subw = extra
        in_specs += [
            pl.BlockSpec((hp, 2, tq, tk), lambda b, h, i: (h, 0, 0, 0), pipeline_mode=pl.Buffered(1)),
            pl.BlockSpec(lamv.shape, lambda b, h, i: (0, 0)),
            pl.BlockSpec(subw.shape, lambda b, h, i: (0, 0)),
        ]
    return pl.pallas_call(
        functools.partial(_attn_kernel, diff=diff, tq=tq, tk=tk, bounded=bounded),
        grid=(batch, heads // hp, nq),
        in_specs=in_specs,
        out_specs=pl.BlockSpec((tq, hp * LANES), lambda b, h, i: (b * nq + i, h)),
        out_shape=jax.ShapeDtypeStruct((batch * seq, heads * LANES), BF16),
        scratch_shapes=[pltpu.VMEM((hp, rows, LANES), F32),
                        pltpu.VMEM((hp, rows, 2 * LANES), F32)],
        compiler_params=_cparams(("parallel", "parallel", "arbitrary")),
        name=("attn_diff" if diff else "attn_mla") + ("_bounded" if bounded else ""),
    )(q, k, v, *extra)


def _router_kernel(om_ref, od_ref, x_ref, wo_ref, fw_ref, rw_ref, rb_ref,
                   x1_ref, h_ref, route_ref, cnt_ref, carry_ref):
    tm = x_ref.shape[0]
    half = om_ref.shape[1]

    @pl.when(pl.program_id(0) == 0)
    def _():
        carry_ref[...] = jnp.zeros(carry_ref.shape, F32)

    y = (jnp.dot(om_ref[...], wo_ref[0:half, :], preferred_element_type=F32)
         + jnp.dot(od_ref[...], wo_ref[half:2 * half, :], preferred_element_type=F32))
    x1 = x_ref[...] + y
    x1_ref[...] = x1
    hh = x1 * lax.rsqrt(jnp.mean(x1 * x1, axis=-1, keepdims=True) + EPS) * fw_ref[...]
    h_ref[...] = _pack_bf16_pair(hh[:, :HALF_D], hh[:, HALF_D:])
    rw = rw_ref[...]
    rw_hi = rw.astype(BF16)
    rw_lo = (rw - rw_hi.astype(F32)).astype(BF16)
    hh_hi = hh.astype(BF16)
    hh_lo = (hh - hh_hi.astype(F32)).astype(BF16)
    logits = (jnp.dot(hh_hi, rw_hi, preferred_element_type=F32) + jnp.dot(hh_lo, rw_hi, preferred_element_type=F32)
              + jnp.dot(hh_hi, rw_lo, preferred_element_type=F32)) + rb_ref[...]
    lane = lax.broadcasted_iota(jnp.int32, (tm, LANES), 1).astype(F32)
    work = jnp.where(lane < N_EXPERTS, logits, -jnp.inf)
    vals, idxs = [], []
    for _ in range(TOP_K):
        mx = jnp.max(work, axis=-1, keepdims=True)
        ix = jnp.min(jnp.where(work == mx, lane, float(LANES)), axis=-1, keepdims=True)
        vals.append(mx)
        idxs.append(ix)
        work = jnp.where(lane == ix, -jnp.inf, work)
    es = [jnp.exp(v - vals[0]) for v in vals]
    den = es[0] + es[1] + es[2] + es[3]
    onehot = jnp.zeros((tm, LANES), F32)
    for ix in idxs:
        onehot = onehot + jnp.where(lane == ix, 1.0, 0.0)
    r_i = lax.broadcasted_iota(jnp.int32, (tm, tm), 0)
    c_i = lax.broadcasted_iota(jnp.int32, (tm, tm), 1)
    tri = jnp.where(c_i < r_i, 1.0, 0.0).astype(BF16)
    prefix = jnp.dot(tri, onehot.astype(BF16), preferred_element_type=F32) + carry_ref[...]
    route = jnp.zeros((tm, LANES), F32)
    for kk in range(TOP_K):
        rank = jnp.sum(jnp.where(lane == idxs[kk], prefix, 0.0), axis=-1, keepdims=True)
        route = jnp.where(lane == kk, idxs[kk], route)
        route = jnp.where(lane == TOP_K + kk, es[kk] / den, route)
        route = jnp.where(lane == 2 * TOP_K + kk, rank, route)
    route_ref[...] = route
    carry = carry_ref[...] + jnp.sum(onehot, axis=0, keepdims=True)
    carry_ref[...] = carry
    cnt_ref[...] = carry


def _router(om, od, x2, wo_b, fw, rw_pad, rb_pad):
    T = x2.shape[0]
    tm = RT_TM
    row = lambda i: (i, 0)
    outs = [
        jax.ShapeDtypeStruct((T, D_MODEL), F32),
        jax.ShapeDtypeStruct((T, HALF_D), jnp.uint32),
        jax.ShapeDtypeStruct((T, LANES), F32),
        jax.ShapeDtypeStruct((1, LANES), F32),
    ]
    return pl.pallas_call(
        _router_kernel,
        grid=(T // tm,),
        in_specs=[
            pl.BlockSpec((tm, om.shape[1]), row), pl.BlockSpec((tm, od.shape[1]), row),
            pl.BlockSpec((tm, D_MODEL), row),
            _const_spec(wo_b.shape), _const_spec(fw.shape), _const_spec(rw_pad.shape), _const_spec(rb_pad.shape),
        ],
        out_specs=[pl.BlockSpec((tm, D_MODEL), row), pl.BlockSpec((tm, HALF_D), row),
                   pl.BlockSpec((tm, LANES), row), pl.BlockSpec((1, LANES), lambda i: (0, 0))],
        out_shape=outs,
        scratch_shapes=[pltpu.VMEM((1, LANES), F32)],
        compiler_params=_cparams(("arbitrary",)),
        name="router",
    )(om, od, x2, wo_b, fw, rw_pad, rb_pad)


def _row_copy(src_ref, src_row, dst_ref, dst_row, sem):
    return pltpu.make_async_copy(src_ref.at[pl.ds(src_row, 1), :], dst_ref.at[pl.ds(dst_row, 1), :], sem)


def _dispatch_kernel(sv_ref, dest_ref, h_ref, xs_hbm, zero_ref, sem):
    tm = h_ref.shape[0]

    @pl.when(pl.program_id(0) == 0)
    def _():
        zero_ref[...] = jnp.zeros(zero_ref.shape, zero_ref.dtype)

        def zero_copy(sb):
            return pltpu.make_async_copy(zero_ref, xs_hbm.at[pl.ds(sb * MOE_TM, MOE_TM), :], sem)

        def start(sb, c):
            @pl.when(sv_ref[sb] < MOE_TM)
            def _():
                zero_copy(sb).start()
            return c

        def wait(sb, c):
            @pl.when(sv_ref[sb] < MOE_TM)
            def _():
                zero_copy(sb).wait()
            return c

        lax.fori_loop(0, sv_ref.shape[0], start, 0)
        lax.fori_loop(0, sv_ref.shape[0], wait, 0)

    def issue(r, c):
        for kk in range(TOP_K):
            _row_copy(h_ref, r, xs_hbm, dest_ref[0, 0, r * TOP_K + kk], sem).start(priority=kk % 2)
        return c

    lax.fori_loop(0, tm, issue, 0, unroll=4)
    for kk in range(TOP_K):
        pltpu.make_async_copy(h_ref, xs_hbm.at[pl.ds(0, tm), :], sem).wait()


def _dispatch(sb_valid, dest3, h, n_slots):
    T = h.shape[0]
    tm = DSP_TM
    grid_spec = pltpu.PrefetchScalarGridSpec(
        num_scalar_prefetch=1,
        grid=(T // tm,),
        in_specs=[pl.BlockSpec((1, 1, tm * TOP_K), lambda i, sv: (i, 0, 0), memory_space=pltpu.SMEM),
                  pl.BlockSpec((tm, HALF_D), lambda i, sv: (i, 0))],
        out_specs=pl.BlockSpec(memory_space=pl.ANY),
        scratch_shapes=[pltpu.VMEM((MOE_TM, HALF_D), jnp.uint32), pltpu.SemaphoreType.DMA(())],
    )
    return pl.pallas_call(
        _dispatch_kernel,
        grid_spec=grid_spec,
        out_shape=jax.ShapeDtypeStruct((n_slots, HALF_D), jnp.uint32),
        compiler_params=_cparams(("arbitrary",)),
        name="dispatch",
    )(sb_valid, dest3, h)


def _stream_weights(se_ref, nx_ref, m, w_hbm, wst_ref, wb_ref, sem):
    j = pl.program_id(0)
    nj = pl.num_programs(0)
    tn = wb_ref.shape[1]

    def tile_copy(e, jj):
        return pltpu.make_async_copy(w_hbm.at[e, :, pl.ds(pl.multiple_of(jj * tn, tn), tn)], wst_ref, sem)

    @pl.when((j == 0) & (m == 0))
    def _():
        tile_copy(se_ref[0], 0).start()

    prev = se_ref[jnp.maximum(m - 1, 0)]

    @pl.when((m == 0) | (se_ref[m] != prev))
    def _():
        tile_copy(se_ref[m], j).wait()
        wb_ref[...] = wst_ref[...].astype(BF16)
        nxt = nx_ref[m]

        @pl.when(nxt >= 0)
        def _():
            tile_copy(nxt, j).start()

        @pl.when((nxt < 0) & (j + 1 < nj))
        def _():
            tile_copy(se_ref[0], j + 1).start()


def _for_row_count(nsub, o_ref, compute):
    @pl.when(nsub == 0)
    def _():
        o_ref[...] = jnp.zeros(o_ref.shape, o_ref.dtype)

    for s in range(1, NSUB + 1):
        @pl.when(nsub == s)
        def _(s=s):
            rows = s * SUB
            compute(rows)
            if rows < o_ref.shape[0]:
                o_ref[rows:, :] = jnp.zeros((o_ref.shape[0] - rows, o_ref.shape[1]), o_ref.dtype)


def _expert_a_kernel(se_ref, ns_ref, nx_ref, nu_ref, x_ref, w_hbm, b_ref, o_ref, wst_ref, wb_ref, sem):
    m = pl.program_id(1)
    tn = wb_ref.shape[1]
    nsub = ns_ref[m]

    @pl.when(nsub > 0)
    def _():
        _stream_weights(se_ref, nx_ref, m, w_hbm, wst_ref, wb_ref, sem)

    def compute(rows):
        r_i = lax.broadcasted_iota(jnp.int32, (MXU_DIM, LANES), 0)
        c_i = lax.broadcasted_iota(jnp.int32, (MXU_DIM, LANES), 1)
        sel = jnp.where(r_i == 2 * c_i, 1.0, 0.0).astype(BF16)
        lo, hi = _unpack_bf16_pair(x_ref[0:rows, :])
        x = jnp.concatenate([lo.astype(BF16), hi.astype(BF16)], axis=1)
        gu = jnp.dot(x, wb_ref[...], preferred_element_type=F32) + b_ref[0]
        for c in range(tn // MXU_DIM):
            parts = []
            for s in range(MXU_DIM // LANES):
                g = gu[:, c * MXU_DIM + s * LANES:c * MXU_DIM + (s + 1) * LANES]
                gate = jnp.minimum(g, SWIGLU_LIMIT)
                up1 = jnp.clip(g, -SWIGLU_LIMIT, SWIGLU_LIMIT) + 1.0
                act = gate * jax.nn.sigmoid(SWIGLU_ALPHA * gate)
                parts.append((act * pltpu.roll(up1, LANES - 1, 1)).astype(BF16))
            pair = jnp.concatenate(parts, axis=1)
            o_ref[0:rows, c * LANES:(c + 1) * LANES] = jnp.dot(
                pair, sel, preferred_element_type=F32).astype(BF16)

    _for_row_count(nsub, o_ref, compute)


def _expert_b_kernel(se_ref, ns_ref, nx_ref, nu_ref, a_ref, w_hbm, b_ref, o_ref, wst_ref, wb_ref, sem):
    m = pl.program_id(1)
    nsub = ns_ref[m]

    @pl.when(nsub > 0)
    def _():
        _stream_weights(se_ref, nx_ref, m, w_hbm, wst_ref, wb_ref, sem)

    def compute(rows):
        y = jnp.dot(a_ref[0:rows, :], wb_ref[...], preferred_element_type=F32) + b_ref[0]
        half = y.shape[1] // 2
        o_ref[0:rows, :] = _pack_bf16_pair(y[:, :half], y[:, half:])

    _for_row_count(nsub, o_ref, compute)


def _expert_call(kernel, name, sb_e, sb_nsub, sb_next_e, n_used, xin, w, b3, out_cols, out_dtype, out_tn):
    n_sb = sb_e.shape[0]
    tm, tn = MOE_TM, MOE_TN
    kdim = w.shape[1]
    nj = w.shape[2] // tn

    def blk(m, nu):
        return jnp.maximum(jnp.minimum(m, nu[0] - 1), 0)

    grid_spec = pltpu.PrefetchScalarGridSpec(
        num_scalar_prefetch=4,
        grid=(nj, n_sb),
        in_specs=[
            pl.BlockSpec((tm, xin.shape[1]), lambda j, m, se, ns, nx, nu: (blk(m, nu), 0)),
            pl.BlockSpec(memory_space=pl.ANY),
            pl.BlockSpec((1, 1, tn), lambda j, m, se, ns, nx, nu: (se[blk(m, nu)], 0, j)),
        ],
        out_specs=pl.BlockSpec((tm, out_tn), lambda j, m, se, ns, nx, nu: (m, j)),
        scratch_shapes=[pltpu.VMEM((kdim, tn), F32), pltpu.VMEM((kdim, tn), BF16), pltpu.SemaphoreType.DMA(())],
    )
    return pl.pallas_call(
        kernel,
        grid_spec=grid_spec,
        out_shape=jax.ShapeDtypeStruct((n_sb * tm, out_cols), out_dtype),
        compiler_params=_cparams(("arbitrary", "arbitrary")),
        name=name,
    )(sb_e, sb_nsub, sb_next_e, n_used, xin, w, b3)


def _combine_kernel(dest_ref, nxt_ref, y_hbm, x1_ref, route_ref, o_ref, buf2_ref, sem):
    i = pl.program_id(0)
    n = pl.num_programs(0)
    tm = o_ref.shape[0]

    def gather(idx_ref, slot):
        def issue(r, c):
            for kk in range(TOP_K):
                _row_copy(y_hbm, idx_ref[0, 0, r * TOP_K + kk], buf2_ref.at[slot, kk], r,
                          sem.at[slot]).start(priority=kk % 2)
            return c
        lax.fori_loop(0, tm, issue, 0, unroll=4)

    @pl.when(i == 0)
    def _():
        gather(dest_ref, 0)

    @pl.when(i + 1 < n)
    def _():
        gather(nxt_ref, (i + 1) % 2)

    slot = i % 2
    buf_ref = buf2_ref.at[slot]
    for kk in range(TOP_K):
        pltpu.make_async_copy(y_hbm.at[pl.ds(0, tm), :], buf_ref.at[kk], sem.at[slot]).wait()
    route = route_ref[...]
    hw = MOE_TN // 2
    for j in range(D_MODEL // MOE_TN):
        lo_acc = x1_ref[:, j * MOE_TN:j * MOE_TN + hw]
        hi_acc = x1_ref[:, j * MOE_TN + hw:(j + 1) * MOE_TN]
        for kk in range(TOP_K):
            gate = route[:, TOP_K + kk:TOP_K + kk + 1]
            lo, hi = _unpack_bf16_pair(buf_ref[kk, :, j * hw:(j + 1) * hw])
            lo_acc = lo_acc + gate * lo
            hi_acc = hi_acc + gate * hi
        o_ref[:, j * MOE_TN:j * MOE_TN + hw] = lo_acc
        o_ref[:, j * MOE_TN + hw:(j + 1) * MOE_TN] = hi_acc


def _combine(dest3, y, x1, route):
    T = x1.shape[0]
    tm = CMB_TM
    nblk = T // tm
    row = lambda i: (i, 0)
    return pl.pallas_call(
        _combine_kernel,
        grid=(nblk,),
        in_specs=[pl.BlockSpec((1, 1, tm * TOP_K), lambda i: (i, 0, 0), memory_space=pltpu.SMEM),
                  pl.BlockSpec((1, 1, tm * TOP_K), lambda i: (jnp.minimum(i + 1, nblk - 1), 0, 0),
                               memory_space=pltpu.SMEM),
                  pl.BlockSpec(memory_space=pl.ANY),
                  pl.BlockSpec((tm, D_MODEL), row),
                  pl.BlockSpec((tm, LANES), row)],
        out_specs=pl.BlockSpec((tm, D_MODEL), row),
        out_shape=jax.ShapeDtypeStruct((T, D_MODEL), F32),
        scratch_shapes=[pltpu.VMEM((2, TOP_K, tm, HALF_D), jnp.uint32), pltpu.SemaphoreType.DMA((2,))],
        compiler_params=_cparams(("arbitrary",)),
        name="combine",
    )(dest3, dest3, y, x1, route)


def _pad_lanes(v, width=LANES):
    return jnp.pad(v, [(0, 0)] * (v.ndim - 1) + [(0, width - v.shape[-1])])


def _swap_halves(v):
    half = v.shape[-1] // 2
    return jnp.concatenate([v[..., half:], v[..., :half]], axis=-1)


def _token_mixers(x2, B, S, attn_norm_w, w_in, q_a_norm_w, w_uq, kv_a_norm_w, w_ukv, mla_q_norm_w, mla_k_norm_w,
                  diff_q_norm_w, diff_k_norm_w, lambda_q1, lambda_k1, lambda_q2, lambda_k2, diff_subln_w, rel_bias):
    L = 0
    win = w_in[L]
    sp = np.cumsum([MLA_Q_RANK, MLA_KV_RANK, MLA_ROPE])
    w_kr = win[:, sp[1]:sp[2]]
    win_r = jnp.concatenate(
        [win[:, :sp[1]], _pad_lanes(w_kr), _pad_lanes(_swap_halves(w_kr)), win[:, sp[2]:]], axis=1).astype(BF16)
    qk = MLA_NOPE + MLA_ROPE
    wuq3 = w_uq[L].reshape(MLA_Q_RANK, MLA_HEADS, qk)
    wuq_rope = wuq3[:, :, MLA_NOPE:]
    wuq_r = jnp.concatenate([
        wuq3[:, :, :MLA_NOPE].reshape(MLA_Q_RANK, -1),
        _pad_lanes(wuq_rope).reshape(MLA_Q_RANK, -1),
        _pad_lanes(_swap_halves(wuq_rope)).reshape(MLA_Q_RANK, -1)], axis=1).astype(BF16)
    wukv_b = w_ukv[L].astype(BF16)
    qn_w, kn_w = mla_q_norm_w[L], mla_k_norm_w[L]
    vecs = jnp.stack([
        qn_w[:MLA_NOPE], _pad_lanes(qn_w[MLA_NOPE:]), _pad_lanes(_swap_halves(qn_w[MLA_NOPE:])),
        kn_w[:MLA_NOPE], _pad_lanes(kn_w[MLA_NOPE:]), _pad_lanes(_swap_halves(kn_w[MLA_NOPE:])),
        jnp.tile(diff_q_norm_w[L], 2), jnp.tile(diff_k_norm_w[L], 2)])
    lamv = _pad_lanes(jnp.stack([lambda_q1[L], lambda_k1[L], lambda_q2[L], lambda_k2[L]]))

    inv_freq = np.float32(ROPE_THETA) ** (-np.arange(0, MLA_ROPE, 2, dtype=np.float32) / np.float32(MLA_ROPE))
    ang = np.arange(S, dtype=np.float32)[:, None] * inv_freq[None, :].astype(np.float32)
    cos, sin = np.cos(ang).astype(np.float32), np.sin(ang).astype(np.float32)
    zpad = np.zeros((S, LANES - MLA_ROPE), np.float32)
    cos128 = jnp.asarray(np.concatenate([cos, cos, zpad], axis=1))
    sin128 = jnp.asarray(np.concatenate([-sin, sin, zpad], axis=1))

    qm, km, vm, qd, kd, vd = _prep(x2, attn_norm_w[L][None], win_r, q_a_norm_w[L][None], wuq_r,
                                   kv_a_norm_w[L][None], wukv_b, vecs, cos128, sin128, S)
    nbias = _relbias(rel_bias, ATT_TQ, ATT_TK)

    amax = lambda v: jnp.max(jnp.abs(v))
    bound_mla = qk * qk ** -0.5 * LOG2E * amax(qn_w) * amax(kn_w)
    bound_diff = (DIFF_DH * DIFF_DH ** -0.5 * LOG2E * amax(diff_q_norm_w[L]) * amax(diff_k_norm_w[L])
                  + LOG2E * amax(rel_bias - rel_bias[REL_BUCKETS // 2 - 1][None, :]))
    bounded = jnp.maximum(bound_mla, bound_diff) * SCORE_SLACK <= SCORE_LIMIT

    def mixers(is_bounded):
        def run(qm, km, vm, qd, kd, vd, nbias, lamv, subw):
            om = _attention(qm, km, vm, B, S, MLA_HEADS, 2 * LANES, diff=False, bounded=is_bounded)
            od = _attention(qd, kd, vd, B, S, DIFF_HEADS, LANES, diff=True, bounded=is_bounded,
                            extra=(nbias, lamv, subw))
            return om, od
        return run

    return lax.cond(bounded, mixers(True), mixers(False),
                    qm, km, vm, qd, kd, vd, nbias, lamv, diff_subln_w[L][None])


def _moe_block(om, od, x2, w_o, ffn_norm_w, router_w, router_b, w_gate_up, b_gate_up, w_down, b_down):
    L = 0
    T, D = x2.shape
    wo_b = w_o[L].astype(BF16)
    rw_pad = _pad_lanes(router_w[L])
    rb_pad = _pad_lanes(router_b[L][None, :])

    x1, h, route, cnt = _router(om, od, x2, wo_b, ffn_norm_w[L][None], rw_pad, rb_pad)

    tm = MOE_TM
    n_sb = -(-T * TOP_K // tm) + N_EXPERTS
    idx = route[:, 0:TOP_K].astype(jnp.int32)
    rank = route[:, 2 * TOP_K:3 * TOP_K].astype(jnp.int32)
    counts = cnt[0, :N_EXPERTS].astype(jnp.int32)
    padded = (counts + tm - 1) // tm * tm
    pad_ends = jnp.cumsum(padded)
    pad_starts = pad_ends - padded
    dest = pad_starts[idx] + rank
    sb_start = jnp.arange(n_sb, dtype=jnp.int32) * tm
    sb_e = jnp.minimum(jnp.sum((pad_ends[None, :] <= sb_start[:, None]).astype(jnp.int32), axis=1), N_EXPERTS - 1)
    sb_valid = jnp.clip(pad_starts[sb_e] + counts[sb_e] - sb_start, 0, tm)
    sb_valid = jnp.where(sb_start < pad_ends[-1], sb_valid, 0)
    sb_nsub = (sb_valid + SUB - 1) // SUB
    n_used = (pad_ends[-1] // tm).astype(jnp.int32)[None]
    nxt_sb = pad_ends[sb_e] // tm
    sb_next_e = jnp.where(nxt_sb < n_used[0], sb_e[jnp.minimum(nxt_sb, n_sb - 1)], -1).astype(jnp.int32)

    xs = _dispatch(sb_valid.astype(jnp.int32), dest.reshape(T // DSP_TM, 1, DSP_TM * TOP_K), h, n_sb * tm)
    act = _expert_call(_expert_a_kernel, "expert_a", sb_e, sb_nsub, sb_next_e, n_used, xs, w_gate_up[L],
                       b_gate_up[L][:, None, :], D_FF, BF16, MOE_TN // 2)
    y = _expert_call(_expert_b_kernel, "expert_b", sb_e, sb_nsub, sb_next_e, n_used, act, w_down[L],
                     b_down[L][:, None, :], D // 2, jnp.uint32, MOE_TN // 2)
    return _combine(dest.reshape(T // CMB_TM, 1, CMB_TM * TOP_K), y, x1, route)


def kernel(x, attn_norm_w, w_in, q_a_norm_w, w_uq, kv_a_norm_w, w_ukv, mla_q_norm_w, mla_k_norm_w, diff_q_norm_w, diff_k_norm_w, lambda_q1, lambda_k1, lambda_q2, lambda_k2, diff_subln_w, w_o, ffn_norm_w, router_w, router_b, w_gate_up, b_gate_up, w_down, b_down, rel_bias):
    B, S, D = x.shape
    x2 = x.reshape(B * S, D)
    om, od = _token_mixers(x2, B, S, attn_norm_w, w_in, q_a_norm_w, w_uq, kv_a_norm_w, w_ukv, mla_q_norm_w,
                           mla_k_norm_w, diff_q_norm_w, diff_k_norm_w, lambda_q1, lambda_k1, lambda_q2, lambda_k2,
                           diff_subln_w, rel_bias)
    out = _moe_block(om, od, x2, w_o, ffn_norm_w, router_w, router_b, w_gate_up, b_gate_up, w_down, b_down)
    return out.reshape(B, S, D)
```

```python
import functools
import math

import jax
import jax.numpy as jnp
import numpy as np
from jax import lax
from jax.experimental import pallas as pl
from jax.experimental.pallas import tpu as pltpu

D_MODEL = 2048
CHUNK = 64
MLA_HEADS = 8
MLA_NOPE = 128
MLA_ROPE = 64
MLA_V = 128
MLA_Q_RANK = 384
MLA_KV_RANK = 256
ROPE_THETA = 10000.0
DIFF_HEADS = 8
DIFF_DH = 64
DIFF_V = 128
REL_BUCKETS = 32
REL_MAX_DIST = 128
N_EXPERTS = 32
TOP_K = 4
D_FF = 2048
SWIGLU_LIMIT = 7.0
SWIGLU_ALPHA = 1.702
EPS = 1e-6
LAMBDA_INIT = 0.8 - 0.6 * math.exp(-0.3 * 0)

LANES = 128
MXU_DIM = 256
VMEM_LIMIT = 48 * 1024 * 1024

PREP_TM = 256
ATT_TQ = 512
ATT_TK = 512
ATT_HP = 4
RT_TM = 512
SUB = 256
MOE_TM = 1024
NSUB = MOE_TM // SUB
MOE_TN = 1024
CMB_TM = 128
HALF_D = D_MODEL // 2
DSP_TM = 256

LOG2E = math.log2(math.e)
NEG = -1e30
SCORE_LIMIT = 80.0
SCORE_SLACK = 1.05
F32 = jnp.float32
BF16 = jnp.bfloat16

_C_CQ = 0
_C_CKV = MLA_Q_RANK
_C_KR = _C_CKV + MLA_KV_RANK
_C_KRR = _C_KR + LANES
_C_MLA_END = _C_KRR + LANES
_C_DQ = _C_MLA_END
_C_DK = _C_DQ + DIFF_HEADS * 2 * DIFF_DH
_C_DV = _C_DK + DIFF_HEADS * 2 * DIFF_DH
_C_END = _C_DV + DIFF_HEADS * DIFF_V


def _cparams(sem):
    return pltpu.CompilerParams(dimension_semantics=sem, vmem_limit_bytes=VMEM_LIMIT)


def _const_spec(shape):
    nd = len(shape)
    return pl.BlockSpec(shape, lambda *_: (0,) * nd, pipeline_mode=pl.Buffered(1))


def _pack_bf16_pair(lo, hi):
    lo_b = lax.bitcast_convert_type(lo.astype(BF16).astype(F32), jnp.uint32)
    hi_b = lax.bitcast_convert_type(hi.astype(BF16).astype(F32), jnp.uint32)
    return (lo_b >> 16) | (hi_b & jnp.uint32(0xFFFF0000))


def _unpack_bf16_pair(w):
    return (lax.bitcast_convert_type(w << 16, F32),
            lax.bitcast_convert_type(w & jnp.uint32(0xFFFF0000), F32))


def _prep_kernel(x_ref, anw_ref, win_ref, qaw_ref, wuq_ref, kvaw_ref, wukv_ref, vec_ref, cos_ref, sin_ref,
                 qm_ref, km_ref, vm_ref, qd_ref, kd_ref, vd_ref):
    xf = x_ref[...]
    inv = lax.rsqrt(jnp.mean(xf * xf, axis=-1, keepdims=True) + EPS)
    h = (xf * inv * anw_ref[...]).astype(BF16)

    def rms(v, w):
        return v * lax.rsqrt(jnp.mean(v * v, axis=-1, keepdims=True) + EPS) * w

    pm = jnp.dot(h, win_ref[:, _C_CQ:_C_MLA_END], preferred_element_type=F32)
    cqn = rms(pm[:, _C_CQ:_C_CKV], qaw_ref[...]).astype(BF16)
    ckn = rms(pm[:, _C_CKV:_C_KR], kvaw_ref[...]).astype(BF16)
    kr = pm[:, _C_KR:_C_KRR]
    krr = pm[:, _C_KRR:_C_MLA_END]
    qall = jnp.dot(cqn, wuq_ref[...], preferred_element_type=F32)
    kvall = jnp.dot(ckn, wukv_ref[...], preferred_element_type=F32)

    vec = vec_ref[...]
    wqn, wqr, wqrr = vec[0:1], vec[1:2], vec[2:3]
    wkn, wkr, wkrr = vec[3:4], vec[4:5], vec[5:6]
    wdq, wdk = vec[6:7], vec[7:8]
    cos = cos_ref[...]
    sin = sin_ref[...]
    cq_r, sq_r = wqr * cos, wqrr * sin
    ck_r, sk_r = wkr * cos, wkrr * sin
    kr2 = kr * kr
    k_rope_base = kr * ck_r + krr * sk_r
    q_scale = (MLA_NOPE + MLA_ROPE) ** -0.5 * LOG2E
    inv_qk = 1.0 / (MLA_NOPE + MLA_ROPE)
    nq = MLA_HEADS * LANES
    for hh in range(MLA_HEADS):
        a = hh * LANES
        qn = qall[:, a:a + LANES]
        qr = qall[:, nq + a:nq + a + LANES]
        qrr = qall[:, 2 * nq + a:2 * nq + a + LANES]
        iq = lax.rsqrt(jnp.sum(qn * qn + qr * qr, axis=-1, keepdims=True) * inv_qk + EPS) * q_scale
        qm_ref[:, 2 * a:2 * a + LANES] = (qn * iq * wqn).astype(BF16)
        qm_ref[:, 2 * a + LANES:2 * a + 2 * LANES] = ((qr * cq_r + qrr * sq_r) * iq).astype(BF16)
        kn = kvall[:, 2 * a:2 * a + LANES]
        ik = lax.rsqrt(jnp.sum(kn * kn + kr2, axis=-1, keepdims=True) * inv_qk + EPS)
        km_ref[:, 2 * a:2 * a + LANES] = (kn * ik * wkn).astype(BF16)
        km_ref[:, 2 * a + LANES:2 * a + 2 * LANES] = (k_rope_base * ik).astype(BF16)
        vm_ref[:, a:a + LANES] = kvall[:, 2 * a + LANES:2 * a + 2 * LANES].astype(BF16)

    dq = jnp.dot(h, win_ref[:, _C_DQ:_C_DK], preferred_element_type=F32)
    dk = jnp.dot(h, win_ref[:, _C_DK:_C_DV], preferred_element_type=F32)
    vd_ref[...] = jnp.dot(h, win_ref[:, _C_DV:_C_END], preferred_element_type=F32).astype(BF16)
    lane = lax.broadcasted_iota(jnp.int32, (xf.shape[0], LANES), 1)
    lo = lane < DIFF_DH
    d_scale = DIFF_DH ** -0.5 * LOG2E
    inv_dh = 1.0 / DIFF_DH

    def half_norm(v):
        sq = v * v
        s1 = jnp.sum(jnp.where(lo, sq, 0.0), axis=-1, keepdims=True)
        s2 = jnp.sum(jnp.where(lo, 0.0, sq), axis=-1, keepdims=True)
        return v * jnp.where(lo, lax.rsqrt(s1 * inv_dh + EPS), lax.rsqrt(s2 * inv_dh + EPS))

    for hh in range(DIFF_HEADS):
        a = hh * LANES
        qn = half_norm(dq[:, a:a + LANES]) * (wdq * d_scale)
        qd_ref[:, 2 * a:2 * a + LANES] = jnp.where(lo, qn, 0.0).astype(BF16)
        qd_ref[:, 2 * a + LANES:2 * a + 2 * LANES] = jnp.where(lo, 0.0, qn).astype(BF16)
        kd_ref[:, a:a + LANES] = (half_norm(dk[:, a:a + LANES]) * wdk).astype(BF16)


def _prep(x2, anw, win_r, qaw, wuq_r, kvaw, wukv_b, vecs, cos128, sin128, seq):
    T = x2.shape[0]
    tm = PREP_TM
    nseq = seq // tm
    row = lambda i: (i, 0)
    outs = [
        jax.ShapeDtypeStruct((T, MLA_HEADS * 2 * LANES), BF16),
        jax.ShapeDtypeStruct((T, MLA_HEADS * 2 * LANES), BF16),
        jax.ShapeDtypeStruct((T, MLA_HEADS * MLA_V), BF16),
        jax.ShapeDtypeStruct((T, DIFF_HEADS * 2 * LANES), BF16),
        jax.ShapeDtypeStruct((T, DIFF_HEADS * LANES), BF16),
        jax.ShapeDtypeStruct((T, DIFF_HEADS * DIFF_V), BF16),
    ]
    return pl.pallas_call(
        _prep_kernel,
        grid=(T // tm,),
        in_specs=[
            pl.BlockSpec((tm, D_MODEL), row),
            _const_spec(anw.shape), _const_spec(win_r.shape), _const_spec(qaw.shape), _const_spec(wuq_r.shape),
            _const_spec(kvaw.shape), _const_spec(wukv_b.shape), _const_spec(vecs.shape),
            pl.BlockSpec((tm, LANES), lambda i: (i % nseq, 0)),
            pl.BlockSpec((tm, LANES), lambda i: (i % nseq, 0)),
        ],
        out_specs=[pl.BlockSpec((tm, o.shape[1]), row) for o in outs],
        out_shape=outs,
        compiler_params=_cparams(("parallel",)),
        name="prep",
    )(x2, anw, win_r, qaw, wuq_r, kvaw, wukv_b, vecs, cos128, sin128)


def _relbias_kernel(rb_ref, o_ref, *, tq, tk):
    hh = pl.program_id(0)
    which = pl.program_id(1)
    row = lax.broadcasted_iota(jnp.int32, (tq, tk), 0)
    col = lax.broadcasted_iota(jnp.int32, (tq, tk), 1)
    rel = col - which * tk - row
    nb = REL_BUCKETS // 2
    max_exact = nb // 2
    ret = jnp.where(rel > 0, nb, 0)
    n = jnp.abs(rel)
    nf = jnp.maximum(n, 1).astype(F32)
    large = max_exact + (jnp.log(nf / max_exact) / math.log(REL_MAX_DIST / max_exact)
                         * (nb - max_exact)).astype(jnp.int32)
    large = jnp.minimum(large, nb - 1)
    bucket = ret + jnp.where(n < max_exact, n, large)
    bias = jnp.zeros((tq, tk), F32)
    for b in range(REL_BUCKETS):
        bias = jnp.where(bucket == b, rb_ref[b, hh], bias)
    bias = (bias - rb_ref[nb - 1, hh]) * LOG2E
    allowed = ((col // CHUNK) <= (row // CHUNK)) | (which > 0)
    o_ref[0, 0] = jnp.where(allowed, bias, NEG)


def _relbias(rel_bias, tq, tk):
    return pl.pallas_call(
        functools.partial(_relbias_kernel, tq=tq, tk=tk),
        grid=(DIFF_HEADS, 2),
        in_specs=[pl.BlockSpec(memory_space=pltpu.SMEM)],
        out_specs=pl.BlockSpec((1, 1, tq, tk), lambda h, w: (h, w, 0, 0)),
        out_shape=jax.ShapeDtypeStruct((DIFF_HEADS, 2, tq, tk), F32),
        compiler_params=_cparams(("parallel", "parallel")),
        name="relbias",
    )(rel_bias)


def _attn_kernel(*refs, diff, tq, tk, bounded):
    if diff:
        q_ref, k_ref, v_ref, nb_ref, lam_ref, sub_ref, o_ref, m_ref, acc_ref = refs
    else:
        q_ref, k_ref, v_ref, o_ref, m_ref, acc_ref = refs
    i = pl.program_id(2)
    hp = o_ref.shape[1] // LANES
    dk = k_ref.shape[1] // hp
    qs = []
    for a in range(hp):
        blk = q_ref[:, a * 2 * LANES:(a + 1) * 2 * LANES]
        qs.append(jnp.concatenate([blk[:, :LANES], blk[:, LANES:]], axis=0) if diff else blk)
    m_ref[...] = jnp.full(m_ref.shape, NEG, F32)
    acc_ref[...] = jnp.zeros(acc_ref.shape, F32)
    ones = jnp.ones((tk, LANES), BF16)

    def step(j, biases):
        start = pl.multiple_of(j * tk, tk)
        for a in range(hp):
            ks = k_ref[pl.ds(start, tk), a * dk:(a + 1) * dk]
            vs = jnp.concatenate([v_ref[pl.ds(start, tk), a * LANES:(a + 1) * LANES], ones], axis=1)
            s = lax.dot_general(qs[a], ks, (((1,), (1,)), ((), ())), preferred_element_type=F32)
            if biases is not None:
                s = s + biases[a]
            if bounded:
                acc_ref[a] += jnp.dot(jnp.exp2(s.astype(BF16)), vs, preferred_element_type=F32)
                continue
            m_prev = m_ref[a]
            m_new = jnp.maximum(m_prev, jnp.max(s, axis=1, keepdims=True))
            alpha = jnp.exp2(m_prev - m_new)
            p = jnp.exp2((s - jnp.concatenate([m_new] * (tk // LANES), axis=1)).astype(BF16))
            acc_ref[a] = (jnp.concatenate([alpha, alpha], axis=1) * acc_ref[a]
                          + jnp.dot(p, vs, preferred_element_type=F32))
            m_ref[a] = m_new

    def far_tiles(n):
        def pair(j2, c):
            step(2 * j2, None)
            step(2 * j2 + 1, None)
            return c

        lax.fori_loop(0, n // 2, pair, 0)

        @pl.when(n % 2 == 1)
        def _():
            step(n - 1, None)

    far_tiles(jnp.maximum(i - 1, 0))

    def last_tiles(with_prev):
        if diff:
            if with_prev:
                step(i - 1, [jnp.concatenate([nb_ref[a, 1]] * 2, axis=0) for a in range(hp)])
            step(i, [jnp.concatenate([nb_ref[a, 0]] * 2, axis=0) for a in range(hp)])
        else:
            if with_prev:
                step(i - 1, None)
            row = lax.broadcasted_iota(jnp.int32, (tq, tk), 0)
            col = lax.broadcasted_iota(jnp.int32, (tq, tk), 1)
            step(i, [jnp.where((col // CHUNK) <= (row // CHUNK), 0.0, NEG)] * hp)

    @pl.when(i > 0)
    def _():
        last_tiles(True)

    @pl.when(i == 0)
    def _():
        last_tiles(False)

    for a in range(hp):
        o = acc_ref[a, :, 0:LANES] / acc_ref[a, :, LANES:2 * LANES]
        if diff:
            lv = lam_ref[...]
            lam = (jnp.exp(jnp.sum(lv[0:1] * lv[1:2], axis=1, keepdims=True))
                   - jnp.exp(jnp.sum(lv[2:3] * lv[3:4], axis=1, keepdims=True)) + LAMBDA_INIT)
            od = o[:tq] - lam * o[tq:]
            od = od * lax.rsqrt(jnp.mean(od * od, axis=-1, keepdims=True) + EPS) * sub_ref[...]
            o_ref[:, a * LANES:(a + 1) * LANES] = (od * (1.0 - LAMBDA_INIT)).astype(BF16)
        else:
            o_ref[:, a * LANES:(a + 1) * LANES] = o.astype(BF16)


def _attention(q, k, v, batch, seq, heads, dk, diff, bounded, extra=()):
    tq, tk, hp = ATT_TQ, ATT_TK, ATT_HP
    assert tq == tk and tk >= REL_MAX_DIST and seq % tq == 0 and heads % hp == 0
    nq = seq // tq
    rows = 2 * tq if diff else tq
    in_specs = [
        pl.BlockSpec((tq, hp * 2 * LANES), lambda b, h, i: (b * nq + i, h)),
        pl.BlockSpec((seq, hp * dk), lambda b, h, i: (b, h), pipeline_mode=pl.Buffered(1)),
        pl.BlockSpec((seq, hp * LANES), lambda b, h, i: (b, h), pipeline_mode=pl.Buffered(1)),
    ]
    if diff:
        nbias, lamv, subw = extra
        in_specs += [
            pl.BlockSpec((hp, 2, tq, tk), lambda b, h, i: (h, 0, 0, 0), pipeline_mode=pl.Buffered(1)),
            pl.BlockSpec(lamv.shape, lambda b, h, i: (0, 0)),
            pl.BlockSpec(subw.shape, lambda b, h, i: (0, 0)),
        ]
    return pl.pallas_call(
        functools.partial(_attn_kernel, diff=diff, tq=tq, tk=tk, bounded=bounded),
        grid=(batch, heads // hp, nq),
        in_specs=in_specs,
        out_specs=pl.BlockSpec((tq, hp * LANES), lambda b, h, i: (b * nq + i, h)),
        out_shape=jax.ShapeDtypeStruct((batch * seq, heads * LANES), BF16),
        scratch_shapes=[pltpu.VMEM((hp, rows, LANES), F32),
                        pltpu.VMEM((hp, rows, 2 * LANES), F32)],
        compiler_params=_cparams(("parallel", "parallel", "arbitrary")),
        name=("attn_diff" if diff else "attn_mla") + ("_bounded" if bounded else ""),
    )(q, k, v, *extra)


def _router_kernel(om_ref, od_ref, x_ref, wo_ref, fw_ref, rw_ref, rb_ref,
                   x1_ref, h_ref, route_ref, cnt_ref, carry_ref):
    tm = x_ref.shape[0]
    half = om_ref.shape[1]

    @pl.when(pl.program_id(0) == 0)
    def _():
        carry_ref[...] = jnp.zeros(carry_ref.shape, F32)

    y = (jnp.dot(om_ref[...], wo_ref[0:half, :], preferred_element_type=F32)
         + jnp.dot(od_ref[...], wo_ref[half:2 * half, :], preferred_element_type=F32))
    x1 = x_ref[...] + y
    x1_ref[...] = x1
    hh = x1 * lax.rsqrt(jnp.mean(x1 * x1, axis=-1, keepdims=True) + EPS) * fw_ref[...]
    h_ref[...] = _pack_bf16_pair(hh[:, :HALF_D], hh[:, HALF_D:])
    rw = rw_ref[...]
    rw_hi = rw.astype(BF16)
    rw_lo = (rw - rw_hi.astype(F32)).astype(BF16)
    hh_hi = hh.astype(BF16)
    hh_lo = (hh - hh_hi.astype(F32)).astype(BF16)
    logits = (jnp.dot(hh_hi, rw_hi, preferred_element_type=F32) + jnp.dot(hh_lo, rw_hi, preferred_element_type=F32)
              + jnp.dot(hh_hi, rw_lo, preferred_element_type=F32)) + rb_ref[...]
    lane = lax.broadcasted_iota(jnp.int32, (tm, LANES), 1).astype(F32)
    work = jnp.where(lane < N_EXPERTS, logits, -jnp.inf)
    vals, idxs = [], []
    for _ in range(TOP_K):
        mx = jnp.max(work, axis=-1, keepdims=True)
        ix = jnp.min(jnp.where(work == mx, lane, float(LANES)), axis=-1, keepdims=True)
        vals.append(mx)
        idxs.append(ix)
        work = jnp.where(lane == ix, -jnp.inf, work)
    es = [jnp.exp(v - vals[0]) for v in vals]
    den = es[0] + es[1] + es[2] + es[3]
    onehot = jnp.zeros((tm, LANES), F32)
    for ix in idxs:
        onehot = onehot + jnp.where(lane == ix, 1.0, 0.0)
    r_i = lax.broadcasted_iota(jnp.int32, (tm, tm), 0)
    c_i = lax.broadcasted_iota(jnp.int32, (tm, tm), 1)
    tri = jnp.where(c_i < r_i, 1.0, 0.0).astype(BF16)
    prefix = jnp.dot(tri, onehot.astype(BF16), preferred_element_type=F32) + carry_ref[...]
    route = jnp.zeros((tm, LANES), F32)
    for kk in range(TOP_K):
        rank = jnp.sum(jnp.where(lane == idxs[kk], prefix, 0.0), axis=-1, keepdims=True)
        route = jnp.where(lane == kk, idxs[kk], route)
        route = jnp.where(lane == TOP_K + kk, es[kk] / den, route)
        route = jnp.where(lane == 2 * TOP_K + kk, rank, route)
    route_ref[...] = route
    carry = carry_ref[...] + jnp.sum(onehot, axis=0, keepdims=True)
    carry_ref[...] = carry
    cnt_ref[...] = carry


def _router(om, od, x2, wo_b, fw, rw_pad, rb_pad):
    T = x2.shape[0]
    tm = RT_TM
    row = lambda i: (i, 0)
    outs = [
        jax.ShapeDtypeStruct((T, D_MODEL), F32),
        jax.ShapeDtypeStruct((T, HALF_D), jnp.uint32),
        jax.ShapeDtypeStruct((T, LANES), F32),
        jax.ShapeDtypeStruct((1, LANES), F32),
    ]
    return pl.pallas_call(
        _router_kernel,
        grid=(T // tm,),
        in_specs=[
            pl.BlockSpec((tm, om.shape[1]), row), pl.BlockSpec((tm, od.shape[1]), row),
            pl.BlockSpec((tm, D_MODEL), row),
            _const_spec(wo_b.shape), _const_spec(fw.shape), _const_spec(rw_pad.shape), _const_spec(rb_pad.shape),
        ],
        out_specs=[pl.BlockSpec((tm, D_MODEL), row), pl.BlockSpec((tm, HALF_D), row),
                   pl.BlockSpec((tm, LANES), row), pl.BlockSpec((1, LANES), lambda i: (0, 0))],
        out_shape=outs,
        scratch_shapes=[pltpu.VMEM((1, LANES), F32)],
        compiler_params=_cparams(("arbitrary",)),
        name="router",
    )(om, od, x2, wo_b, fw, rw_pad, rb_pad)


def _row_copy(src_ref, src_row, dst_ref, dst_row, sem):
    return pltpu.make_async_copy(src_ref.at[pl.ds(src_row, 1), :], dst_ref.at[pl.ds(dst_row, 1), :], sem)


def _dispatch_kernel(sv_ref, dest_ref, h_ref, xs_hbm, zero_ref, sem):
    tm = h_ref.shape[0]

    @pl.when(pl.program_id(0) == 0)
    def _():
        zero_ref[...] = jnp.zeros(zero_ref.shape, zero_ref.dtype)

        def zero_copy(sb):
            return pltpu.make_async_copy(zero_ref, xs_hbm.at[pl.ds(sb * MOE_TM, MOE_TM), :], sem)

        def start(sb, c):
            @pl.when(sv_ref[sb] < MOE_TM)
            def _():
                zero_copy(sb).start()
            return c

        def wait(sb, c):
            @pl.when(sv_ref[sb] < MOE_TM)
            def _():
                zero_copy(sb).wait()
            return c

        lax.fori_loop(0, sv_ref.shape[0], start, 0)
        lax.fori_loop(0, sv_ref.shape[0], wait, 0)

    def issue(r, c):
        for kk in range(TOP_K):
            _row_copy(h_ref, r, xs_hbm, dest_ref[0, 0, r * TOP_K + kk], sem).start(priority=kk % 2)
        return c

    lax.fori_loop(0, tm, issue, 0, unroll=4)
    for kk in range(TOP_K):
        pltpu.make_async_copy(h_ref, xs_hbm.at[pl.ds(0, tm), :], sem).wait()


def _dispatch(sb_valid, dest3, h, n_slots):
    T = h.shape[0]
    tm = DSP_TM
    grid_spec = pltpu.PrefetchScalarGridSpec(
        num_scalar_prefetch=1,
        grid=(T // tm,),
        in_specs=[pl.BlockSpec((1, 1, tm * TOP_K), lambda i, sv: (i, 0, 0), memory_space=pltpu.SMEM),
                  pl.BlockSpec((tm, HALF_D), lambda i, sv: (i, 0))],
        out_specs=pl.BlockSpec(memory_space=pl.ANY),
        scratch_shapes=[pltpu.VMEM((MOE_TM, HALF_D), jnp.uint32), pltpu.SemaphoreType.DMA(())],
    )
    return pl.pallas_call(
        _dispatch_kernel,
        grid_spec=grid_spec,
        out_shape=jax.ShapeDtypeStruct((n_slots, HALF_D), jnp.uint32),
        compiler_params=_cparams(("arbitrary",)),
        name="dispatch",
    )(sb_valid, dest3, h)


def _stream_weights(se_ref, nx_ref, slot, m, w_hbm, wst_ref, sem):
    j = pl.program_id(0)
    nj = pl.num_programs(0)
    tn = wst_ref.shape[2]

    def tile_copy(e, jj, s):
        return pltpu.make_async_copy(w_hbm.at[e, :, pl.ds(pl.multiple_of(jj * tn, tn), tn)], wst_ref.at[s], sem.at[s])

    @pl.when((j == 0) & (m == 0))
    def _():
        tile_copy(se_ref[0], 0, slot).start()

    prev = se_ref[jnp.maximum(m - 1, 0)]

    @pl.when((m == 0) | (se_ref[m] != prev))
    def _():
        tile_copy(se_ref[m], j, slot).wait()
        nxt = nx_ref[m]

        @pl.when(nxt >= 0)
        def _():
            tile_copy(nxt, j, 1 - slot).start()

        @pl.when((nxt < 0) & (j + 1 < nj))
        def _():
            tile_copy(se_ref[0], j + 1, 1 - slot).start()


def _tile_slot(od_ref, nu_ref, m):
    return (pl.program_id(0) * nu_ref[1] + od_ref[m]) % 2


def _for_row_count(nsub, o_ref, compute):
    @pl.when(nsub == 0)
    def _():
        o_ref[...] = jnp.zeros(o_ref.shape, o_ref.dtype)

    for s in range(1, NSUB + 1):
        @pl.when(nsub == s)
        def _(s=s):
            rows = s * SUB
            compute(rows)
            if rows < o_ref.shape[0]:
                o_ref[rows:, :] = jnp.zeros((o_ref.shape[0] - rows, o_ref.shape[1]), o_ref.dtype)


def _expert_a_kernel(se_ref, ns_ref, nx_ref, od_ref, nu_ref, x_ref, w_hbm, b_ref, o_ref, wst_ref, sem):
    m = pl.program_id(1)
    tn = wst_ref.shape[2]
    nsub = ns_ref[m]
    slot = _tile_slot(od_ref, nu_ref, m)

    @pl.when(nsub > 0)
    def _():
        _stream_weights(se_ref, nx_ref, slot, m, w_hbm, wst_ref, sem)

    def compute(rows):
        r_i = lax.broadcasted_iota(jnp.int32, (MXU_DIM, LANES), 0)
        c_i = lax.broadcasted_iota(jnp.int32, (MXU_DIM, LANES), 1)
        sel = jnp.where(r_i == 2 * c_i, 1.0, 0.0).astype(BF16)
        lo, hi = _unpack_bf16_pair(x_ref[0:rows, :])
        x = jnp.concatenate([lo.astype(BF16), hi.astype(BF16)], axis=1)
        gu = jnp.dot(x, wst_ref[slot].astype(BF16), preferred_element_type=F32) + b_ref[0]
        for c in range(tn // MXU_DIM):
            parts = []
            for s in range(MXU_DIM // LANES):
                g = gu[:, c * MXU_DIM + s * LANES:c * MXU_DIM + (s + 1) * LANES]
                gate = jnp.minimum(g, SWIGLU_LIMIT)
                up1 = jnp.clip(g, -SWIGLU_LIMIT, SWIGLU_LIMIT) + 1.0
                act = gate * jax.nn.sigmoid(SWIGLU_ALPHA * gate)
                parts.append((act * pltpu.roll(up1, LANES - 1, 1)).astype(BF16))
            pair = jnp.concatenate(parts, axis=1)
            o_ref[0:rows, c * LANES:(c + 1) * LANES] = jnp.dot(
                pair, sel, preferred_element_type=F32).astype(BF16)

    _for_row_count(nsub, o_ref, compute)


def _expert_b_kernel(se_ref, ns_ref, nx_ref, od_ref, nu_ref, a_ref, w_hbm, b_ref, o_ref, wst_ref, sem):
    m = pl.program_id(1)
    nsub = ns_ref[m]
    slot = _tile_slot(od_ref, nu_ref, m)

    @pl.when(nsub > 0)
    def _():
        _stream_weights(se_ref, nx_ref, slot, m, w_hbm, wst_ref, sem)

    def compute(rows):
        y = jnp.dot(a_ref[0:rows, :], wst_ref[slot].astype(BF16), preferred_element_type=F32) + b_ref[0]
        half = y.shape[1] // 2
        o_ref[0:rows, :] = _pack_bf16_pair(y[:, :half], y[:, half:])

    _for_row_count(nsub, o_ref, compute)


def _expert_call(kernel, name, sb_e, sb_nsub, sb_next_e, sb_ord, n_used, xin, w, b3, out_cols, out_dtype, out_tn):
    n_sb = sb_e.shape[0]
    tm, tn = MOE_TM, MOE_TN
    kdim = w.shape[1]
    nj = w.shape[2] // tn

    def blk(m, nu):
        return jnp.maximum(jnp.minimum(m, nu[0] - 1), 0)

    grid_spec = pltpu.PrefetchScalarGridSpec(
        num_scalar_prefetch=5,
        grid=(nj, n_sb),
        in_specs=[
            pl.BlockSpec((tm, xin.shape[1]), lambda j, m, se, ns, nx, od, nu: (blk(m, nu), 0)),
            pl.BlockSpec(memory_space=pl.ANY),
            pl.BlockSpec((1, 1, tn), lambda j, m, se, ns, nx, od, nu: (se[blk(m, nu)], 0, j)),
        ],
        out_specs=pl.BlockSpec((tm, out_tn), lambda j, m, se, ns, nx, od, nu: (m, j)),
        scratch_shapes=[pltpu.VMEM((2, kdim, tn), F32), pltpu.SemaphoreType.DMA((2,))],
    )
    return pl.pallas_call(
        kernel,
        grid_spec=grid_spec,
        out_shape=jax.ShapeDtypeStruct((n_sb * tm, out_cols), out_dtype),
        compiler_params=_cparams(("arbitrary", "arbitrary")),
        name=name,
    )(sb_e, sb_nsub, sb_next_e, sb_ord, n_used, xin, w, b3)


def _combine_kernel(dest_ref, nxt_ref, y_hbm, x1_ref, route_ref, o_ref, buf2_ref, sem):
    i = pl.program_id(0)
    n = pl.num_programs(0)
    tm = o_ref.shape[0]

    def gather(idx_ref, slot):
        def issue(r, c):
            for kk in range(TOP_K):
                _row_copy(y_hbm, idx_ref[0, 0, r * TOP_K + kk], buf2_ref.at[slot, kk], r,
                          sem.at[slot]).start(priority=kk % 2)
            return c
        lax.fori_loop(0, tm, issue, 0, unroll=4)

    @pl.when(i == 0)
    def _():
        gather(dest_ref, 0)

    @pl.when(i + 1 < n)
    def _():
        gather(nxt_ref, (i + 1) % 2)

    slot = i % 2
    buf_ref = buf2_ref.at[slot]
    for kk in range(TOP_K):
        pltpu.make_async_copy(y_hbm.at[pl.ds(0, tm), :], buf_ref.at[kk], sem.at[slot]).wait()
    route = route_ref[...]
    hw = MOE_TN // 2
    for j in range(D_MODEL // MOE_TN):
        lo_acc = x1_ref[:, j * MOE_TN:j * MOE_TN + hw]
        hi_acc = x1_ref[:, j * MOE_TN + hw:(j + 1) * MOE_TN]
        for kk in range(TOP_K):
            gate = route[:, TOP_K + kk:TOP_K + kk + 1]
            lo, hi = _unpack_bf16_pair(buf_ref[kk, :, j * hw:(j + 1) * hw])
            lo_acc = lo_acc + gate * lo
            hi_acc = hi_acc + gate * hi
        o_ref[:, j * MOE_TN:j * MOE_TN + hw] = lo_acc
        o_ref[:, j * MOE_TN + hw:(j + 1) * MOE_TN] = hi_acc


def _combine(dest3, y, x1, route):
    T = x1.shape[0]
    tm = CMB_TM
    nblk = T // tm
    row = lambda i: (i, 0)
    return pl.pallas_call(
        _combine_kernel,
        grid=(nblk,),
        in_specs=[pl.BlockSpec((1, 1, tm * TOP_K), lambda i: (i, 0, 0), memory_space=pltpu.SMEM),
                  pl.BlockSpec((1, 1, tm * TOP_K), lambda i: (jnp.minimum(i + 1, nblk - 1), 0, 0),
                               memory_space=pltpu.SMEM),
                  pl.BlockSpec(memory_space=pl.ANY),
                  pl.BlockSpec((tm, D_MODEL), row),
                  pl.BlockSpec((tm, LANES), row)],
        out_specs=pl.BlockSpec((tm, D_MODEL), row),
        out_shape=jax.ShapeDtypeStruct((T, D_MODEL), F32),
        scratch_shapes=[pltpu.VMEM((2, TOP_K, tm, HALF_D), jnp.uint32), pltpu.SemaphoreType.DMA((2,))],
        compiler_params=_cparams(("arbitrary",)),
        name="combine",
    )(dest3, dest3, y, x1, route)


def _pad_lanes(v, width=LANES):
    return jnp.pad(v, [(0, 0)] * (v.ndim - 1) + [(0, width - v.shape[-1])])


def _swap_halves(v):
    half = v.shape[-1] // 2
    return jnp.concatenate([v[..., half:], v[..., :half]], axis=-1)


def _token_mixers(x2, B, S, attn_norm_w, w_in, q_a_norm_w, w_uq, kv_a_norm_w, w_ukv, mla_q_norm_w, mla_k_norm_w,
                  diff_q_norm_w, diff_k_norm_w, lambda_q1, lambda_k1, lambda_q2, lambda_k2, diff_subln_w, rel_bias):
    L = 0
    win = w_in[L]
    sp = np.cumsum([MLA_Q_RANK, MLA_KV_RANK, MLA_ROPE])
    w_kr = win[:, sp[1]:sp[2]]
    win_r = jnp.concatenate(
        [win[:, :sp[1]], _pad_lanes(w_kr), _pad_lanes(_swap_halves(w_kr)), win[:, sp[2]:]], axis=1).astype(BF16)
    qk = MLA_NOPE + MLA_ROPE
    wuq3 = w_uq[L].reshape(MLA_Q_RANK, MLA_HEADS, qk)
    wuq_rope = wuq3[:, :, MLA_NOPE:]
    wuq_r = jnp.concatenate([
        wuq3[:, :, :MLA_NOPE].reshape(MLA_Q_RANK, -1),
        _pad_lanes(wuq_rope).reshape(MLA_Q_RANK, -1),
        _pad_lanes(_swap_halves(wuq_rope)).reshape(MLA_Q_RANK, -1)], axis=1).astype(BF16)
    wukv_b = w_ukv[L].astype(BF16)
    qn_w, kn_w = mla_q_norm_w[L], mla_k_norm_w[L]
    vecs = jnp.stack([
        qn_w[:MLA_NOPE], _pad_lanes(qn_w[MLA_NOPE:]), _pad_lanes(_swap_halves(qn_w[MLA_NOPE:])),
        kn_w[:MLA_NOPE], _pad_lanes(kn_w[MLA_NOPE:]), _pad_lanes(_swap_halves(kn_w[MLA_NOPE:])),
        jnp.tile(diff_q_norm_w[L], 2), jnp.tile(diff_k_norm_w[L], 2)])
    lamv = _pad_lanes(jnp.stack([lambda_q1[L], lambda_k1[L], lambda_q2[L], lambda_k2[L]]))

    inv_freq = np.float32(ROPE_THETA) ** (-np.arange(0, MLA_ROPE, 2, dtype=np.float32) / np.float32(MLA_ROPE))
    ang = np.arange(S, dtype=np.float32)[:, None] * inv_freq[None, :].astype(np.float32)
    cos, sin = np.cos(ang).astype(np.float32), np.sin(ang).astype(np.float32)
    zpad = np.zeros((S, LANES - MLA_ROPE), np.float32)
    cos128 = jnp.asarray(np.concatenate([cos, cos, zpad], axis=1))
    sin128 = jnp.asarray(np.concatenate([-sin, sin, zpad], axis=1))

    qm, km, vm, qd, kd, vd = _prep(x2, attn_norm_w[L][None], win_r, q_a_norm_w[L][None], wuq_r,
                                   kv_a_norm_w[L][None], wukv_b, vecs, cos128, sin128, S)
    nbias = _relbias(rel_bias, ATT_TQ, ATT_TK)

    amax = lambda v: jnp.max(jnp.abs(v))
    bound_mla = qk * qk ** -0.5 * LOG2E * amax(qn_w) * amax(kn_w)
    bound_diff = (DIFF_DH * DIFF_DH ** -0.5 * LOG2E * amax(diff_q_norm_w[L]) * amax(diff_k_norm_w[L])
                  + LOG2E * amax(rel_bias - rel_bias[REL_BUCKETS // 2 - 1][None, :]))
    bounded = jnp.maximum(bound_mla, bound_diff) * SCORE_SLACK <= SCORE_LIMIT

    def mixers(is_bounded):
        def run(qm, km, vm, qd, kd, vd, nbias, lamv, subw):
            om = _attention(qm, km, vm, B, S, MLA_HEADS, 2 * LANES, diff=False, bounded=is_bounded)
            od = _attention(qd, kd, vd, B, S, DIFF_HEADS, LANES, diff=True, bounded=is_bounded,
                            extra=(nbias, lamv, subw))
            return om, od
        return run

    return lax.cond(bounded, mixers(True), mixers(False),
                    qm, km, vm, qd, kd, vd, nbias, lamv, diff_subln_w[L][None])


def _moe_block(om, od, x2, w_o, ffn_norm_w, router_w, router_b, w_gate_up, b_gate_up, w_down, b_down):
    L = 0
    T, D = x2.shape
    wo_b = w_o[L].astype(BF16)
    rw_pad = _pad_lanes(router_w[L])
    rb_pad = _pad_lanes(router_b[L][None, :])

    x1, h, route, cnt = _router(om, od, x2, wo_b, ffn_norm_w[L][None], rw_pad, rb_pad)

    tm = MOE_TM
    n_sb = -(-T * TOP_K // tm) + N_EXPERTS
    idx = route[:, 0:TOP_K].astype(jnp.int32)
    rank = route[:, 2 * TOP_K:3 * TOP_K].astype(jnp.int32)
    counts = cnt[0, :N_EXPERTS].astype(jnp.int32)
    padded = (counts + tm - 1) // tm * tm
    pad_ends = jnp.cumsum(padded)
    pad_starts = pad_ends - padded
    dest = pad_starts[idx] + rank
    sb_start = jnp.arange(n_sb, dtype=jnp.int32) * tm
    sb_e = jnp.minimum(jnp.sum((pad_ends[None, :] <= sb_start[:, None]).astype(jnp.int32), axis=1), N_EXPERTS - 1)
    sb_valid = jnp.clip(pad_starts[sb_e] + counts[sb_e] - sb_start, 0, tm)
    sb_valid = jnp.where(sb_start < pad_ends[-1], sb_valid, 0)
    sb_nsub = (sb_valid + SUB - 1) // SUB
    owns = (counts > 0).astype(jnp.int32)
    n_used = jnp.stack([pad_ends[-1] // tm, jnp.sum(owns)]).astype(jnp.int32)
    nxt_sb = pad_ends[sb_e] // tm
    sb_next_e = jnp.where(nxt_sb < n_used[0], sb_e[jnp.minimum(nxt_sb, n_sb - 1)], -1).astype(jnp.int32)
    sb_ord = (jnp.cumsum(owns) - 1)[sb_e].astype(jnp.int32)

    xs = _dispatch(sb_valid.astype(jnp.int32), dest.reshape(T // DSP_TM, 1, DSP_TM * TOP_K), h, n_sb * tm)
    act = _expert_call(_expert_a_kernel, "expert_a", sb_e, sb_nsub, sb_next_e, sb_ord, n_used, xs, w_gate_up[L],
                       b_gate_up[L][:, None, :], D_FF, BF16, MOE_TN // 2)
    y = _expert_call(_expert_b_kernel, "expert_b", sb_e, sb_nsub, sb_next_e, sb_ord, n_used, act, w_down[L],
                     b_down[L][:, None, :], D // 2, jnp.uint32, MOE_TN // 2)
    return _combine(dest.reshape(T // CMB_TM, 1, CMB_TM * TOP_K), y, x1, route)


def kernel(x, attn_norm_w, w_in, q_a_norm_w, w_uq, kv_a_norm_w, w_ukv, mla_q_norm_w, mla_k_norm_w, diff_q_norm_w, diff_k_norm_w, lambda_q1, lambda_k1, lambda_q2, lambda_k2, diff_subln_w, w_o, ffn_norm_w, router_w, router_b, w_gate_up, b_gate_up, w_down, b_down, rel_bias):
    B, S, D = x.shape
    x2 = x.reshape(B * S, D)
    om, od = _token_mixers(x2, B, S, attn_norm_w, w_in, q_a_norm_w, w_uq, kv_a_norm_w, w_ukv, mla_q_norm_w,
                           mla_k_norm_w, diff_q_norm_w, diff_k_norm_w, lambda_q1, lambda_k1, lambda_q2, lambda_k2,
                           diff_subln_w, rel_bias)
    out = _moe_block(om, od, x2, w_o, ffn_norm_w, router_w, router_b, w_gate_up, b_gate_up, w_down, b_down)
    return out.reshape(B, S, D)
```

```python
import functools
import math

import jax
import jax.numpy as jnp
import numpy as np
from jax import lax
from jax.experimental import pallas as pl
from jax.experimental.pallas import tpu as pltpu

D_MODEL = 2048
CHUNK = 64
MLA_HEADS = 8
MLA_NOPE = 128
MLA_ROPE = 64
MLA_V = 128
MLA_Q_RANK = 384
MLA_KV_RANK = 256
ROPE_THETA = 10000.0
DIFF_HEADS = 8
DIFF_DH = 64
DIFF_V = 128
REL_BUCKETS = 32
REL_MAX_DIST = 128
N_EXPERTS = 32
TOP_K = 4
D_FF = 2048
SWIGLU_LIMIT = 7.0
SWIGLU_ALPHA = 1.702
EPS = 1e-6
LAMBDA_INIT = 0.8 - 0.6 * math.exp(-0.3 * 0)

LANES = 128
VMEM_LIMIT = 48 * 1024 * 1024

PREP_TM = 256
ATT_TQ = 512
ATT_TK = 512
ATT_HP = 4
RT_TM = 512
SUB = 256
MOE_TM = 1024
NSUB = MOE_TM // SUB
MOE_TN = 1024
CMB_TM = 128
HALF_D = D_MODEL // 2
DSP_TM = 256
ROUTE_ROWS = 16

LOG2E = math.log2(math.e)
NEG = -1e30
SCORE_LIMIT = 80.0
SCORE_SLACK = 1.05
F32 = jnp.float32
BF16 = jnp.bfloat16

_C_CQ = 0
_C_CKV = MLA_Q_RANK
_C_KR = _C_CKV + MLA_KV_RANK
_C_KRR = _C_KR + LANES
_C_MLA_END = _C_KRR + LANES
_C_DQ = _C_MLA_END
_C_DK = _C_DQ + DIFF_HEADS * 2 * DIFF_DH
_C_DV = _C_DK + DIFF_HEADS * 2 * DIFF_DH
_C_END = _C_DV + DIFF_HEADS * DIFF_V


def _cparams(sem):
    return pltpu.CompilerParams(dimension_semantics=sem, vmem_limit_bytes=VMEM_LIMIT)


def _const_spec(shape):
    nd = len(shape)
    return pl.BlockSpec(shape, lambda *_: (0,) * nd, pipeline_mode=pl.Buffered(1))


def _pack_bf16_pair(lo, hi):
    lo_b = lax.bitcast_convert_type(lo.astype(BF16).astype(F32), jnp.uint32)
    hi_b = lax.bitcast_convert_type(hi.astype(BF16).astype(F32), jnp.uint32)
    return (lo_b >> 16) | (hi_b & jnp.uint32(0xFFFF0000))


def _unpack_bf16_pair(w):
    return (lax.bitcast_convert_type(w << 16, F32),
            lax.bitcast_convert_type(w & jnp.uint32(0xFFFF0000), F32))


def _prep_kernel(x_ref, anw_ref, win_ref, qaw_ref, wuq_ref, kvaw_ref, wukv_ref, vec_ref, cos_ref, sin_ref,
                 qm_ref, km_ref, vm_ref, qd_ref, kd_ref, vd_ref):
    xf = x_ref[...]
    inv = lax.rsqrt(jnp.mean(xf * xf, axis=-1, keepdims=True) + EPS)
    h = (xf * inv * anw_ref[...]).astype(BF16)

    def rms(v, w):
        return v * lax.rsqrt(jnp.mean(v * v, axis=-1, keepdims=True) + EPS) * w

    pm = jnp.dot(h, win_ref[:, _C_CQ:_C_MLA_END], preferred_element_type=F32)
    cqn = rms(pm[:, _C_CQ:_C_CKV], qaw_ref[...]).astype(BF16)
    ckn = rms(pm[:, _C_CKV:_C_KR], kvaw_ref[...]).astype(BF16)
    kr = pm[:, _C_KR:_C_KRR]
    krr = pm[:, _C_KRR:_C_MLA_END]
    qall = jnp.dot(cqn, wuq_ref[...], preferred_element_type=F32)
    kvall = jnp.dot(ckn, wukv_ref[...], preferred_element_type=F32)

    vec = vec_ref[...]
    wqn, wqr, wqrr = vec[0:1], vec[1:2], vec[2:3]
    wkn, wkr, wkrr = vec[3:4], vec[4:5], vec[5:6]
    wdq, wdk = vec[6:7], vec[7:8]
    cos = cos_ref[...]
    sin = sin_ref[...]
    cq_r, sq_r = wqr * cos, wqrr * sin
    ck_r, sk_r = wkr * cos, wkrr * sin
    kr2 = kr * kr
    k_rope_base = kr * ck_r + krr * sk_r
    q_scale = (MLA_NOPE + MLA_ROPE) ** -0.5 * LOG2E
    inv_qk = 1.0 / (MLA_NOPE + MLA_ROPE)
    nq = MLA_HEADS * LANES
    for hh in range(MLA_HEADS):
        a = hh * LANES
        qn = qall[:, a:a + LANES]
        qr = qall[:, nq + a:nq + a + LANES]
        qrr = qall[:, 2 * nq + a:2 * nq + a + LANES]
        iq = lax.rsqrt(jnp.sum(qn * qn + qr * qr, axis=-1, keepdims=True) * inv_qk + EPS) * q_scale
        qm_ref[:, 2 * a:2 * a + LANES] = (qn * iq * wqn).astype(BF16)
        qm_ref[:, 2 * a + LANES:2 * a + 2 * LANES] = ((qr * cq_r + qrr * sq_r) * iq).astype(BF16)
        kn = kvall[:, 2 * a:2 * a + LANES]
        ik = lax.rsqrt(jnp.sum(kn * kn + kr2, axis=-1, keepdims=True) * inv_qk + EPS)
        km_ref[:, 2 * a:2 * a + LANES] = (kn * ik * wkn).astype(BF16)
        km_ref[:, 2 * a + LANES:2 * a + 2 * LANES] = (k_rope_base * ik).astype(BF16)
        vm_ref[:, a:a + LANES] = kvall[:, 2 * a + LANES:2 * a + 2 * LANES].astype(BF16)

    dq = jnp.dot(h, win_ref[:, _C_DQ:_C_DK], preferred_element_type=F32)
    dk = jnp.dot(h, win_ref[:, _C_DK:_C_DV], preferred_element_type=F32)
    vd_ref[...] = jnp.dot(h, win_ref[:, _C_DV:_C_END], preferred_element_type=F32).astype(BF16)
    lane = lax.broadcasted_iota(jnp.int32, (xf.shape[0], LANES), 1)
    lo = lane < DIFF_DH
    d_scale = DIFF_DH ** -0.5 * LOG2E
    inv_dh = 1.0 / DIFF_DH

    def half_norm(v):
        sq = v * v
        s1 = jnp.sum(jnp.where(lo, sq, 0.0), axis=-1, keepdims=True)
        s2 = jnp.sum(jnp.where(lo, 0.0, sq), axis=-1, keepdims=True)
        return v * jnp.where(lo, lax.rsqrt(s1 * inv_dh + EPS), lax.rsqrt(s2 * inv_dh + EPS))

    for hh in range(DIFF_HEADS):
        a = hh * LANES
        qn = half_norm(dq[:, a:a + LANES]) * (wdq * d_scale)
        qd_ref[:, 2 * a:2 * a + LANES] = jnp.where(lo, qn, 0.0).astype(BF16)
        qd_ref[:, 2 * a + LANES:2 * a + 2 * LANES] = jnp.where(lo, 0.0, qn).astype(BF16)
        kd_ref[:, a:a + LANES] = (half_norm(dk[:, a:a + LANES]) * wdk).astype(BF16)


def _prep(x2, anw, win_r, qaw, wuq_r, kvaw, wukv_b, vecs, cos128, sin128, seq):
    T = x2.shape[0]
    tm = PREP_TM
    nseq = seq // tm
    row = lambda i: (i, 0)
    outs = [
        jax.ShapeDtypeStruct((T, MLA_HEADS * 2 * LANES), BF16),
        jax.ShapeDtypeStruct((T, MLA_HEADS * 2 * LANES), BF16),
        jax.ShapeDtypeStruct((T, MLA_HEADS * MLA_V), BF16),
        jax.ShapeDtypeStruct((T, DIFF_HEADS * 2 * LANES), BF16),
        jax.ShapeDtypeStruct((T, DIFF_HEADS * LANES), BF16),
        jax.ShapeDtypeStruct((T, DIFF_HEADS * DIFF_V), BF16),
    ]
    return pl.pallas_call(
        _prep_kernel,
        grid=(T // tm,),
        in_specs=[
            pl.BlockSpec((tm, D_MODEL), row),
            _const_spec(anw.shape), _const_spec(win_r.shape), _const_spec(qaw.shape), _const_spec(wuq_r.shape),
            _const_spec(kvaw.shape), _const_spec(wukv_b.shape), _const_spec(vecs.shape),
            pl.BlockSpec((tm, LANES), lambda i: (i % nseq, 0)),
            pl.BlockSpec((tm, LANES), lambda i: (i % nseq, 0)),
        ],
        out_specs=[pl.BlockSpec((tm, o.shape[1]), row) for o in outs],
        out_shape=outs,
        compiler_params=_cparams(("parallel",)),
        name="prep",
    )(x2, anw, win_r, qaw, wuq_r, kvaw, wukv_b, vecs, cos128, sin128)


def _relbias_kernel(rb_ref, o_ref, *, tq, tk):
    hh = pl.program_id(0)
    which = pl.program_id(1)
    row = lax.broadcasted_iota(jnp.int32, (tq, tk), 0)
    col = lax.broadcasted_iota(jnp.int32, (tq, tk), 1)
    rel = col - which * tk - row
    nb = REL_BUCKETS // 2
    max_exact = nb // 2
    ret = jnp.where(rel > 0, nb, 0)
    n = jnp.abs(rel)
    nf = jnp.maximum(n, 1).astype(F32)
    large = max_exact + (jnp.log(nf / max_exact) / math.log(REL_MAX_DIST / max_exact)
                         * (nb - max_exact)).astype(jnp.int32)
    large = jnp.minimum(large, nb - 1)
    bucket = ret + jnp.where(n < max_exact, n, large)
    bias = jnp.zeros((tq, tk), F32)
    for b in range(REL_BUCKETS):
        bias = jnp.where(bucket == b, rb_ref[b, hh], bias)
    bias = (bias - rb_ref[nb - 1, hh]) * LOG2E
    allowed = ((col // CHUNK) <= (row // CHUNK)) | (which > 0)
    o_ref[0, 0] = jnp.where(allowed, bias, NEG)


def _relbias(rel_bias, tq, tk):
    return pl.pallas_call(
        functools.partial(_relbias_kernel, tq=tq, tk=tk),
        grid=(DIFF_HEADS, 2),
        in_specs=[pl.BlockSpec(memory_space=pltpu.SMEM)],
        out_specs=pl.BlockSpec((1, 1, tq, tk), lambda h, w: (h, w, 0, 0)),
        out_shape=jax.ShapeDtypeStruct((DIFF_HEADS, 2, tq, tk), F32),
        compiler_params=_cparams(("parallel", "parallel")),
        name="relbias",
    )(rel_bias)


def _attn_kernel(*refs, diff, tq, tk, bounded):
    if diff:
        q_ref, k_ref, v_ref, nb_ref, lam_ref, sub_ref, o_ref, m_ref, acc_ref = refs
    else:
        q_ref, k_ref, v_ref, o_ref, m_ref, acc_ref = refs
    i = pl.program_id(2)
    hp = o_ref.shape[1] // LANES
    dk = k_ref.shape[1] // hp
    qs = []
    for a in range(hp):
        blk = q_ref[:, a * 2 * LANES:(a + 1) * 2 * LANES]
        qs.append(jnp.concatenate([blk[:, :LANES], blk[:, LANES:]], axis=0) if diff else blk)
    m_ref[...] = jnp.full(m_ref.shape, NEG, F32)
    acc_ref[...] = jnp.zeros(acc_ref.shape, F32)
    ones = jnp.ones((tk, LANES), BF16)

    def step(j, biases):
        start = pl.multiple_of(j * tk, tk)
        for a in range(hp):
            ks = k_ref[pl.ds(start, tk), a * dk:(a + 1) * dk]
            vs = jnp.concatenate([v_ref[pl.ds(start, tk), a * LANES:(a + 1) * LANES], ones], axis=1)
            s = lax.dot_general(qs[a], ks, (((1,), (1,)), ((), ())), preferred_element_type=F32)
            if biases is not None:
                s = s + biases[a]
            if bounded:
                acc_ref[a] += jnp.dot(jnp.exp2(s.astype(BF16)), vs, preferred_element_type=F32)
                continue
            m_prev = m_ref[a]
            m_new = jnp.maximum(m_prev, jnp.max(s, axis=1, keepdims=True))
            alpha = jnp.exp2(m_prev - m_new)
            p = jnp.exp2((s - jnp.concatenate([m_new] * (tk // LANES), axis=1)).astype(BF16))
            acc_ref[a] = (jnp.concatenate([alpha, alpha], axis=1) * acc_ref[a]
                          + jnp.dot(p, vs, preferred_element_type=F32))
            m_ref[a] = m_new

    def far_tiles(n):
        def pair(j2, c):
            step(2 * j2, None)
            step(2 * j2 + 1, None)
            return c

        lax.fori_loop(0, n // 2, pair, 0)

        @pl.when(n % 2 == 1)
        def _():
            step(n - 1, None)

    far_tiles(jnp.maximum(i - 1, 0))

    def last_tiles(with_prev):
        if diff:
            if with_prev:
                step(i - 1, [jnp.concatenate([nb_ref[a, 1]] * 2, axis=0) for a in range(hp)])
            step(i, [jnp.concatenate([nb_ref[a, 0]] * 2, axis=0) for a in range(hp)])
        else:
            if with_prev:
                step(i - 1, None)
            row = lax.broadcasted_iota(jnp.int32, (tq, tk), 0)
            col = lax.broadcasted_iota(jnp.int32, (tq, tk), 1)
            step(i, [jnp.where((col // CHUNK) <= (row // CHUNK), 0.0, NEG)] * hp)

    @pl.when(i > 0)
    def _():
        last_tiles(True)

    @pl.when(i == 0)
    def _():
        last_tiles(False)

    for a in range(hp):
        o = acc_ref[a, :, 0:LANES] / acc_ref[a, :, LANES:2 * LANES]
        if diff:
            lv = lam_ref[...]
            lam = (jnp.exp(jnp.sum(lv[0:1] * lv[1:2], axis=1, keepdims=True))
                   - jnp.exp(jnp.sum(lv[2:3] * lv[3:4], axis=1, keepdims=True)) + LAMBDA_INIT)
            od = o[:tq] - lam * o[tq:]
            od = od * lax.rsqrt(jnp.mean(od * od, axis=-1, keepdims=True) + EPS) * sub_ref[...]
            o_ref[:, a * LANES:(a + 1) * LANES] = (od * (1.0 - LAMBDA_INIT)).astype(BF16)
        else:
            o_ref[:, a * LANES:(a + 1) * LANES] = o.astype(BF16)


def _attention(q, k, v, batch, seq, heads, dk, diff, bounded, extra=()):
    tq, tk, hp = ATT_TQ, ATT_TK, ATT_HP
    assert tq == tk and tk >= REL_MAX_DIST and seq % tq == 0 and heads % hp == 0
    nq = seq // tq
    rows = 2 * tq if diff else tq
    in_specs = [
        pl.BlockSpec((tq, hp * 2 * LANES), lambda b, h, i: (b * nq + i, h)),
        pl.BlockSpec((seq, hp * dk), lambda b, h, i: (b, h), pipeline_mode=pl.Buffered(1)),
        pl.BlockSpec((seq, hp * LANES), lambda b, h, i: (b, h), pipeline_mode=pl.Buffered(1)),
    ]
    if diff:
        nbias, lamv, subw = extra
        in_specs += [
            pl.BlockSpec((hp, 2, tq, tk), lambda b, h, i: (h, 0, 0, 0), pipeline_mode=pl.Buffered(1)),
            pl.BlockSpec(lamv.shape, lambda b, h, i: (0, 0)),
            pl.BlockSpec(subw.shape, lambda b, h, i: (0, 0)),
        ]
    return pl.pallas_call(
        functools.partial(_attn_kernel, diff=diff, tq=tq, tk=tk, bounded=bounded),
        grid=(batch, heads // hp, nq),
        in_specs=in_specs,
        out_specs=pl.BlockSpec((tq, hp * LANES), lambda b, h, i: (b * nq + i, h)),
        out_shape=jax.ShapeDtypeStruct((batch * seq, heads * LANES), BF16),
        scratch_shapes=[pltpu.VMEM((hp, rows, LANES), F32),
                        pltpu.VMEM((hp, rows, 2 * LANES), F32)],
        compiler_params=_cparams(("parallel", "parallel", "arbitrary")),
        name=("attn_diff" if diff else "attn_mla") + ("_bounded" if bounded else ""),
    )(q, k, v, *extra)


def _router_kernel(om_ref, od_ref, x_ref, wo_ref, fw_ref, rw_ref, rb_ref,
                   x1_ref, h_ref, route_ref, rt_ref, cnt_ref, carry_ref):
    tm = x_ref.shape[0]
    half = om_ref.shape[1]

    @pl.when(pl.program_id(0) == 0)
    def _():
        carry_ref[...] = jnp.zeros(carry_ref.shape, F32)

    y = (jnp.dot(om_ref[...], wo_ref[0:half, :], preferred_element_type=F32)
         + jnp.dot(od_ref[...], wo_ref[half:2 * half, :], preferred_element_type=F32))
    x1 = x_ref[...] + y
    x1_ref[...] = x1
    hh = x1 * lax.rsqrt(jnp.mean(x1 * x1, axis=-1, keepdims=True) + EPS) * fw_ref[...]
    h_ref[...] = _pack_bf16_pair(hh[:, :HALF_D], hh[:, HALF_D:])
    rw = rw_ref[...]
    rw_hi = rw.astype(BF16)
    rw_lo = (rw - rw_hi.astype(F32)).astype(BF16)
    hh_hi = hh.astype(BF16)
    hh_lo = (hh - hh_hi.astype(F32)).astype(BF16)
    logits = (jnp.dot(hh_hi, rw_hi, preferred_element_type=F32) + jnp.dot(hh_lo, rw_hi, preferred_element_type=F32)
              + jnp.dot(hh_hi, rw_lo, preferred_element_type=F32)) + rb_ref[...]
    lane = lax.broadcasted_iota(jnp.int32, (tm, LANES), 1).astype(F32)
    work = jnp.where(lane < N_EXPERTS, logits, -jnp.inf)
    vals, idxs = [], []
    for _ in range(TOP_K):
        mx = jnp.max(work, axis=-1, keepdims=True)
        ix = jnp.min(jnp.where(work == mx, lane, float(LANES)), axis=-1, keepdims=True)
        vals.append(mx)
        idxs.append(ix)
        work = jnp.where(lane == ix, -jnp.inf, work)
    es = [jnp.exp(v - vals[0]) for v in vals]
    den = es[0] + es[1] + es[2] + es[3]
    onehot = jnp.zeros((tm, LANES), F32)
    for ix in idxs:
        onehot = onehot + jnp.where(lane == ix, 1.0, 0.0)
    r_i = lax.broadcasted_iota(jnp.int32, (tm, tm), 0)
    c_i = lax.broadcasted_iota(jnp.int32, (tm, tm), 1)
    tri = jnp.where(c_i < r_i, 1.0, 0.0).astype(BF16)
    prefix = jnp.dot(tri, onehot.astype(BF16), preferred_element_type=F32) + carry_ref[...]
    route = jnp.zeros((tm, LANES), F32)
    for kk in range(TOP_K):
        rank = jnp.sum(jnp.where(lane == idxs[kk], prefix, 0.0), axis=-1, keepdims=True)
        route = jnp.where(lane == kk, idxs[kk], route)
        route = jnp.where(lane == TOP_K + kk, es[kk] / den, route)
        route = jnp.where(lane == 2 * TOP_K + kk, rank, route)
    route_ref[...] = route
    rt_ref[...] = route.T[0:rt_ref.shape[0], :]
    carry = carry_ref[...] + jnp.sum(onehot, axis=0, keepdims=True)
    carry_ref[...] = carry
    cnt_ref[...] = carry


def _router(om, od, x2, wo_b, fw, rw_pad, rb_pad):
    T = x2.shape[0]
    tm = RT_TM
    row = lambda i: (i, 0)
    outs = [
        jax.ShapeDtypeStruct((T, D_MODEL), F32),
        jax.ShapeDtypeStruct((T, HALF_D), jnp.uint32),
        jax.ShapeDtypeStruct((T, LANES), F32),
        jax.ShapeDtypeStruct((ROUTE_ROWS, T), F32),
        jax.ShapeDtypeStruct((1, LANES), F32),
    ]
    return pl.pallas_call(
        _router_kernel,
        grid=(T // tm,),
        in_specs=[
            pl.BlockSpec((tm, om.shape[1]), row), pl.BlockSpec((tm, od.shape[1]), row),
            pl.BlockSpec((tm, D_MODEL), row),
            _const_spec(wo_b.shape), _const_spec(fw.shape), _const_spec(rw_pad.shape), _const_spec(rb_pad.shape),
        ],
        out_specs=[pl.BlockSpec((tm, D_MODEL), row), pl.BlockSpec((tm, HALF_D), row),
                   pl.BlockSpec((tm, LANES), row), pl.BlockSpec((ROUTE_ROWS, tm), lambda i: (0, i)),
                   pl.BlockSpec((1, LANES), lambda i: (0, 0))],
        out_shape=outs,
        scratch_shapes=[pltpu.VMEM((1, LANES), F32)],
        compiler_params=_cparams(("arbitrary",)),
        name="router",
    )(om, od, x2, wo_b, fw, rw_pad, rb_pad)


def _row_copy(src_ref, src_row, dst_ref, dst_row, sem):
    return pltpu.make_async_copy(src_ref.at[pl.ds(src_row, 1), :], dst_ref.at[pl.ds(dst_row, 1), :], sem)


def _dispatch_kernel(sv_ref, dest_ref, h_ref, xs_hbm, zero_ref, sem):
    tm = h_ref.shape[0]

    @pl.when(pl.program_id(0) == 0)
    def _():
        zero_ref[...] = jnp.zeros(zero_ref.shape, zero_ref.dtype)

        def zero_copy(sb):
            return pltpu.make_async_copy(zero_ref, xs_hbm.at[pl.ds(sb * MOE_TM, MOE_TM), :], sem)

        def start(sb, c):
            @pl.when(sv_ref[sb] < MOE_TM)
            def _():
                zero_copy(sb).start()
            return c

        def wait(sb, c):
            @pl.when(sv_ref[sb] < MOE_TM)
            def _():
                zero_copy(sb).wait()
            return c

        lax.fori_loop(0, sv_ref.shape[0], start, 0)
        lax.fori_loop(0, sv_ref.shape[0], wait, 0)

    def issue(r, c):
        for kk in range(TOP_K):
            _row_copy(h_ref, r, xs_hbm, dest_ref[0, kk, r], sem).start(priority=kk % 2)
        return c

    lax.fori_loop(0, tm, issue, 0, unroll=4)
    for kk in range(TOP_K):
        pltpu.make_async_copy(h_ref, xs_hbm.at[pl.ds(0, tm), :], sem).wait()


def _dispatch(sb_valid, dest3, h, n_slots):
    T = h.shape[0]
    tm = DSP_TM
    grid_spec = pltpu.PrefetchScalarGridSpec(
        num_scalar_prefetch=1,
        grid=(T // tm,),
        in_specs=[pl.BlockSpec((1, TOP_K, tm), lambda i, sv: (i, 0, 0), memory_space=pltpu.SMEM),
                  pl.BlockSpec((tm, HALF_D), lambda i, sv: (i, 0))],
        out_specs=pl.BlockSpec(memory_space=pl.ANY),
        scratch_shapes=[pltpu.VMEM((MOE_TM, HALF_D), jnp.uint32), pltpu.SemaphoreType.DMA(())],
    )
    return pl.pallas_call(
        _dispatch_kernel,
        grid_spec=grid_spec,
        out_shape=jax.ShapeDtypeStruct((n_slots, HALF_D), jnp.uint32),
        compiler_params=_cparams(("arbitrary",)),
        name="dispatch",
    )(sb_valid, dest3, h)


def _stream_weights(se_ref, nx_ref, slot, m, w_hbm, wst_ref, sem):
    j = pl.program_id(0)
    nj = pl.num_programs(0)
    tn = wst_ref.shape[2]

    def tile_copy(e, jj, s):
        return pltpu.make_async_copy(w_hbm.at[e, :, pl.ds(pl.multiple_of(jj * tn, tn), tn)], wst_ref.at[s], sem.at[s])

    @pl.when((j == 0) & (m == 0))
    def _():
        tile_copy(se_ref[0], 0, slot).start()

    prev = se_ref[jnp.maximum(m - 1, 0)]

    @pl.when((m == 0) | (se_ref[m] != prev))
    def _():
        tile_copy(se_ref[m], j, slot).wait()
        nxt = nx_ref[m]

        @pl.when(nxt >= 0)
        def _():
            tile_copy(nxt, j, 1 - slot).start()

        @pl.when((nxt < 0) & (j + 1 < nj))
        def _():
            tile_copy(se_ref[0], j + 1, 1 - slot).start()


def _tile_slot(od_ref, nu_ref, m):
    return (pl.program_id(0) * nu_ref[1] + od_ref[m]) % 2


def _for_row_count(nsub, o_ref, compute):
    @pl.when(nsub == 0)
    def _():
        o_ref[...] = jnp.zeros(o_ref.shape, o_ref.dtype)

    for s in range(1, NSUB + 1):
        @pl.when(nsub == s)
        def _(s=s):
            rows = s * SUB
            compute(rows)
            if rows < o_ref.shape[0]:
                o_ref[rows:, :] = jnp.zeros((o_ref.shape[0] - rows, o_ref.shape[1]), o_ref.dtype)


def _expert_a_kernel(se_ref, ns_ref, nx_ref, od_ref, nu_ref, x_ref, w_hbm, b_ref, o_ref, wst_ref, sem):
    m = pl.program_id(1)
    tn = wst_ref.shape[2]
    nsub = ns_ref[m]
    slot = _tile_slot(od_ref, nu_ref, m)

    @pl.when(nsub > 0)
    def _():
        _stream_weights(se_ref, nx_ref, slot, m, w_hbm, wst_ref, sem)

    def compute(rows):
        lo, hi = _unpack_bf16_pair(x_ref[0:rows, :])
        x = jnp.concatenate([lo.astype(BF16), hi.astype(BF16)], axis=1)
        half = LANES // 2
        lane = lax.broadcasted_iota(jnp.int32, (rows, LANES), 1)
        low = lane < half
        idx_a = jnp.where(low, 2 * lane, 2 * (lane - half) + 1)
        idx_b = jnp.where(low, 2 * lane + 1, 2 * (lane - half))
        gu = jnp.dot(x, wst_ref[slot].astype(BF16), preferred_element_type=F32) + b_ref[0]
        for c in range(tn // (2 * LANES)):
            pa = jnp.take_along_axis(gu[:, 2 * c * LANES:(2 * c + 1) * LANES], idx_a, axis=1)
            pb = jnp.take_along_axis(gu[:, (2 * c + 1) * LANES:(2 * c + 2) * LANES], idx_b, axis=1)
            g = jnp.where(low, pa, pb)
            u = pltpu.roll(jnp.where(low, pb, pa), half, 1)
            gate = jnp.minimum(g, SWIGLU_LIMIT)
            up1 = jnp.clip(u, -SWIGLU_LIMIT, SWIGLU_LIMIT) + 1.0
            act = gate * jax.nn.sigmoid(SWIGLU_ALPHA * gate) * up1
            o_ref[0:rows, c * LANES:(c + 1) * LANES] = act.astype(BF16)

    _for_row_count(nsub, o_ref, compute)


def _expert_b_kernel(se_ref, ns_ref, nx_ref, od_ref, nu_ref, a_ref, w_hbm, b_ref, o_ref, wst_ref, sem):
    m = pl.program_id(1)
    nsub = ns_ref[m]
    slot = _tile_slot(od_ref, nu_ref, m)

    @pl.when(nsub > 0)
    def _():
        _stream_weights(se_ref, nx_ref, slot, m, w_hbm, wst_ref, sem)

    def compute(rows):
        y = jnp.dot(a_ref[0:rows, :], wst_ref[slot].astype(BF16), preferred_element_type=F32) + b_ref[0]
        half = y.shape[1] // 2
        o_ref[0:rows, :] = _pack_bf16_pair(y[:, :half], y[:, half:])

    _for_row_count(nsub, o_ref, compute)


def _expert_call(kernel, name, sb_e, sb_nsub, sb_next_e, sb_ord, n_used, xin, w, b3, out_cols, out_dtype, out_tn):
    n_sb = sb_e.shape[0]
    tm, tn = MOE_TM, MOE_TN
    kdim = w.shape[1]
    nj = w.shape[2] // tn

    def blk(m, nu):
        return jnp.maximum(jnp.minimum(m, nu[0] - 1), 0)

    grid_spec = pltpu.PrefetchScalarGridSpec(
        num_scalar_prefetch=5,
        grid=(nj, n_sb),
        in_specs=[
            pl.BlockSpec((tm, xin.shape[1]), lambda j, m, se, ns, nx, od, nu: (blk(m, nu), 0)),
            pl.BlockSpec(memory_space=pl.ANY),
            pl.BlockSpec((1, 1, tn), lambda j, m, se, ns, nx, od, nu: (se[blk(m, nu)], 0, j)),
        ],
        out_specs=pl.BlockSpec((tm, out_tn), lambda j, m, se, ns, nx, od, nu: (m, j)),
        scratch_shapes=[pltpu.VMEM((2, kdim, tn), F32), pltpu.SemaphoreType.DMA((2,))],
    )
    return pl.pallas_call(
        kernel,
        grid_spec=grid_spec,
        out_shape=jax.ShapeDtypeStruct((n_sb * tm, out_cols), out_dtype),
        compiler_params=_cparams(("arbitrary", "arbitrary")),
        name=name,
    )(sb_e, sb_nsub, sb_next_e, sb_ord, n_used, xin, w, b3)


def _combine_kernel(dest_ref, nxt_ref, y_hbm, x1_ref, route_ref, o_ref, buf2_ref, sem):
    i = pl.program_id(0)
    n = pl.num_programs(0)
    tm = o_ref.shape[0]

    def gather(idx_ref, slot):
        def issue(r, c):
            for kk in range(TOP_K):
                _row_copy(y_hbm, idx_ref[0, kk, r], buf2_ref.at[slot, kk], r,
                          sem.at[slot]).start(priority=kk % 2)
            return c
        lax.fori_loop(0, tm, issue, 0, unroll=4)

    @pl.when(i == 0)
    def _():
        gather(dest_ref, 0)

    @pl.when(i + 1 < n)
    def _():
        gather(nxt_ref, (i + 1) % 2)

    slot = i % 2
    buf_ref = buf2_ref.at[slot]
    for kk in range(TOP_K):
        pltpu.make_async_copy(y_hbm.at[pl.ds(0, tm), :], buf_ref.at[kk], sem.at[slot]).wait()
    route = route_ref[...]
    hw = MOE_TN // 2
    for j in range(D_MODEL // MOE_TN):
        lo_acc = x1_ref[:, j * MOE_TN:j * MOE_TN + hw]
        hi_acc = x1_ref[:, j * MOE_TN + hw:(j + 1) * MOE_TN]
        for kk in range(TOP_K):
            gate = route[:, TOP_K + kk:TOP_K + kk + 1]
            lo, hi = _unpack_bf16_pair(buf_ref[kk, :, j * hw:(j + 1) * hw])
            lo_acc = lo_acc + gate * lo
            hi_acc = hi_acc + gate * hi
        o_ref[:, j * MOE_TN:j * MOE_TN + hw] = lo_acc
        o_ref[:, j * MOE_TN + hw:(j + 1) * MOE_TN] = hi_acc


def _combine(dest3, y, x1, route):
    T = x1.shape[0]
    tm = CMB_TM
    nblk = T // tm
    row = lambda i: (i, 0)
    return pl.pallas_call(
        _combine_kernel,
        grid=(nblk,),
        in_specs=[pl.BlockSpec((1, TOP_K, tm), lambda i: (i, 0, 0), memory_space=pltpu.SMEM),
                  pl.BlockSpec((1, TOP_K, tm), lambda i: (jnp.minimum(i + 1, nblk - 1), 0, 0),
                               memory_space=pltpu.SMEM),
                  pl.BlockSpec(memory_space=pl.ANY),
                  pl.BlockSpec((tm, D_MODEL), row),
                  pl.BlockSpec((tm, LANES), row)],
        out_specs=pl.BlockSpec((tm, D_MODEL), row),
        out_shape=jax.ShapeDtypeStruct((T, D_MODEL), F32),
        scratch_shapes=[pltpu.VMEM((2, TOP_K, tm, HALF_D), jnp.uint32), pltpu.SemaphoreType.DMA((2,))],
        compiler_params=_cparams(("arbitrary",)),
        name="combine",
    )(dest3, dest3, y, x1, route)


def _pad_lanes(v, width=LANES):
    return jnp.pad(v, [(0, 0)] * (v.ndim - 1) + [(0, width - v.shape[-1])])


def _swap_halves(v):
    half = v.shape[-1] // 2
    return jnp.concatenate([v[..., half:], v[..., :half]], axis=-1)


def _token_mixers(x2, B, S, attn_norm_w, w_in, q_a_norm_w, w_uq, kv_a_norm_w, w_ukv, mla_q_norm_w, mla_k_norm_w,
                  diff_q_norm_w, diff_k_norm_w, lambda_q1, lambda_k1, lambda_q2, lambda_k2, diff_subln_w, rel_bias):
    L = 0
    win = w_in[L]
    sp = np.cumsum([MLA_Q_RANK, MLA_KV_RANK, MLA_ROPE])
    w_kr = win[:, sp[1]:sp[2]]
    win_r = jnp.concatenate(
        [win[:, :sp[1]], _pad_lanes(w_kr), _pad_lanes(_swap_halves(w_kr)), win[:, sp[2]:]], axis=1).astype(BF16)
    qk = MLA_NOPE + MLA_ROPE
    wuq3 = w_uq[L].reshape(MLA_Q_RANK, MLA_HEADS, qk)
    wuq_rope = wuq3[:, :, MLA_NOPE:]
    wuq_r = jnp.concatenate([
        wuq3[:, :, :MLA_NOPE].reshape(MLA_Q_RANK, -1),
        _pad_lanes(wuq_rope).reshape(MLA_Q_RANK, -1),
        _pad_lanes(_swap_halves(wuq_rope)).reshape(MLA_Q_RANK, -1)], axis=1).astype(BF16)
    wukv_b = w_ukv[L].astype(BF16)
    qn_w, kn_w = mla_q_norm_w[L], mla_k_norm_w[L]
    vecs = jnp.stack([
        qn_w[:MLA_NOPE], _pad_lanes(qn_w[MLA_NOPE:]), _pad_lanes(_swap_halves(qn_w[MLA_NOPE:])),
        kn_w[:MLA_NOPE], _pad_lanes(kn_w[MLA_NOPE:]), _pad_lanes(_swap_halves(kn_w[MLA_NOPE:])),
        jnp.tile(diff_q_norm_w[L], 2), jnp.tile(diff_k_norm_w[L], 2)])
    lamv = _pad_lanes(jnp.stack([lambda_q1[L], lambda_k1[L], lambda_q2[L], lambda_k2[L]]))

    inv_freq = np.float32(ROPE_THETA) ** (-np.arange(0, MLA_ROPE, 2, dtype=np.float32) / np.float32(MLA_ROPE))
    ang = np.arange(S, dtype=np.float32)[:, None] * inv_freq[None, :].astype(np.float32)
    cos, sin = np.cos(ang).astype(np.float32), np.sin(ang).astype(np.float32)
    zpad = np.zeros((S, LANES - MLA_ROPE), np.float32)
    cos128 = jnp.asarray(np.concatenate([cos, cos, zpad], axis=1))
    sin128 = jnp.asarray(np.concatenate([-sin, sin, zpad], axis=1))

    qm, km, vm, qd, kd, vd = _prep(x2, attn_norm_w[L][None], win_r, q_a_norm_w[L][None], wuq_r,
                                   kv_a_norm_w[L][None], wukv_b, vecs, cos128, sin128, S)
    nbias = _relbias(rel_bias, ATT_TQ, ATT_TK)

    amax = lambda v: jnp.max(jnp.abs(v))
    bound_mla = qk * qk ** -0.5 * LOG2E * amax(qn_w) * amax(kn_w)
    bound_diff = (DIFF_DH * DIFF_DH ** -0.5 * LOG2E * amax(diff_q_norm_w[L]) * amax(diff_k_norm_w[L])
                  + LOG2E * amax(rel_bias - rel_bias[REL_BUCKETS // 2 - 1][None, :]))
    bounded = jnp.maximum(bound_mla, bound_diff) * SCORE_SLACK <= SCORE_LIMIT

    def mixers(is_bounded):
        def run(qm, km, vm, qd, kd, vd, nbias, lamv, subw):
            om = _attention(qm, km, vm, B, S, MLA_HEADS, 2 * LANES, diff=False, bounded=is_bounded)
            od = _attention(qd, kd, vd, B, S, DIFF_HEADS, LANES, diff=True, bounded=is_bounded,
                            extra=(nbias, lamv, subw))
            return om, od
        return run

    return lax.cond(bounded, mixers(True), mixers(False),
                    qm, km, vm, qd, kd, vd, nbias, lamv, diff_subln_w[L][None])


def _moe_block(om, od, x2, w_o, ffn_norm_w, router_w, router_b, w_gate_up, b_gate_up, w_down, b_down):
    L = 0
    T, D = x2.shape
    wo_b = w_o[L].astype(BF16)
    rw_pad = _pad_lanes(router_w[L])
    rb_pad = _pad_lanes(router_b[L][None, :])

    x1, h, route, route_t, cnt = _router(om, od, x2, wo_b, ffn_norm_w[L][None], rw_pad, rb_pad)

    tm = MOE_TM
    n_sb = -(-T * TOP_K // tm) + N_EXPERTS
    idx = route_t[0:TOP_K].astype(jnp.int32)
    rank = route_t[2 * TOP_K:3 * TOP_K].astype(jnp.int32)
    counts = cnt[0, :N_EXPERTS].astype(jnp.int32)
    padded = (counts + tm - 1) // tm * tm
    pad_ends = jnp.cumsum(padded)
    pad_starts = pad_ends - padded
    dest = pad_starts[idx] + rank

    def dest_blocks(rows):
        return dest.reshape(TOP_K, T // rows, rows).transpose(1, 0, 2)
    sb_start = jnp.arange(n_sb, dtype=jnp.int32) * tm
    sb_e = jnp.minimum(jnp.sum((pad_ends[None, :] <= sb_start[:, None]).astype(jnp.int32), axis=1), N_EXPERTS - 1)
    sb_valid = jnp.clip(pad_starts[sb_e] + counts[sb_e] - sb_start, 0, tm)
    sb_valid = jnp.where(sb_start < pad_ends[-1], sb_valid, 0)
    sb_nsub = (sb_valid + SUB - 1) // SUB
    owns = (counts > 0).astype(jnp.int32)
    n_used = jnp.stack([pad_ends[-1] // tm, jnp.sum(owns)]).astype(jnp.int32)
    nxt_sb = pad_ends[sb_e] // tm
    sb_next_e = jnp.where(nxt_sb < n_used[0], sb_e[jnp.minimum(nxt_sb, n_sb - 1)], -1).astype(jnp.int32)
    sb_ord = (jnp.cumsum(owns) - 1)[sb_e].astype(jnp.int32)

    xs = _dispatch(sb_valid.astype(jnp.int32), dest_blocks(DSP_TM), h, n_sb * tm)
    act = _expert_call(_expert_a_kernel, "expert_a", sb_e, sb_nsub, sb_next_e, sb_ord, n_used, xs, w_gate_up[L],
                       b_gate_up[L][:, None, :], D_FF, BF16, MOE_TN // 2)
    y = _expert_call(_expert_b_kernel, "expert_b", sb_e, sb_nsub, sb_next_e, sb_ord, n_used, act, w_down[L],
                     b_down[L][:, None, :], D // 2, jnp.uint32, MOE_TN // 2)
    return _combine(dest_blocks(CMB_TM), y, x1, route)


def kernel(x, attn_norm_w, w_in, q_a_norm_w, w_uq, kv_a_norm_w, w_ukv, mla_q_norm_w, mla_k_norm_w, diff_q_norm_w, diff_k_norm_w, lambda_q1, lambda_k1, lambda_q2, lambda_k2, diff_subln_w, w_o, ffn_norm_w, router_w, router_b, w_gate_up, b_gate_up, w_down, b_down, rel_bias):
    B, S, D = x.shape
    x2 = x.reshape(B * S, D)
    om, od = _token_mixers(x2, B, S, attn_norm_w, w_in, q_a_norm_w, w_uq, kv_a_norm_w, w_ukv, mla_q_norm_w,
                           mla_k_norm_w, diff_q_norm_w, diff_k_norm_w, lambda_q1, lambda_k1, lambda_q2, lambda_k2,
                           diff_subln_w, rel_bias)
    out = _moe_block(om, od, x2, w_o, ffn_norm_w, router_w, router_b, w_gate_up, b_gate_up, w_down, b_down)
    return out.reshape(B, S, D)
```

```python
import functools
import math

import jax
import jax.numpy as jnp
import numpy as np
from jax import lax
from jax.experimental import pallas as pl
from jax.experimental.pallas import tpu as pltpu

D_MODEL = 2048
CHUNK = 64
MLA_HEADS = 8
MLA_NOPE = 128
MLA_ROPE = 64
MLA_V = 128
MLA_Q_RANK = 384
MLA_KV_RANK = 256
ROPE_THETA = 10000.0
DIFF_HEADS = 8
DIFF_DH = 64
DIFF_V = 128
REL_BUCKETS = 32
REL_MAX_DIST = 128
N_EXPERTS = 32
TOP_K = 4
D_FF = 2048
SWIGLU_LIMIT = 7.0
SWIGLU_ALPHA = 1.702
EPS = 1e-6
LAMBDA_INIT = 0.8 - 0.6 * math.exp(-0.3 * 0)

LANES = 128
VMEM_LIMIT = 48 * 1024 * 1024

PREP_TM = 256
ATT_TQ = 512
ATT_TK = 512
ATT_HP = 4
RT_TM = 512
SUB = 256
MOE_TM = 1024
NSUB = MOE_TM // SUB
MOE_TN = 1024
CMB_TM = 128
HALF_D = D_MODEL // 2
DSP_TM = 256
ROUTE_ROWS = 16

LOG2E = math.log2(math.e)
NEG = -1e30
SCORE_LIMIT = 80.0
SCORE_SLACK = 1.05
F32 = jnp.float32
BF16 = jnp.bfloat16

_C_CQ = 0
_C_CKV = MLA_Q_RANK
_C_KR = _C_CKV + MLA_KV_RANK
_C_KRR = _C_KR + LANES
_C_MLA_END = _C_KRR + LANES
_C_DQ = _C_MLA_END
_C_DK = _C_DQ + DIFF_HEADS * 2 * DIFF_DH
_C_DV = _C_DK + DIFF_HEADS * 2 * DIFF_DH
_C_END = _C_DV + DIFF_HEADS * DIFF_V


def _cparams(sem):
    return pltpu.CompilerParams(dimension_semantics=sem, vmem_limit_bytes=VMEM_LIMIT)


def _const_spec(shape):
    nd = len(shape)
    return pl.BlockSpec(shape, lambda *_: (0,) * nd, pipeline_mode=pl.Buffered(1))


def _pack_bf16_pair(lo, hi):
    lo_b = lax.bitcast_convert_type(lo.astype(BF16).astype(F32), jnp.uint32)
    hi_b = lax.bitcast_convert_type(hi.astype(BF16).astype(F32), jnp.uint32)
    return (lo_b >> 16) | (hi_b & jnp.uint32(0xFFFF0000))


def _unpack_bf16_pair(w):
    return (lax.bitcast_convert_type(w << 16, F32),
            lax.bitcast_convert_type(w & jnp.uint32(0xFFFF0000), F32))


def _prep_kernel(x_ref, anw_ref, win_ref, qaw_ref, wuq_ref, kvaw_ref, wukv_ref, vec_ref, cos_ref, sin_ref,
                 qm_ref, km_ref, vm_ref, qd_ref, kd_ref, vd_ref):
    xf = x_ref[...]
    inv = lax.rsqrt(jnp.mean(xf * xf, axis=-1, keepdims=True) + EPS)
    h = (xf * inv * anw_ref[...]).astype(BF16)

    def rms(v, w):
        return v * lax.rsqrt(jnp.mean(v * v, axis=-1, keepdims=True) + EPS) * w

    pm = jnp.dot(h, win_ref[:, _C_CQ:_C_MLA_END], preferred_element_type=F32)
    cqn = rms(pm[:, _C_CQ:_C_CKV], qaw_ref[...]).astype(BF16)
    ckn = rms(pm[:, _C_CKV:_C_KR], kvaw_ref[...]).astype(BF16)
    kr = pm[:, _C_KR:_C_KRR]
    krr = pm[:, _C_KRR:_C_MLA_END]
    qall = jnp.dot(cqn, wuq_ref[...], preferred_element_type=F32)
    kvall = jnp.dot(ckn, wukv_ref[...], preferred_element_type=F32)

    vec = vec_ref[...]
    wqn, wqr, wqrr = vec[0:1], vec[1:2], vec[2:3]
    wkn, wkr, wkrr = vec[3:4], vec[4:5], vec[5:6]
    wdq, wdk = vec[6:7], vec[7:8]
    cos = cos_ref[...]
    sin = sin_ref[...]
    cq_r, sq_r = wqr * cos, wqrr * sin
    ck_r, sk_r = wkr * cos, wkrr * sin
    kr2 = kr * kr
    k_rope_base = kr * ck_r + krr * sk_r
    q_scale = (MLA_NOPE + MLA_ROPE) ** -0.5 * LOG2E
    inv_qk = 1.0 / (MLA_NOPE + MLA_ROPE)
    nq = MLA_HEADS * LANES
    for hh in range(MLA_HEADS):
        a = hh * LANES
        qn = qall[:, a:a + LANES]
        qr = qall[:, nq + a:nq + a + LANES]
        qrr = qall[:, 2 * nq + a:2 * nq + a + LANES]
        iq = lax.rsqrt(jnp.sum(qn * qn + qr * qr, axis=-1, keepdims=True) * inv_qk + EPS) * q_scale
        qm_ref[:, 2 * a:2 * a + LANES] = (qn * iq * wqn).astype(BF16)
        qm_ref[:, 2 * a + LANES:2 * a + 2 * LANES] = ((qr * cq_r + qrr * sq_r) * iq).astype(BF16)
        kn = kvall[:, 2 * a:2 * a + LANES]
        ik = lax.rsqrt(jnp.sum(kn * kn + kr2, axis=-1, keepdims=True) * inv_qk + EPS)
        km_ref[:, 2 * a:2 * a + LANES] = (kn * ik * wkn).astype(BF16)
        km_ref[:, 2 * a + LANES:2 * a + 2 * LANES] = (k_rope_base * ik).astype(BF16)
        vm_ref[:, a:a + LANES] = kvall[:, 2 * a + LANES:2 * a + 2 * LANES].astype(BF16)

    dq = jnp.dot(h, win_ref[:, _C_DQ:_C_DK], preferred_element_type=F32)
    dk = jnp.dot(h, win_ref[:, _C_DK:_C_DV], preferred_element_type=F32)
    vd_ref[...] = jnp.dot(h, win_ref[:, _C_DV:_C_END], preferred_element_type=F32).astype(BF16)
    lane = lax.broadcasted_iota(jnp.int32, (xf.shape[0], LANES), 1)
    lo = lane < DIFF_DH
    d_scale = DIFF_DH ** -0.5 * LOG2E
    inv_dh = 1.0 / DIFF_DH

    def half_norm(v):
        sq = v * v
        s1 = jnp.sum(jnp.where(lo, sq, 0.0), axis=-1, keepdims=True)
        s2 = jnp.sum(jnp.where(lo, 0.0, sq), axis=-1, keepdims=True)
        return v * jnp.where(lo, lax.rsqrt(s1 * inv_dh + EPS), lax.rsqrt(s2 * inv_dh + EPS))

    for hh in range(DIFF_HEADS):
        a = hh * LANES
        qn = half_norm(dq[:, a:a + LANES]) * (wdq * d_scale)
        qd_ref[:, 2 * a:2 * a + LANES] = jnp.where(lo, qn, 0.0).astype(BF16)
        qd_ref[:, 2 * a + LANES:2 * a + 2 * LANES] = jnp.where(lo, 0.0, qn).astype(BF16)
        kd_ref[:, a:a + LANES] = (half_norm(dk[:, a:a + LANES]) * wdk).astype(BF16)


def _prep(x2, anw, win_r, qaw, wuq_r, kvaw, wukv_b, vecs, cos128, sin128, seq):
    T = x2.shape[0]
    tm = PREP_TM
    nseq = seq // tm
    row = lambda i: (i, 0)
    outs = [
        jax.ShapeDtypeStruct((T, MLA_HEADS * 2 * LANES), BF16),
        jax.ShapeDtypeStruct((T, MLA_HEADS * 2 * LANES), BF16),
        jax.ShapeDtypeStruct((T, MLA_HEADS * MLA_V), BF16),
        jax.ShapeDtypeStruct((T, DIFF_HEADS * 2 * LANES), BF16),
        jax.ShapeDtypeStruct((T, DIFF_HEADS * LANES), BF16),
        jax.ShapeDtypeStruct((T, DIFF_HEADS * DIFF_V), BF16),
    ]
    return pl.pallas_call(
        _prep_kernel,
        grid=(T // tm,),
        in_specs=[
            pl.BlockSpec((tm, D_MODEL), row),
            _const_spec(anw.shape), _const_spec(win_r.shape), _const_spec(qaw.shape), _const_spec(wuq_r.shape),
            _const_spec(kvaw.shape), _const_spec(wukv_b.shape), _const_spec(vecs.shape),
            pl.BlockSpec((tm, LANES), lambda i: (i % nseq, 0)),
            pl.BlockSpec((tm, LANES), lambda i: (i % nseq, 0)),
        ],
        out_specs=[pl.BlockSpec((tm, o.shape[1]), row) for o in outs],
        out_shape=outs,
        compiler_params=_cparams(("parallel",)),
        name="prep",
    )(x2, anw, win_r, qaw, wuq_r, kvaw, wukv_b, vecs, cos128, sin128)


def _relbias_kernel(rb_ref, o_ref, *, tq, tk):
    hh = pl.program_id(0)
    which = pl.program_id(1)
    row = lax.broadcasted_iota(jnp.int32, (tq, tk), 0)
    col = lax.broadcasted_iota(jnp.int32, (tq, tk), 1)
    rel = col - which * tk - row
    nb = REL_BUCKETS // 2
    max_exact = nb // 2
    ret = jnp.where(rel > 0, nb, 0)
    n = jnp.abs(rel)
    nf = jnp.maximum(n, 1).astype(F32)
    large = max_exact + (jnp.log(nf / max_exact) / math.log(REL_MAX_DIST / max_exact)
                         * (nb - max_exact)).astype(jnp.int32)
    large = jnp.minimum(large, nb - 1)
    bucket = ret + jnp.where(n < max_exact, n, large)
    bias = jnp.zeros((tq, tk), F32)
    for b in range(REL_BUCKETS):
        bias = jnp.where(bucket == b, rb_ref[b, hh], bias)
    bias = (bias - rb_ref[nb - 1, hh]) * LOG2E
    allowed = ((col // CHUNK) <= (row // CHUNK)) | (which > 0)
    o_ref[0, 0] = jnp.where(allowed, bias, NEG)


def _relbias(rel_bias, tq, tk):
    return pl.pallas_call(
        functools.partial(_relbias_kernel, tq=tq, tk=tk),
        grid=(DIFF_HEADS, 2),
        in_specs=[pl.BlockSpec(memory_space=pltpu.SMEM)],
        out_specs=pl.BlockSpec((1, 1, tq, tk), lambda h, w: (h, w, 0, 0)),
        out_shape=jax.ShapeDtypeStruct((DIFF_HEADS, 2, tq, tk), F32),
        compiler_params=_cparams(("parallel", "parallel")),
        name="relbias",
    )(rel_bias)


def _attn_kernel(*refs, diff, tq, tk, bounded):
    if diff:
        q_ref, k_ref, v_ref, nb_ref, lam_ref, sub_ref, o_ref, m_ref, acc_ref = refs
    else:
        q_ref, k_ref, v_ref, o_ref, m_ref, acc_ref = refs
    i = pl.program_id(2)
    hp = o_ref.shape[1] // LANES
    dk = k_ref.shape[1] // hp
    qs = []
    for a in range(hp):
        blk = q_ref[:, a * 2 * LANES:(a + 1) * 2 * LANES]
        qs.append(jnp.concatenate([blk[:, :LANES], blk[:, LANES:]], axis=0) if diff else blk)
    m_ref[...] = jnp.full(m_ref.shape, NEG, F32)
    acc_ref[...] = jnp.zeros(acc_ref.shape, F32)
    ones = jnp.ones((tk, LANES), BF16)

    def step(j, biases):
        start = pl.multiple_of(j * tk, tk)
        for a in range(hp):
            ks = k_ref[pl.ds(start, tk), a * dk:(a + 1) * dk]
            vs = jnp.concatenate([v_ref[pl.ds(start, tk), a * LANES:(a + 1) * LANES], ones], axis=1)
            s = lax.dot_general(qs[a], ks, (((1,), (1,)), ((), ())), preferred_element_type=F32)
            if biases is not None:
                s = s + biases[a]
            if bounded:
                acc_ref[a] += jnp.dot(jnp.exp2(s.astype(BF16)), vs, preferred_element_type=F32)
                continue
            m_prev = m_ref[a]
            m_new = jnp.maximum(m_prev, jnp.max(s, axis=1, keepdims=True))
            alpha = jnp.exp2(m_prev - m_new)
            p = jnp.exp2((s - jnp.concatenate([m_new] * (tk // LANES), axis=1)).astype(BF16))
            acc_ref[a] = (jnp.concatenate([alpha, alpha], axis=1) * acc_ref[a]
                          + jnp.dot(p, vs, preferred_element_type=F32))
            m_ref[a] = m_new

    def far_tiles(n):
        def pair(j2, c):
            step(2 * j2, None)
            step(2 * j2 + 1, None)
            return c

        lax.fori_loop(0, n // 2, pair, 0)

        @pl.when(n % 2 == 1)
        def _():
            step(n - 1, None)

    far_tiles(jnp.maximum(i - 1, 0))

    def last_tiles(with_prev):
        if diff:
            if with_prev:
                step(i - 1, [jnp.concatenate([nb_ref[a, 1]] * 2, axis=0) for a in range(hp)])
            step(i, [jnp.concatenate([nb_ref[a, 0]] * 2, axis=0) for a in range(hp)])
        else:
            if with_prev:
                step(i - 1, None)
            row = lax.broadcasted_iota(jnp.int32, (tq, tk), 0)
            col = lax.broadcasted_iota(jnp.int32, (tq, tk), 1)
            step(i, [jnp.where((col // CHUNK) <= (row // CHUNK), 0.0, NEG)] * hp)

    @pl.when(i > 0)
    def _():
        last_tiles(True)

    @pl.when(i == 0)
    def _():
        last_tiles(False)

    for a in range(hp):
        o = acc_ref[a, :, 0:LANES] / acc_ref[a, :, LANES:2 * LANES]
        if diff:
            lv = lam_ref[...]
            lam = (jnp.exp(jnp.sum(lv[0:1] * lv[1:2], axis=1, keepdims=True))
                   - jnp.exp(jnp.sum(lv[2:3] * lv[3:4], axis=1, keepdims=True)) + LAMBDA_INIT)
            od = o[:tq] - lam * o[tq:]
            od = od * lax.rsqrt(jnp.mean(od * od, axis=-1, keepdims=True) + EPS) * sub_ref[...]
            o_ref[:, a * LANES:(a + 1) * LANES] = (od * (1.0 - LAMBDA_INIT)).astype(BF16)
        else:
            o_ref[:, a * LANES:(a + 1) * LANES] = o.astype(BF16)


def _attention(q, k, v, batch, seq, heads, dk, diff, bounded, extra=()):
    tq, tk, hp = ATT_TQ, ATT_TK, ATT_HP
    assert tq == tk and tk >= REL_MAX_DIST and seq % tq == 0 and heads % hp == 0
    nq = seq // tq
    rows = 2 * tq if diff else tq
    in_specs = [
        pl.BlockSpec((tq, hp * 2 * LANES), lambda b, h, i: (b * nq + i, h)),
        pl.BlockSpec((seq, hp * dk), lambda b, h, i: (b, h), pipeline_mode=pl.Buffered(1)),
        pl.BlockSpec((seq, hp * LANES), lambda b, h, i: (b, h), pipeline_mode=pl.Buffered(1)),
    ]
    if diff:
        nbias, lamv, subw = extra
        in_specs += [
            pl.BlockSpec((hp, 2, tq, tk), lambda b, h, i: (h, 0, 0, 0), pipeline_mode=pl.Buffered(1)),
            pl.BlockSpec(lamv.shape, lambda b, h, i: (0, 0)),
            pl.BlockSpec(subw.shape, lambda b, h, i: (0, 0)),
        ]
    return pl.pallas_call(
        functools.partial(_attn_kernel, diff=diff, tq=tq, tk=tk, bounded=bounded),
        grid=(batch, heads // hp, nq),
        in_specs=in_specs,
        out_specs=pl.BlockSpec((tq, hp * LANES), lambda b, h, i: (b * nq + i, h)),
        out_shape=jax.ShapeDtypeStruct((batch * seq, heads * LANES), BF16),
        scratch_shapes=[pltpu.VMEM((hp, rows, LANES), F32),
                        pltpu.VMEM((hp, rows, 2 * LANES), F32)],
        compiler_params=_cparams(("parallel", "parallel", "arbitrary")),
        name=("attn_diff" if diff else "attn_mla") + ("_bounded" if bounded else ""),
    )(q, k, v, *extra)


def _router_kernel(om_ref, od_ref, x_ref, wo_ref, fw_ref, rw_ref, rb_ref,
                   x1_ref, h_ref, route_ref, rt_ref, cnt_ref, carry_ref):
    tm = x_ref.shape[0]
    half = om_ref.shape[1]

    @pl.when(pl.program_id(0) == 0)
    def _():
        carry_ref[...] = jnp.zeros(carry_ref.shape, F32)

    y = (jnp.dot(om_ref[...], wo_ref[0:half, :], preferred_element_type=F32)
         + jnp.dot(od_ref[...], wo_ref[half:2 * half, :], preferred_element_type=F32))
    x1 = x_ref[...] + y
    x1_ref[...] = x1
    hh = x1 * lax.rsqrt(jnp.mean(x1 * x1, axis=-1, keepdims=True) + EPS) * fw_ref[...]
    h_ref[...] = _pack_bf16_pair(hh[:, :HALF_D], hh[:, HALF_D:])
    rw = rw_ref[...]
    rw_hi = rw.astype(BF16)
    rw_lo = (rw - rw_hi.astype(F32)).astype(BF16)
    hh_hi = hh.astype(BF16)
    hh_lo = (hh - hh_hi.astype(F32)).astype(BF16)
    logits = (jnp.dot(hh_hi, rw_hi, preferred_element_type=F32) + jnp.dot(hh_lo, rw_hi, preferred_element_type=F32)
              + jnp.dot(hh_hi, rw_lo, preferred_element_type=F32)) + rb_ref[...]
    lane = lax.broadcasted_iota(jnp.int32, (tm, LANES), 1).astype(F32)
    work = jnp.where(lane < N_EXPERTS, logits, -jnp.inf)
    vals, idxs = [], []
    for _ in range(TOP_K):
        mx = jnp.max(work, axis=-1, keepdims=True)
        ix = jnp.min(jnp.where(work == mx, lane, float(LANES)), axis=-1, keepdims=True)
        vals.append(mx)
        idxs.append(ix)
        work = jnp.where(lane == ix, -jnp.inf, work)
    es = [jnp.exp(v - vals[0]) for v in vals]
    den = es[0] + es[1] + es[2] + es[3]
    onehot = jnp.zeros((tm, LANES), F32)
    for ix in idxs:
        onehot = onehot + jnp.where(lane == ix, 1.0, 0.0)
    r_i = lax.broadcasted_iota(jnp.int32, (tm, tm), 0)
    c_i = lax.broadcasted_iota(jnp.int32, (tm, tm), 1)
    tri = jnp.where(c_i < r_i, 1.0, 0.0).astype(BF16)
    prefix = jnp.dot(tri, onehot.astype(BF16), preferred_element_type=F32) + carry_ref[...]
    route = jnp.zeros((tm, LANES), F32)
    for kk in range(TOP_K):
        rank = jnp.sum(jnp.where(lane == idxs[kk], prefix, 0.0), axis=-1, keepdims=True)
        route = jnp.where(lane == kk, idxs[kk], route)
        route = jnp.where(lane == TOP_K + kk, es[kk] / den, route)
        route = jnp.where(lane == 2 * TOP_K + kk, rank, route)
    route_ref[...] = route
    rt_ref[...] = route.T[0:rt_ref.shape[0], :]
    carry = carry_ref[...] + jnp.sum(onehot, axis=0, keepdims=True)
    carry_ref[...] = carry
    cnt_ref[...] = carry


def _router(om, od, x2, wo_b, fw, rw_pad, rb_pad):
    T = x2.shape[0]
    tm = RT_TM
    row = lambda i: (i, 0)
    outs = [
        jax.ShapeDtypeStruct((T, D_MODEL), F32),
        jax.ShapeDtypeStruct((T, HALF_D), jnp.uint32),
        jax.ShapeDtypeStruct((T, LANES), F32),
        jax.ShapeDtypeStruct((ROUTE_ROWS, T), F32),
        jax.ShapeDtypeStruct((1, LANES), F32),
    ]
    return pl.pallas_call(
        _router_kernel,
        grid=(T // tm,),
        in_specs=[
            pl.BlockSpec((tm, om.shape[1]), row), pl.BlockSpec((tm, od.shape[1]), row),
            pl.BlockSpec((tm, D_MODEL), row),
            _const_spec(wo_b.shape), _const_spec(fw.shape), _const_spec(rw_pad.shape), _const_spec(rb_pad.shape),
        ],
        out_specs=[pl.BlockSpec((tm, D_MODEL), row), pl.BlockSpec((tm, HALF_D), row),
                   pl.BlockSpec((tm, LANES), row), pl.BlockSpec((ROUTE_ROWS, tm), lambda i: (0, i)),
                   pl.BlockSpec((1, LANES), lambda i: (0, 0))],
        out_shape=outs,
        scratch_shapes=[pltpu.VMEM((1, LANES), F32)],
        compiler_params=_cparams(("arbitrary",)),
        name="router",
    )(om, od, x2, wo_b, fw, rw_pad, rb_pad)


def _row_copy(src_ref, src_row, dst_ref, dst_row, sem):
    return pltpu.make_async_copy(src_ref.at[pl.ds(src_row, 1), :], dst_ref.at[pl.ds(dst_row, 1), :], sem)


def _dispatch_kernel(sv_ref, dest_ref, h_ref, xs_hbm, zero_ref, sem):
    tm = h_ref.shape[0]

    @pl.when(pl.program_id(0) == 0)
    def _():
        zero_ref[...] = jnp.zeros(zero_ref.shape, zero_ref.dtype)

        def zero_copy(sb):
            return pltpu.make_async_copy(zero_ref, xs_hbm.at[pl.ds(sb * MOE_TM, MOE_TM), :], sem)

        def start(sb, c):
            @pl.when(sv_ref[sb] < MOE_TM)
            def _():
                zero_copy(sb).start()
            return c

        def wait(sb, c):
            @pl.when(sv_ref[sb] < MOE_TM)
            def _():
                zero_copy(sb).wait()
            return c

        lax.fori_loop(0, sv_ref.shape[0], start, 0)
        lax.fori_loop(0, sv_ref.shape[0], wait, 0)

    def issue(r, c):
        for kk in range(TOP_K):
            _row_copy(h_ref, r, xs_hbm, dest_ref[0, 0, kk * tm + r], sem).start(priority=kk % 2)
        return c

    lax.fori_loop(0, tm, issue, 0, unroll=4)
    for kk in range(TOP_K):
        pltpu.make_async_copy(h_ref, xs_hbm.at[pl.ds(0, tm), :], sem).wait()


def _dispatch(sb_valid, dest3, h, n_slots):
    T = h.shape[0]
    tm = DSP_TM
    grid_spec = pltpu.PrefetchScalarGridSpec(
        num_scalar_prefetch=1,
        grid=(T // tm,),
        in_specs=[pl.BlockSpec((1, 1, TOP_K * tm), lambda i, sv: (i, 0, 0), memory_space=pltpu.SMEM),
                  pl.BlockSpec((tm, HALF_D), lambda i, sv: (i, 0))],
        out_specs=pl.BlockSpec(memory_space=pl.ANY),
        scratch_shapes=[pltpu.VMEM((MOE_TM, HALF_D), jnp.uint32), pltpu.SemaphoreType.DMA(())],
    )
    return pl.pallas_call(
        _dispatch_kernel,
        grid_spec=grid_spec,
        out_shape=jax.ShapeDtypeStruct((n_slots, HALF_D), jnp.uint32),
        compiler_params=_cparams(("arbitrary",)),
        name="dispatch",
    )(sb_valid, dest3, h)


def _stream_weights(se_ref, nx_ref, slot, m, w_hbm, wst_ref, sem):
    j = pl.program_id(0)
    nj = pl.num_programs(0)
    tn = wst_ref.shape[2]

    def tile_copy(e, jj, s):
        return pltpu.make_async_copy(w_hbm.at[e, :, pl.ds(pl.multiple_of(jj * tn, tn), tn)], wst_ref.at[s], sem.at[s])

    @pl.when((j == 0) & (m == 0))
    def _():
        tile_copy(se_ref[0], 0, slot).start()

    prev = se_ref[jnp.maximum(m - 1, 0)]

    @pl.when((m == 0) | (se_ref[m] != prev))
    def _():
        tile_copy(se_ref[m], j, slot).wait()
        nxt = nx_ref[m]

        @pl.when(nxt >= 0)
        def _():
            tile_copy(nxt, j, 1 - slot).start()

        @pl.when((nxt < 0) & (j + 1 < nj))
        def _():
            tile_copy(se_ref[0], j + 1, 1 - slot).start()


def _tile_slot(od_ref, nu_ref, m):
    return (pl.program_id(0) * nu_ref[1] + od_ref[m]) % 2


def _for_row_count(nsub, o_ref, compute):
    @pl.when(nsub == 0)
    def _():
        o_ref[...] = jnp.zeros(o_ref.shape, o_ref.dtype)

    for s in range(1, NSUB + 1):
        @pl.when(nsub == s)
        def _(s=s):
            rows = s * SUB
            compute(rows)
            if rows < o_ref.shape[0]:
                o_ref[rows:, :] = jnp.zeros((o_ref.shape[0] - rows, o_ref.shape[1]), o_ref.dtype)


def _expert_a_kernel(se_ref, ns_ref, nx_ref, od_ref, nu_ref, x_ref, w_hbm, b_ref, o_ref, wst_ref, sem):
    m = pl.program_id(1)
    tn = wst_ref.shape[2]
    nsub = ns_ref[m]
    slot = _tile_slot(od_ref, nu_ref, m)

    @pl.when(nsub > 0)
    def _():
        _stream_weights(se_ref, nx_ref, slot, m, w_hbm, wst_ref, sem)

    def compute(rows):
        lo, hi = _unpack_bf16_pair(x_ref[0:rows, :])
        x = jnp.concatenate([lo.astype(BF16), hi.astype(BF16)], axis=1)
        half = LANES // 2
        lane = lax.broadcasted_iota(jnp.int32, (rows, LANES), 1)
        low = lane < half
        idx_a = jnp.where(low, 2 * lane, 2 * (lane - half) + 1)
        idx_b = jnp.where(low, 2 * lane + 1, 2 * (lane - half))
        gu = jnp.dot(x, wst_ref[slot].astype(BF16), preferred_element_type=F32) + b_ref[0]
        for c in range(tn // (2 * LANES)):
            pa = jnp.take_along_axis(gu[:, 2 * c * LANES:(2 * c + 1) * LANES], idx_a, axis=1)
            pb = jnp.take_along_axis(gu[:, (2 * c + 1) * LANES:(2 * c + 2) * LANES], idx_b, axis=1)
            g = jnp.where(low, pa, pb)
            u = pltpu.roll(jnp.where(low, pb, pa), half, 1)
            gate = jnp.minimum(g, SWIGLU_LIMIT)
            up1 = jnp.clip(u, -SWIGLU_LIMIT, SWIGLU_LIMIT) + 1.0
            act = gate * jax.nn.sigmoid(SWIGLU_ALPHA * gate) * up1
            o_ref[0:rows, c * LANES:(c + 1) * LANES] = act.astype(BF16)

    _for_row_count(nsub, o_ref, compute)


def _expert_b_kernel(se_ref, ns_ref, nx_ref, od_ref, nu_ref, a_ref, w_hbm, b_ref, o_ref, wst_ref, sem):
    m = pl.program_id(1)
    nsub = ns_ref[m]
    slot = _tile_slot(od_ref, nu_ref, m)

    @pl.when(nsub > 0)
    def _():
        _stream_weights(se_ref, nx_ref, slot, m, w_hbm, wst_ref, sem)

    def compute(rows):
        y = jnp.dot(a_ref[0:rows, :], wst_ref[slot].astype(BF16), preferred_element_type=F32) + b_ref[0]
        half = y.shape[1] // 2
        o_ref[0:rows, :] = _pack_bf16_pair(y[:, :half], y[:, half:])

    _for_row_count(nsub, o_ref, compute)


def _expert_call(kernel, name, sb_e, sb_nsub, sb_next_e, sb_ord, n_used, xin, w, b3, out_cols, out_dtype, out_tn):
    n_sb = sb_e.shape[0]
    tm, tn = MOE_TM, MOE_TN
    kdim = w.shape[1]
    nj = w.shape[2] // tn

    def blk(m, nu):
        return jnp.maximum(jnp.minimum(m, nu[0] - 1), 0)

    grid_spec = pltpu.PrefetchScalarGridSpec(
        num_scalar_prefetch=5,
        grid=(nj, n_sb),
        in_specs=[
            pl.BlockSpec((tm, xin.shape[1]), lambda j, m, se, ns, nx, od, nu: (blk(m, nu), 0)),
            pl.BlockSpec(memory_space=pl.ANY),
            pl.BlockSpec((1, 1, tn), lambda j, m, se, ns, nx, od, nu: (se[blk(m, nu)], 0, j)),
        ],
        out_specs=pl.BlockSpec((tm, out_tn), lambda j, m, se, ns, nx, od, nu: (m, j)),
        scratch_shapes=[pltpu.VMEM((2, kdim, tn), F32), pltpu.SemaphoreType.DMA((2,))],
    )
    return pl.pallas_call(
        kernel,
        grid_spec=grid_spec,
        out_shape=jax.ShapeDtypeStruct((n_sb * tm, out_cols), out_dtype),
        compiler_params=_cparams(("arbitrary", "arbitrary")),
        name=name,
    )(sb_e, sb_nsub, sb_next_e, sb_ord, n_used, xin, w, b3)


def _combine_kernel(dest_ref, nxt_ref, y_hbm, x1_ref, route_ref, o_ref, buf2_ref, sem):
    i = pl.program_id(0)
    n = pl.num_programs(0)
    tm = o_ref.shape[0]

    def gather(idx_ref, slot):
        def issue(r, c):
            for kk in range(TOP_K):
                _row_copy(y_hbm, idx_ref[0, 0, kk * tm + r], buf2_ref.at[slot, kk], r,
                          sem.at[slot]).start(priority=kk % 2)
            return c
        lax.fori_loop(0, tm, issue, 0, unroll=4)

    @pl.when(i == 0)
    def _():
        gather(dest_ref, 0)

    @pl.when(i + 1 < n)
    def _():
        gather(nxt_ref, (i + 1) % 2)

    slot = i % 2
    buf_ref = buf2_ref.at[slot]
    for kk in range(TOP_K):
        pltpu.make_async_copy(y_hbm.at[pl.ds(0, tm), :], buf_ref.at[kk], sem.at[slot]).wait()
    route = route_ref[...]
    hw = MOE_TN // 2
    for j in range(D_MODEL // MOE_TN):
        lo_acc = x1_ref[:, j * MOE_TN:j * MOE_TN + hw]
        hi_acc = x1_ref[:, j * MOE_TN + hw:(j + 1) * MOE_TN]
        for kk in range(TOP_K):
            gate = route[:, TOP_K + kk:TOP_K + kk + 1]
            lo, hi = _unpack_bf16_pair(buf_ref[kk, :, j * hw:(j + 1) * hw])
            lo_acc = lo_acc + gate * lo
            hi_acc = hi_acc + gate * hi
        o_ref[:, j * MOE_TN:j * MOE_TN + hw] = lo_acc
        o_ref[:, j * MOE_TN + hw:(j + 1) * MOE_TN] = hi_acc


def _combine(dest3, y, x1, route):
    T = x1.shape[0]
    tm = CMB_TM
    nblk = T // tm
    row = lambda i: (i, 0)
    return pl.pallas_call(
        _combine_kernel,
        grid=(nblk,),
        in_specs=[pl.BlockSpec((1, 1, TOP_K * tm), lambda i: (i, 0, 0), memory_space=pltpu.SMEM),
                  pl.BlockSpec((1, 1, TOP_K * tm), lambda i: (jnp.minimum(i + 1, nblk - 1), 0, 0),
                               memory_space=pltpu.SMEM),
                  pl.BlockSpec(memory_space=pl.ANY),
                  pl.BlockSpec((tm, D_MODEL), row),
                  pl.BlockSpec((tm, LANES), row)],
        out_specs=pl.BlockSpec((tm, D_MODEL), row),
        out_shape=jax.ShapeDtypeStruct((T, D_MODEL), F32),
        scratch_shapes=[pltpu.VMEM((2, TOP_K, tm, HALF_D), jnp.uint32), pltpu.SemaphoreType.DMA((2,))],
        compiler_params=_cparams(("arbitrary",)),
        name="combine",
    )(dest3, dest3, y, x1, route)


def _pad_lanes(v, width=LANES):
    return jnp.pad(v, [(0, 0)] * (v.ndim - 1) + [(0, width - v.shape[-1])])


def _swap_halves(v):
    half = v.shape[-1] // 2
    return jnp.concatenate([v[..., half:], v[..., :half]], axis=-1)


def _token_mixers(x2, B, S, attn_norm_w, w_in, q_a_norm_w, w_uq, kv_a_norm_w, w_ukv, mla_q_norm_w, mla_k_norm_w,
                  diff_q_norm_w, diff_k_norm_w, lambda_q1, lambda_k1, lambda_q2, lambda_k2, diff_subln_w, rel_bias):
    L = 0
    win = w_in[L]
    sp = np.cumsum([MLA_Q_RANK, MLA_KV_RANK, MLA_ROPE])
    w_kr = win[:, sp[1]:sp[2]]
    win_r = jnp.concatenate(
        [win[:, :sp[1]], _pad_lanes(w_kr), _pad_lanes(_swap_halves(w_kr)), win[:, sp[2]:]], axis=1).astype(BF16)
    qk = MLA_NOPE + MLA_ROPE
    wuq3 = w_uq[L].reshape(MLA_Q_RANK, MLA_HEADS, qk)
    wuq_rope = wuq3[:, :, MLA_NOPE:]
    wuq_r = jnp.concatenate([
        wuq3[:, :, :MLA_NOPE].reshape(MLA_Q_RANK, -1),
        _pad_lanes(wuq_rope).reshape(MLA_Q_RANK, -1),
        _pad_lanes(_swap_halves(wuq_rope)).reshape(MLA_Q_RANK, -1)], axis=1).astype(BF16)
    wukv_b = w_ukv[L].astype(BF16)
    qn_w, kn_w = mla_q_norm_w[L], mla_k_norm_w[L]
    vecs = jnp.stack([
        qn_w[:MLA_NOPE], _pad_lanes(qn_w[MLA_NOPE:]), _pad_lanes(_swap_halves(qn_w[MLA_NOPE:])),
        kn_w[:MLA_NOPE], _pad_lanes(kn_w[MLA_NOPE:]), _pad_lanes(_swap_halves(kn_w[MLA_NOPE:])),
        jnp.tile(diff_q_norm_w[L], 2), jnp.tile(diff_k_norm_w[L], 2)])
    lamv = _pad_lanes(jnp.stack([lambda_q1[L], lambda_k1[L], lambda_q2[L], lambda_k2[L]]))

    inv_freq = np.float32(ROPE_THETA) ** (-np.arange(0, MLA_ROPE, 2, dtype=np.float32) / np.float32(MLA_ROPE))
    ang = np.arange(S, dtype=np.float32)[:, None] * inv_freq[None, :].astype(np.float32)
    cos, sin = np.cos(ang).astype(np.float32), np.sin(ang).astype(np.float32)
    zpad = np.zeros((S, LANES - MLA_ROPE), np.float32)
    cos128 = jnp.asarray(np.concatenate([cos, cos, zpad], axis=1))
    sin128 = jnp.asarray(np.concatenate([-sin, sin, zpad], axis=1))

    qm, km, vm, qd, kd, vd = _prep(x2, attn_norm_w[L][None], win_r, q_a_norm_w[L][None], wuq_r,
                                   kv_a_norm_w[L][None], wukv_b, vecs, cos128, sin128, S)
    nbias = _relbias(rel_bias, ATT_TQ, ATT_TK)

    amax = lambda v: jnp.max(jnp.abs(v))
    bound_mla = qk * qk ** -0.5 * LOG2E * amax(qn_w) * amax(kn_w)
    bound_diff = (DIFF_DH * DIFF_DH ** -0.5 * LOG2E * amax(diff_q_norm_w[L]) * amax(diff_k_norm_w[L])
                  + LOG2E * amax(rel_bias - rel_bias[REL_BUCKETS // 2 - 1][None, :]))
    bounded = jnp.maximum(bound_mla, bound_diff) * SCORE_SLACK <= SCORE_LIMIT

    def mixers(is_bounded):
        def run(qm, km, vm, qd, kd, vd, nbias, lamv, subw):
            om = _attention(qm, km, vm, B, S, MLA_HEADS, 2 * LANES, diff=False, bounded=is_bounded)
            od = _attention(qd, kd, vd, B, S, DIFF_HEADS, LANES, diff=True, bounded=is_bounded,
                            extra=(nbias, lamv, subw))
            return om, od
        return run

    return lax.cond(bounded, mixers(True), mixers(False),
                    qm, km, vm, qd, kd, vd, nbias, lamv, diff_subln_w[L][None])


def _moe_block(om, od, x2, w_o, ffn_norm_w, router_w, router_b, w_gate_up, b_gate_up, w_down, b_down):
    L = 0
    T, D = x2.shape
    wo_b = w_o[L].astype(BF16)
    rw_pad = _pad_lanes(router_w[L])
    rb_pad = _pad_lanes(router_b[L][None, :])

    x1, h, route, route_t, cnt = _router(om, od, x2, wo_b, ffn_norm_w[L][None], rw_pad, rb_pad)

    tm = MOE_TM
    n_sb = -(-T * TOP_K // tm) + N_EXPERTS
    idx = route_t[0:TOP_K].astype(jnp.int32)
    rank = route_t[2 * TOP_K:3 * TOP_K].astype(jnp.int32)
    counts = cnt[0, :N_EXPERTS].astype(jnp.int32)
    padded = (counts + tm - 1) // tm * tm
    pad_ends = jnp.cumsum(padded)
    pad_starts = pad_ends - padded
    dest = rank
    for e in range(N_EXPERTS):
        dest = dest + jnp.where(idx == e, pad_starts[e], 0)

    def dest_blocks(rows):
        return dest.reshape(TOP_K, T // rows, rows).transpose(1, 0, 2).reshape(T // rows, 1, TOP_K * rows)
    sb_start = jnp.arange(n_sb, dtype=jnp.int32) * tm
    sb_e = jnp.minimum(jnp.sum((pad_ends[None, :] <= sb_start[:, None]).astype(jnp.int32), axis=1), N_EXPERTS - 1)
    sb_valid = jnp.clip(pad_starts[sb_e] + counts[sb_e] - sb_start, 0, tm)
    sb_valid = jnp.where(sb_start < pad_ends[-1], sb_valid, 0)
    sb_nsub = (sb_valid + SUB - 1) // SUB
    owns = (counts > 0).astype(jnp.int32)
    n_used = jnp.stack([pad_ends[-1] // tm, jnp.sum(owns)]).astype(jnp.int32)
    nxt_sb = pad_ends[sb_e] // tm
    sb_next_e = jnp.where(nxt_sb < n_used[0], sb_e[jnp.minimum(nxt_sb, n_sb - 1)], -1).astype(jnp.int32)
    sb_ord = (jnp.cumsum(owns) - 1)[sb_e].astype(jnp.int32)

    xs = _dispatch(sb_valid.astype(jnp.int32), dest_blocks(DSP_TM), h, n_sb * tm)
    act = _expert_call(_expert_a_kernel, "expert_a", sb_e, sb_nsub, sb_next_e, sb_ord, n_used, xs, w_gate_up[L],
                       b_gate_up[L][:, None, :], D_FF, BF16, MOE_TN // 2)
    y = _expert_call(_expert_b_kernel, "expert_b", sb_e, sb_nsub, sb_next_e, sb_ord, n_used, act, w_down[L],
                     b_down[L][:, None, :], D // 2, jnp.uint32, MOE_TN // 2)
    return _combine(dest_blocks(CMB_TM), y, x1, route)


def kernel(x, attn_norm_w, w_in, q_a_norm_w, w_uq, kv_a_norm_w, w_ukv, mla_q_norm_w, mla_k_norm_w, diff_q_norm_w, diff_k_norm_w, lambda_q1, lambda_k1, lambda_q2, lambda_k2, diff_subln_w, w_o, ffn_norm_w, router_w, router_b, w_gate_up, b_gate_up, w_down, b_down, rel_bias):
    B, S, D = x.shape
    x2 = x.reshape(B * S, D)
    om, od = _token_mixers(x2, B, S, attn_norm_w, w_in, q_a_norm_w, w_uq, kv_a_norm_w, w_ukv, mla_q_norm_w,
                           mla_k_norm_w, diff_q_norm_w, diff_k_norm_w, lambda_q1, lambda_k1, lambda_q2, lambda_k2,
                           diff_subln_w, rel_bias)
    out = _moe_block(om, od, x2, w_o, ffn_norm_w, router_w, router_b, w_gate_up, b_gate_up, w_down, b_down)
    return out.reshape(B, S, D)
```

```python
import functools
import math

import jax
import jax.numpy as jnp
import numpy as np
from jax import lax
from jax.experimental import pallas as pl
from jax.experimental.pallas import tpu as pltpu

D_MODEL = 2048
CHUNK = 64
MLA_HEADS = 8
MLA_NOPE = 128
MLA_ROPE = 64
MLA_V = 128
MLA_Q_RANK = 384
MLA_KV_RANK = 256
ROPE_THETA = 10000.0
DIFF_HEADS = 8
DIFF_DH = 64
DIFF_V = 128
REL_BUCKETS = 32
REL_MAX_DIST = 128
N_EXPERTS = 32
TOP_K = 4
D_FF = 2048
SWIGLU_LIMIT = 7.0
SWIGLU_ALPHA = 1.702
EPS = 1e-6
LAMBDA_INIT = 0.8 - 0.6 * math.exp(-0.3 * 0)

LANES = 128
VMEM_LIMIT = 48 * 1024 * 1024

PREP_TM = 256
ATT_TQ = 512
ATT_TK = 512
ATT_HP = 4
RT_TM = 512
SUB = 256
MOE_TM = 1024
NSUB = MOE_TM // SUB
MOE_TN = 1024
CMB_TM = 128
HALF_D = D_MODEL // 2
DSP_TM = 256
ROUTE_ROWS = 16

LOG2E = math.log2(math.e)
NEG = -1e30
SCORE_LIMIT = 80.0
SCORE_SLACK = 1.05
F32 = jnp.float32
BF16 = jnp.bfloat16

_C_CQ = 0
_C_CKV = MLA_Q_RANK
_C_KR = _C_CKV + MLA_KV_RANK
_C_MLA_END = _C_KR + LANES
_C_DQ = _C_MLA_END
_C_DK = _C_DQ + DIFF_HEADS * 2 * DIFF_DH
_C_DV = _C_DK + DIFF_HEADS * 2 * DIFF_DH
_C_END = _C_DV + DIFF_HEADS * DIFF_V


def _cparams(sem):
    return pltpu.CompilerParams(dimension_semantics=sem, vmem_limit_bytes=VMEM_LIMIT)


def _const_spec(shape):
    nd = len(shape)
    return pl.BlockSpec(shape, lambda *_: (0,) * nd, pipeline_mode=pl.Buffered(1))


def _pack_bf16_pair(lo, hi):
    lo_b = lax.bitcast_convert_type(lo.astype(BF16).astype(F32), jnp.uint32)
    hi_b = lax.bitcast_convert_type(hi.astype(BF16).astype(F32), jnp.uint32)
    return (lo_b >> 16) | (hi_b & jnp.uint32(0xFFFF0000))


def _unpack_bf16_pair(w):
    return (lax.bitcast_convert_type(w << 16, F32),
            lax.bitcast_convert_type(w & jnp.uint32(0xFFFF0000), F32))


def _prep_kernel(x_ref, anw_ref, win_ref, qaw_ref, wuq_ref, kvaw_ref, wukv_ref, vec_ref, cs_ref,
                 qm_ref, km_ref, vm_ref, qd_ref, kd_ref, vd_ref):
    xf = x_ref[...]
    inv = lax.rsqrt(jnp.mean(xf * xf, axis=-1, keepdims=True) + EPS)
    h = (xf * inv * anw_ref[...]).astype(BF16)

    def rms(v, w):
        return v * lax.rsqrt(jnp.mean(v * v, axis=-1, keepdims=True) + EPS) * w

    pm = jnp.dot(h, win_ref[:, _C_CQ:_C_MLA_END], preferred_element_type=F32)
    cqn = rms(pm[:, _C_CQ:_C_CKV], qaw_ref[...]).astype(BF16)
    ckn = rms(pm[:, _C_CKV:_C_KR], kvaw_ref[...]).astype(BF16)
    kp = pm[:, _C_KR:_C_MLA_END]
    qall = jnp.dot(cqn, wuq_ref[...], preferred_element_type=F32)
    kvall = jnp.dot(ckn, wukv_ref[...], preferred_element_type=F32)

    vec = vec_ref[...]
    wqn, wqp, wkn, wkp, wdq, wdk = (vec[r:r + 1] for r in range(6))
    lane = lax.broadcasted_iota(jnp.int32, (xf.shape[0], LANES), 1)
    lo = lane < LANES // 2

    def rope(p, table):
        m = p * table
        return jnp.where(lo, m + pltpu.roll(m, LANES // 2, 1), 0.0)

    cs = cs_ref[...]
    tq_r, tk_r = wqp * cs, wkp * cs
    kr2 = jnp.where(lo, kp * kp, 0.0)
    k_rope_base = rope(kp, tk_r)
    q_scale = (MLA_NOPE + MLA_ROPE) ** -0.5 * LOG2E
    inv_qk = 1.0 / (MLA_NOPE + MLA_ROPE)
    nq = MLA_HEADS * LANES
    for hh in range(MLA_HEADS):
        a = hh * LANES
        qn = qall[:, a:a + LANES]
        qp = qall[:, nq + a:nq + a + LANES]
        iq = lax.rsqrt(jnp.sum(qn * qn + jnp.where(lo, qp * qp, 0.0), axis=-1, keepdims=True) * inv_qk + EPS) * q_scale
        qm_ref[:, 2 * a:2 * a + LANES] = (qn * iq * wqn).astype(BF16)
        qm_ref[:, 2 * a + LANES:2 * a + 2 * LANES] = (rope(qp, tq_r) * iq).astype(BF16)
        kn = kvall[:, 2 * a:2 * a + LANES]
        ik = lax.rsqrt(jnp.sum(kn * kn + kr2, axis=-1, keepdims=True) * inv_qk + EPS)
        km_ref[:, 2 * a:2 * a + LANES] = (kn * ik * wkn).astype(BF16)
        km_ref[:, 2 * a + LANES:2 * a + 2 * LANES] = (k_rope_base * ik).astype(BF16)
        vm_ref[:, a:a + LANES] = kvall[:, 2 * a + LANES:2 * a + 2 * LANES].astype(BF16)

    dq = jnp.dot(h, win_ref[:, _C_DQ:_C_DK], preferred_element_type=F32)
    dk = jnp.dot(h, win_ref[:, _C_DK:_C_DV], preferred_element_type=F32)
    vd_ref[...] = jnp.dot(h, win_ref[:, _C_DV:_C_END], preferred_element_type=F32).astype(BF16)
    d_scale = DIFF_DH ** -0.5 * LOG2E
    inv_dh = 1.0 / DIFF_DH

    def half_norm(v):
        sq = v * v
        s1 = jnp.sum(jnp.where(lo, sq, 0.0), axis=-1, keepdims=True)
        s2 = jnp.sum(jnp.where(lo, 0.0, sq), axis=-1, keepdims=True)
        return v * jnp.where(lo, lax.rsqrt(s1 * inv_dh + EPS), lax.rsqrt(s2 * inv_dh + EPS))

    for hh in range(DIFF_HEADS):
        a = hh * LANES
        qn = half_norm(dq[:, a:a + LANES]) * (wdq * d_scale)
        qd_ref[:, 2 * a:2 * a + LANES] = jnp.where(lo, qn, 0.0).astype(BF16)
        qd_ref[:, 2 * a + LANES:2 * a + 2 * LANES] = jnp.where(lo, 0.0, qn).astype(BF16)
        kd_ref[:, a:a + LANES] = (half_norm(dk[:, a:a + LANES]) * wdk).astype(BF16)


def _prep(x2, anw, win_r, qaw, wuq_r, kvaw, wukv_b, vecs, cs128, seq):
    T = x2.shape[0]
    tm = PREP_TM
    nseq = seq // tm
    row = lambda i: (i, 0)
    outs = [
        jax.ShapeDtypeStruct((T, MLA_HEADS * 2 * LANES), BF16),
        jax.ShapeDtypeStruct((T, MLA_HEADS * 2 * LANES), BF16),
        jax.ShapeDtypeStruct((T, MLA_HEADS * MLA_V), BF16),
        jax.ShapeDtypeStruct((T, DIFF_HEADS * 2 * LANES), BF16),
        jax.ShapeDtypeStruct((T, DIFF_HEADS * LANES), BF16),
        jax.ShapeDtypeStruct((T, DIFF_HEADS * DIFF_V), BF16),
    ]
    return pl.pallas_call(
        _prep_kernel,
        grid=(T // tm,),
        in_specs=[
            pl.BlockSpec((tm, D_MODEL), row),
            _const_spec(anw.shape), _const_spec(win_r.shape), _const_spec(qaw.shape), _const_spec(wuq_r.shape),
            _const_spec(kvaw.shape), _const_spec(wukv_b.shape), _const_spec(vecs.shape),
            pl.BlockSpec((tm, LANES), lambda i: (i % nseq, 0)),
        ],
        out_specs=[pl.BlockSpec((tm, o.shape[1]), row) for o in outs],
        out_shape=outs,
        compiler_params=_cparams(("parallel",)),
        name="prep",
    )(x2, anw, win_r, qaw, wuq_r, kvaw, wukv_b, vecs, cs128)


def _relbias_kernel(rb_ref, o_ref, *, tq, tk):
    hh = pl.program_id(0)
    which = pl.program_id(1)
    row = lax.broadcasted_iota(jnp.int32, (tq, tk), 0)
    col = lax.broadcasted_iota(jnp.int32, (tq, tk), 1)
    rel = col - which * tk - row
    nb = REL_BUCKETS // 2
    max_exact = nb // 2
    ret = jnp.where(rel > 0, nb, 0)
    n = jnp.abs(rel)
    nf = jnp.maximum(n, 1).astype(F32)
    large = max_exact + (jnp.log(nf / max_exact) / math.log(REL_MAX_DIST / max_exact)
                         * (nb - max_exact)).astype(jnp.int32)
    large = jnp.minimum(large, nb - 1)
    bucket = ret + jnp.where(n < max_exact, n, large)
    bias = jnp.zeros((tq, tk), F32)
    for b in range(REL_BUCKETS):
        bias = jnp.where(bucket == b, rb_ref[b, hh], bias)
    bias = (bias - rb_ref[nb - 1, hh]) * LOG2E
    allowed = ((col // CHUNK) <= (row // CHUNK)) | (which > 0)
    o_ref[0, 0] = jnp.where(allowed, bias, NEG)


def _relbias(rel_bias, tq, tk):
    return pl.pallas_call(
        functools.partial(_relbias_kernel, tq=tq, tk=tk),
        grid=(DIFF_HEADS, 2),
        in_specs=[pl.BlockSpec(memory_space=pltpu.SMEM)],
        out_specs=pl.BlockSpec((1, 1, tq, tk), lambda h, w: (h, w, 0, 0)),
        out_shape=jax.ShapeDtypeStruct((DIFF_HEADS, 2, tq, tk), F32),
        compiler_params=_cparams(("parallel", "parallel")),
        name="relbias",
    )(rel_bias)


def _attn_kernel(*refs, diff, tq, tk, bounded):
    if diff:
        q_ref, k_ref, v_ref, nb_ref, lam_ref, sub_ref, o_ref, m_ref, acc_ref = refs
    else:
        q_ref, k_ref, v_ref, o_ref, m_ref, acc_ref = refs
    i = pl.program_id(2)
    hp = o_ref.shape[1] // LANES
    dk = k_ref.shape[1] // hp
    qs = []
    for a in range(hp):
        blk = q_ref[:, a * 2 * LANES:(a + 1) * 2 * LANES]
        qs.append(jnp.concatenate([blk[:, :LANES], blk[:, LANES:]], axis=0) if diff else blk)
    if not bounded:
        m_ref[...] = jnp.full(m_ref.shape, NEG, F32)
    acc_ref[...] = jnp.zeros(acc_ref.shape, F32)
    ones = jnp.ones((tk, LANES), BF16)

    def step(j, biases):
        start = pl.multiple_of(j * tk, tk)
        for a in range(hp):
            ks = k_ref[pl.ds(start, tk), a * dk:(a + 1) * dk]
            vs = jnp.concatenate([v_ref[pl.ds(start, tk), a * LANES:(a + 1) * LANES], ones], axis=1)
            s = lax.dot_general(qs[a], ks, (((1,), (1,)), ((), ())), preferred_element_type=F32)
            if biases is not None:
                s = s + biases[a]
            if bounded:
                acc_ref[a] += jnp.dot(jnp.exp2(s.astype(BF16)), vs, preferred_element_type=F32)
                continue
            m_prev = m_ref[a]
            m_new = jnp.maximum(m_prev, jnp.max(s, axis=1, keepdims=True))
            alpha = jnp.exp2(m_prev - m_new)
            p = jnp.exp2((s - jnp.concatenate([m_new] * (tk // LANES), axis=1)).astype(BF16))
            acc_ref[a] = (jnp.concatenate([alpha, alpha], axis=1) * acc_ref[a]
                          + jnp.dot(p, vs, preferred_element_type=F32))
            m_ref[a] = m_new

    def far_tiles(n):
        def pair(j2, c):
            step(2 * j2, None)
            step(2 * j2 + 1, None)
            return c

        lax.fori_loop(0, n // 2, pair, 0)

        @pl.when(n % 2 == 1)
        def _():
            step(n - 1, None)

    far_tiles(jnp.maximum(i - 1, 0))

    def last_tiles(with_prev):
        if diff:
            if with_prev:
                step(i - 1, [jnp.concatenate([nb_ref[a, 1]] * 2, axis=0) for a in range(hp)])
            step(i, [jnp.concatenate([nb_ref[a, 0]] * 2, axis=0) for a in range(hp)])
        else:
            if with_prev:
                step(i - 1, None)
            row = lax.broadcasted_iota(jnp.int32, (tq, tk), 0)
            col = lax.broadcasted_iota(jnp.int32, (tq, tk), 1)
            step(i, [jnp.where((col // CHUNK) <= (row // CHUNK), 0.0, NEG)] * hp)

    @pl.when(i > 0)
    def _():
        last_tiles(True)

    @pl.when(i == 0)
    def _():
        last_tiles(False)

    for a in range(hp):
        o = acc_ref[a, :, 0:LANES] / acc_ref[a, :, LANES:2 * LANES]
        if diff:
            lv = lam_ref[...]
            lam = (jnp.exp(jnp.sum(lv[0:1] * lv[1:2], axis=1, keepdims=True))
                   - jnp.exp(jnp.sum(lv[2:3] * lv[3:4], axis=1, keepdims=True)) + LAMBDA_INIT)
            od = o[:tq] - lam * o[tq:]
            od = od * lax.rsqrt(jnp.mean(od * od, axis=-1, keepdims=True) + EPS) * sub_ref[...]
            o_ref[:, a * LANES:(a + 1) * LANES] = (od * (1.0 - LAMBDA_INIT)).astype(BF16)
        else:
            o_ref[:, a * LANES:(a + 1) * LANES] = o.astype(BF16)


def _attention(q, k, v, batch, seq, heads, dk, diff, bounded, extra=()):
    tq, tk, hp = ATT_TQ, ATT_TK, ATT_HP
    assert tq == tk and tk >= REL_MAX_DIST and seq % tq == 0 and heads % hp == 0
    nq = seq // tq
    rows = 2 * tq if diff else tq
    in_specs = [
        pl.BlockSpec((tq, hp * 2 * LANES), lambda b, h, i: (b * nq + i, h)),
        pl.BlockSpec((seq, hp * dk), lambda b, h, i: (b, h), pipeline_mode=pl.Buffered(1)),
        pl.BlockSpec((seq, hp * LANES), lambda b, h, i: (b, h), pipeline_mode=pl.Buffered(1)),
    ]
    if diff:
        nbias, lamv, subw = extra
        in_specs += [
            pl.BlockSpec((hp, 2, tq, tk), lambda b, h, i: (h, 0, 0, 0), pipeline_mode=pl.Buffered(1)),
            pl.BlockSpec(lamv.shape, lambda b, h, i: (0, 0)),
            pl.BlockSpec(subw.shape, lambda b, h, i: (0, 0)),
        ]
    return pl.pallas_call(
        functools.partial(_attn_kernel, diff=diff, tq=tq, tk=tk, bounded=bounded),
        grid=(batch, heads // hp, nq),
        in_specs=in_specs,
        out_specs=pl.BlockSpec((tq, hp * LANES), lambda b, h, i: (b * nq + i, h)),
        out_shape=jax.ShapeDtypeStruct((batch * seq, heads * LANES), BF16),
        scratch_shapes=[pltpu.VMEM((hp, rows, LANES), F32),
                        pltpu.VMEM((hp, rows, 2 * LANES), F32)],
        compiler_params=_cparams(("parallel", "parallel", "arbitrary")),
        name=("attn_diff" if diff else "attn_mla") + ("_bounded" if bounded else ""),
    )(q, k, v, *extra)


def _router_kernel(om_ref, od_ref, x_ref, wo_ref, fw_ref, rw_ref, rb_ref,
                   x1_ref, h_ref, route_ref, rt_ref, cnt_ref, carry_ref):
    tm = x_ref.shape[0]
    half = om_ref.shape[1]

    @pl.when(pl.program_id(0) == 0)
    def _():
        carry_ref[...] = jnp.zeros(carry_ref.shape, F32)

    y = (jnp.dot(om_ref[...], wo_ref[0:half, :], preferred_element_type=F32)
         + jnp.dot(od_ref[...], wo_ref[half:2 * half, :], preferred_element_type=F32))
    x1 = x_ref[...] + y
    x1_ref[...] = x1
    hh = x1 * lax.rsqrt(jnp.mean(x1 * x1, axis=-1, keepdims=True) + EPS) * fw_ref[...]
    h_ref[...] = _pack_bf16_pair(hh[:, :HALF_D], hh[:, HALF_D:])
    rw = rw_ref[...]
    rw_hi = rw.astype(BF16)
    rw_lo = (rw - rw_hi.astype(F32)).astype(BF16)
    hh_hi = hh.astype(BF16)
    hh_lo = (hh - hh_hi.astype(F32)).astype(BF16)
    logits = (jnp.dot(hh_hi, rw_hi, preferred_element_type=F32) + jnp.dot(hh_lo, rw_hi, preferred_element_type=F32)
              + jnp.dot(hh_hi, rw_lo, preferred_element_type=F32)) + rb_ref[...]
    lane = lax.broadcasted_iota(jnp.int32, (tm, LANES), 1).astype(F32)
    work = jnp.where(lane < N_EXPERTS, logits, -jnp.inf)
    vals, idxs = [], []
    for _ in range(TOP_K):
        mx = jnp.max(work, axis=-1, keepdims=True)
        ix = jnp.min(jnp.where(work == mx, lane, float(LANES)), axis=-1, keepdims=True)
        vals.append(mx)
        idxs.append(ix)
        work = jnp.where(lane == ix, -jnp.inf, work)
    es = [jnp.exp(v - vals[0]) for v in vals]
    den = es[0] + es[1] + es[2] + es[3]
    onehot = jnp.zeros((tm, LANES), F32)
    for ix in idxs:
        onehot = onehot + jnp.where(lane == ix, 1.0, 0.0)
    r_i = lax.broadcasted_iota(jnp.int32, (tm, tm), 0)
    c_i = lax.broadcasted_iota(jnp.int32, (tm, tm), 1)
    tri = jnp.where(c_i < r_i, 1.0, 0.0).astype(BF16)
    prefix = jnp.dot(tri, onehot.astype(BF16), preferred_element_type=F32) + carry_ref[...]
    route = jnp.zeros((tm, LANES), F32)
    for kk in range(TOP_K):
        rank = jnp.sum(jnp.where(lane == idxs[kk], prefix, 0.0), axis=-1, keepdims=True)
        route = jnp.where(lane == kk, idxs[kk], route)
        route = jnp.where(lane == TOP_K + kk, es[kk] / den, route)
        route = jnp.where(lane == 2 * TOP_K + kk, rank, route)
    route_ref[...] = route
    rt_ref[...] = route.T[0:rt_ref.shape[0], :]
    carry = carry_ref[...] + jnp.sum(onehot, axis=0, keepdims=True)
    carry_ref[...] = carry
    cnt_ref[...] = carry


def _router(om, od, x2, wo_b, fw, rw_pad, rb_pad):
    T = x2.shape[0]
    tm = RT_TM
    row = lambda i: (i, 0)
    outs = [
        jax.ShapeDtypeStruct((T, D_MODEL), F32),
        jax.ShapeDtypeStruct((T, HALF_D), jnp.uint32),
        jax.ShapeDtypeStruct((T, LANES), F32),
        jax.ShapeDtypeStruct((ROUTE_ROWS, T), F32),
        jax.ShapeDtypeStruct((1, LANES), F32),
    ]
    return pl.pallas_call(
        _router_kernel,
        grid=(T // tm,),
        in_specs=[
            pl.BlockSpec((tm, om.shape[1]), row), pl.BlockSpec((tm, od.shape[1]), row),
            pl.BlockSpec((tm, D_MODEL), row),
            _const_spec(wo_b.shape), _const_spec(fw.shape), _const_spec(rw_pad.shape), _const_spec(rb_pad.shape),
        ],
        out_specs=[pl.BlockSpec((tm, D_MODEL), row), pl.BlockSpec((tm, HALF_D), row),
                   pl.BlockSpec((tm, LANES), row), pl.BlockSpec((ROUTE_ROWS, tm), lambda i: (0, i)),
                   pl.BlockSpec((1, LANES), lambda i: (0, 0))],
        out_shape=outs,
        scratch_shapes=[pltpu.VMEM((1, LANES), F32)],
        compiler_params=_cparams(("arbitrary",)),
        name="router",
    )(om, od, x2, wo_b, fw, rw_pad, rb_pad)


def _row_copy(src_ref, src_row, dst_ref, dst_row, sem):
    return pltpu.make_async_copy(src_ref.at[pl.ds(src_row, 1), :], dst_ref.at[pl.ds(dst_row, 1), :], sem)


def _dispatch_kernel(sv_ref, dest_ref, h_ref, xs_hbm, zero_ref, sem):
    tm = h_ref.shape[0]

    @pl.when(pl.program_id(0) == 0)
    def _():
        zero_ref[...] = jnp.zeros(zero_ref.shape, zero_ref.dtype)

        def zero_copy(sb):
            return pltpu.make_async_copy(zero_ref, xs_hbm.at[pl.ds(sb * MOE_TM, MOE_TM), :], sem)

        def start(sb, c):
            @pl.when(sv_ref[sb] < MOE_TM)
            def _():
                zero_copy(sb).start()
            return c

        def wait(sb, c):
            @pl.when(sv_ref[sb] < MOE_TM)
            def _():
                zero_copy(sb).wait()
            return c

        lax.fori_loop(0, sv_ref.shape[0], start, 0)
        lax.fori_loop(0, sv_ref.shape[0], wait, 0)

    def issue(r, c):
        for kk in range(TOP_K):
            _row_copy(h_ref, r, xs_hbm, dest_ref[0, 0, kk * tm + r], sem).start(priority=kk % 2)
        return c

    lax.fori_loop(0, tm, issue, 0, unroll=4)
    for kk in range(TOP_K):
        pltpu.make_async_copy(h_ref, xs_hbm.at[pl.ds(0, tm), :], sem).wait()


def _dispatch(sb_valid, dest3, h, n_slots):
    T = h.shape[0]
    tm = DSP_TM
    grid_spec = pltpu.PrefetchScalarGridSpec(
        num_scalar_prefetch=1,
        grid=(T // tm,),
        in_specs=[pl.BlockSpec((1, 1, TOP_K * tm), lambda i, sv: (i, 0, 0), memory_space=pltpu.SMEM),
                  pl.BlockSpec((tm, HALF_D), lambda i, sv: (i, 0))],
        out_specs=pl.BlockSpec(memory_space=pl.ANY),
        scratch_shapes=[pltpu.VMEM((MOE_TM, HALF_D), jnp.uint32), pltpu.SemaphoreType.DMA(())],
    )
    return pl.pallas_call(
        _dispatch_kernel,
        grid_spec=grid_spec,
        out_shape=jax.ShapeDtypeStruct((n_slots, HALF_D), jnp.uint32),
        compiler_params=_cparams(("arbitrary",)),
        name="dispatch",
    )(sb_valid, dest3, h)


def _stream_weights(se_ref, nx_ref, slot, m, w_hbm, wst_ref, sem):
    j = pl.program_id(0)
    nj = pl.num_programs(0)
    tn = wst_ref.shape[2]

    def tile_copy(e, jj, s):
        return pltpu.make_async_copy(w_hbm.at[e, :, pl.ds(pl.multiple_of(jj * tn, tn), tn)], wst_ref.at[s], sem.at[s])

    @pl.when((j == 0) & (m == 0))
    def _():
        tile_copy(se_ref[0], 0, slot).start()

    prev = se_ref[jnp.maximum(m - 1, 0)]

    @pl.when((m == 0) | (se_ref[m] != prev))
    def _():
        tile_copy(se_ref[m], j, slot).wait()
        nxt = nx_ref[m]

        @pl.when(nxt >= 0)
        def _():
            tile_copy(nxt, j, 1 - slot).start()

        @pl.when((nxt < 0) & (j + 1 < nj))
        def _():
            tile_copy(se_ref[0], j + 1, 1 - slot).start()


def _tile_slot(od_ref, nu_ref, m):
    return (pl.program_id(0) * nu_ref[1] + od_ref[m]) % 2


def _for_row_count(nsub, o_ref, compute):
    @pl.when(nsub == 0)
    def _():
        o_ref[...] = jnp.zeros(o_ref.shape, o_ref.dtype)

    for s in range(1, NSUB + 1):
        @pl.when(nsub == s)
        def _(s=s):
            rows = s * SUB
            compute(rows)
            if rows < o_ref.shape[0]:
                o_ref[rows:, :] = jnp.zeros((o_ref.shape[0] - rows, o_ref.shape[1]), o_ref.dtype)


def _expert_a_kernel(se_ref, ns_ref, nx_ref, od_ref, nu_ref, x_ref, w_hbm, b_ref, o_ref, wst_ref, sem):
    m = pl.program_id(1)
    tn = wst_ref.shape[2]
    nsub = ns_ref[m]
    slot = _tile_slot(od_ref, nu_ref, m)

    @pl.when(nsub > 0)
    def _():
        _stream_weights(se_ref, nx_ref, slot, m, w_hbm, wst_ref, sem)

    def compute(rows):
        lo, hi = _unpack_bf16_pair(x_ref[0:rows, :])
        x = jnp.concatenate([lo.astype(BF16), hi.astype(BF16)], axis=1)
        half = LANES // 2
        lane = lax.broadcasted_iota(jnp.int32, (rows, LANES), 1)
        low = lane < half
        idx_a = jnp.where(low, 2 * lane, 2 * (lane - half) + 1)
        idx_b = jnp.where(low, 2 * lane + 1, 2 * (lane - half))
        gu = jnp.dot(x, wst_ref[slot].astype(BF16), preferred_element_type=F32) + b_ref[0]
        for c in range(tn // (2 * LANES)):
            pa = jnp.take_along_axis(gu[:, 2 * c * LANES:(2 * c + 1) * LANES], idx_a, axis=1)
            pb = jnp.take_along_axis(gu[:, (2 * c + 1) * LANES:(2 * c + 2) * LANES], idx_b, axis=1)
            g = jnp.where(low, pa, pb)
            u = pltpu.roll(jnp.where(low, pb, pa), half, 1)
            gate = jnp.minimum(g, SWIGLU_LIMIT)
            up1 = jnp.clip(u, -SWIGLU_LIMIT, SWIGLU_LIMIT) + 1.0
            act = gate * jax.nn.sigmoid(SWIGLU_ALPHA * gate) * up1
            o_ref[0:rows, c * LANES:(c + 1) * LANES] = act.astype(BF16)

    _for_row_count(nsub, o_ref, compute)


def _expert_b_kernel(se_ref, ns_ref, nx_ref, od_ref, nu_ref, a_ref, w_hbm, b_ref, o_ref, wst_ref, sem):
    m = pl.program_id(1)
    nsub = ns_ref[m]
    slot = _tile_slot(od_ref, nu_ref, m)

    @pl.when(nsub > 0)
    def _():
        _stream_weights(se_ref, nx_ref, slot, m, w_hbm, wst_ref, sem)

    def compute(rows):
        y = jnp.dot(a_ref[0:rows, :], wst_ref[slot].astype(BF16), preferred_element_type=F32) + b_ref[0]
        half = y.shape[1] // 2
        o_ref[0:rows, :] = _pack_bf16_pair(y[:, :half], y[:, half:])

    _for_row_count(nsub, o_ref, compute)


def _expert_call(kernel, name, sb_e, sb_nsub, sb_next_e, sb_ord, n_used, xin, w, b3, out_cols, out_dtype, out_tn):
    n_sb = sb_e.shape[0]
    tm, tn = MOE_TM, MOE_TN
    kdim = w.shape[1]
    nj = w.shape[2] // tn

    def blk(m, nu):
        return jnp.maximum(jnp.minimum(m, nu[0] - 1), 0)

    grid_spec = pltpu.PrefetchScalarGridSpec(
        num_scalar_prefetch=5,
        grid=(nj, n_sb),
        in_specs=[
            pl.BlockSpec((tm, xin.shape[1]), lambda j, m, se, ns, nx, od, nu: (blk(m, nu), 0)),
            pl.BlockSpec(memory_space=pl.ANY),
            pl.BlockSpec((1, 1, tn), lambda j, m, se, ns, nx, od, nu: (se[blk(m, nu)], 0, j)),
        ],
        out_specs=pl.BlockSpec((tm, out_tn), lambda j, m, se, ns, nx, od, nu: (m, j)),
        scratch_shapes=[pltpu.VMEM((2, kdim, tn), F32), pltpu.SemaphoreType.DMA((2,))],
    )
    return pl.pallas_call(
        kernel,
        grid_spec=grid_spec,
        out_shape=jax.ShapeDtypeStruct((n_sb * tm, out_cols), out_dtype),
        compiler_params=_cparams(("arbitrary", "arbitrary")),
        name=name,
    )(sb_e, sb_nsub, sb_next_e, sb_ord, n_used, xin, w, b3)


def _combine_kernel(dest_ref, nxt_ref, y_hbm, x1_ref, route_ref, o_ref, buf2_ref, sem):
    i = pl.program_id(0)
    n = pl.num_programs(0)
    tm = o_ref.shape[0]

    def gather(idx_ref, slot):
        def issue(r, c):
            for kk in range(TOP_K):
                _row_copy(y_hbm, idx_ref[0, 0, kk * tm + r], buf2_ref.at[slot, kk], r,
                          sem.at[slot]).start(priority=kk % 2)
            return c
        lax.fori_loop(0, tm, issue, 0, unroll=4)

    @pl.when(i == 0)
    def _():
        gather(dest_ref, 0)

    @pl.when(i + 1 < n)
    def _():
        gather(nxt_ref, (i + 1) % 2)

    slot = i % 2
    buf_ref = buf2_ref.at[slot]
    for kk in range(TOP_K):
        pltpu.make_async_copy(y_hbm.at[pl.ds(0, tm), :], buf_ref.at[kk], sem.at[slot]).wait()
    route = route_ref[...]
    hw = MOE_TN // 2
    for j in range(D_MODEL // MOE_TN):
        lo_acc = x1_ref[:, j * MOE_TN:j * MOE_TN + hw]
        hi_acc = x1_ref[:, j * MOE_TN + hw:(j + 1) * MOE_TN]
        for kk in range(TOP_K):
            gate = route[:, TOP_K + kk:TOP_K + kk + 1]
            lo, hi = _unpack_bf16_pair(buf_ref[kk, :, j * hw:(j + 1) * hw])
            lo_acc = lo_acc + gate * lo
            hi_acc = hi_acc + gate * hi
        o_ref[:, j * MOE_TN:j * MOE_TN + hw] = lo_acc
        o_ref[:, j * MOE_TN + hw:(j + 1) * MOE_TN] = hi_acc


def _combine(dest3, y, x1, route):
    T = x1.shape[0]
    tm = CMB_TM
    nblk = T // tm
    row = lambda i: (i, 0)
    return pl.pallas_call(
        _combine_kernel,
        grid=(nblk,),
        in_specs=[pl.BlockSpec((1, 1, TOP_K * tm), lambda i: (i, 0, 0), memory_space=pltpu.SMEM),
                  pl.BlockSpec((1, 1, TOP_K * tm), lambda i: (jnp.minimum(i + 1, nblk - 1), 0, 0),
                               memory_space=pltpu.SMEM),
                  pl.BlockSpec(memory_space=pl.ANY),
                  pl.BlockSpec((tm, D_MODEL), row),
                  pl.BlockSpec((tm, LANES), row)],
        out_specs=pl.BlockSpec((tm, D_MODEL), row),
        out_shape=jax.ShapeDtypeStruct((T, D_MODEL), F32),
        scratch_shapes=[pltpu.VMEM((2, TOP_K, tm, HALF_D), jnp.uint32), pltpu.SemaphoreType.DMA((2,))],
        compiler_params=_cparams(("arbitrary",)),
        name="combine",
    )(dest3, dest3, y, x1, route)


def _pad_lanes(v, width=LANES):
    return jnp.pad(v, [(0, 0)] * (v.ndim - 1) + [(0, width - v.shape[-1])])


def _swap_halves(v):
    half = v.shape[-1] // 2
    return jnp.concatenate([v[..., half:], v[..., :half]], axis=-1)


def _token_mixers(x2, B, S, attn_norm_w, w_in, q_a_norm_w, w_uq, kv_a_norm_w, w_ukv, mla_q_norm_w, mla_k_norm_w,
                  diff_q_norm_w, diff_k_norm_w, lambda_q1, lambda_k1, lambda_q2, lambda_k2, diff_subln_w, rel_bias):
    L = 0
    win = w_in[L]
    sp = np.cumsum([MLA_Q_RANK, MLA_KV_RANK, MLA_ROPE])
    w_kr = win[:, sp[1]:sp[2]]
    with_swap = lambda v: jnp.concatenate([v, _swap_halves(v)], axis=-1)
    win_r = jnp.concatenate([win[:, :sp[1]], with_swap(w_kr), win[:, sp[2]:]], axis=1).astype(BF16)
    qk = MLA_NOPE + MLA_ROPE
    wuq3 = w_uq[L].reshape(MLA_Q_RANK, MLA_HEADS, qk)
    wuq_r = jnp.concatenate([
        wuq3[:, :, :MLA_NOPE].reshape(MLA_Q_RANK, -1),
        with_swap(wuq3[:, :, MLA_NOPE:]).reshape(MLA_Q_RANK, -1)], axis=1).astype(BF16)
    wukv_b = w_ukv[L].astype(BF16)
    qn_w, kn_w = mla_q_norm_w[L], mla_k_norm_w[L]
    zero_row = jnp.zeros((LANES,), F32)
    vecs = jnp.stack([
        qn_w[:MLA_NOPE], with_swap(qn_w[MLA_NOPE:]), kn_w[:MLA_NOPE], with_swap(kn_w[MLA_NOPE:]),
        jnp.tile(diff_q_norm_w[L], 2), jnp.tile(diff_k_norm_w[L], 2), zero_row, zero_row])
    lamv = _pad_lanes(jnp.stack([lambda_q1[L], lambda_k1[L], lambda_q2[L], lambda_k2[L]]))

    inv_freq = np.float32(ROPE_THETA) ** (-np.arange(0, MLA_ROPE, 2, dtype=np.float32) / np.float32(MLA_ROPE))
    ang = np.arange(S, dtype=np.float32)[:, None] * inv_freq[None, :].astype(np.float32)
    cos, sin = np.cos(ang).astype(np.float32), np.sin(ang).astype(np.float32)
    cs128 = jnp.asarray(np.concatenate([cos, cos, -sin, sin], axis=1))

    qm, km, vm, qd, kd, vd = _prep(x2, attn_norm_w[L][None], win_r, q_a_norm_w[L][None], wuq_r,
                                   kv_a_norm_w[L][None], wukv_b, vecs, cs128, S)
    nbias = _relbias(rel_bias, ATT_TQ, ATT_TK)

    amax = lambda v: jnp.max(jnp.abs(v))
    bound_mla = qk * qk ** -0.5 * LOG2E * amax(qn_w) * amax(kn_w)
    bound_diff = (DIFF_DH * DIFF_DH ** -0.5 * LOG2E * amax(diff_q_norm_w[L]) * amax(diff_k_norm_w[L])
                  + LOG2E * amax(rel_bias - rel_bias[REL_BUCKETS // 2 - 1][None, :]))
    bounded = jnp.maximum(bound_mla, bound_diff) * SCORE_SLACK <= SCORE_LIMIT

    def mixers(is_bounded):
        def run(qm, km, vm, qd, kd, vd, nbias, lamv, subw):
            om = _attention(qm, km, vm, B, S, MLA_HEADS, 2 * LANES, diff=False, bounded=is_bounded)
            od = _attention(qd, kd, vd, B, S, DIFF_HEADS, LANES, diff=True, bounded=is_bounded,
                            extra=(nbias, lamv, subw))
            return om, od
        return run

    return lax.cond(bounded, mixers(True), mixers(False),
                    qm, km, vm, qd, kd, vd, nbias, lamv, diff_subln_w[L][None])


def _moe_block(om, od, x2, w_o, ffn_norm_w, router_w, router_b, w_gate_up, b_gate_up, w_down, b_down):
    L = 0
    T, D = x2.shape
    wo_b = w_o[L].astype(BF16)
    rw_pad = _pad_lanes(router_w[L])
    rb_pad = _pad_lanes(router_b[L][None, :])

    x1, h, route, route_t, cnt = _router(om, od, x2, wo_b, ffn_norm_w[L][None], rw_pad, rb_pad)

    tm = MOE_TM
    n_sb = -(-T * TOP_K // tm) + N_EXPERTS
    idx = route_t[0:TOP_K].astype(jnp.int32)
    rank = route_t[2 * TOP_K:3 * TOP_K].astype(jnp.int32)
    counts = cnt[0, :N_EXPERTS].astype(jnp.int32)
    padded = (counts + tm - 1) // tm * tm
    pad_ends = jnp.cumsum(padded)
    pad_starts = pad_ends - padded
    dest = rank
    for e in range(N_EXPERTS):
        dest = dest + jnp.where(idx == e, pad_starts[e], 0)

    def dest_blocks(rows):
        return dest.reshape(TOP_K, T // rows, rows).transpose(1, 0, 2).reshape(T // rows, 1, TOP_K * rows)
    sb_start = jnp.arange(n_sb, dtype=jnp.int32) * tm
    sb_e = jnp.minimum(jnp.sum((pad_ends[None, :] <= sb_start[:, None]).astype(jnp.int32), axis=1), N_EXPERTS - 1)
    sb_valid = jnp.clip(pad_starts[sb_e] + counts[sb_e] - sb_start, 0, tm)
    sb_valid = jnp.where(sb_start < pad_ends[-1], sb_valid, 0)
    sb_nsub = (sb_valid + SUB - 1) // SUB
    owns = (counts > 0).astype(jnp.int32)
    n_used = jnp.stack([pad_ends[-1] // tm, jnp.sum(owns)]).astype(jnp.int32)
    nxt_sb = pad_ends[sb_e] // tm
    sb_next_e = jnp.where(nxt_sb < n_used[0], sb_e[jnp.minimum(nxt_sb, n_sb - 1)], -1).astype(jnp.int32)
    sb_ord = (jnp.cumsum(owns) - 1)[sb_e].astype(jnp.int32)

    xs = _dispatch(sb_valid.astype(jnp.int32), dest_blocks(DSP_TM), h, n_sb * tm)
    act = _expert_call(_expert_a_kernel, "expert_a", sb_e, sb_nsub, sb_next_e, sb_ord, n_used, xs, w_gate_up[L],
                       b_gate_up[L][:, None, :], D_FF, BF16, MOE_TN // 2)
    y = _expert_call(_expert_b_kernel, "expert_b", sb_e, sb_nsub, sb_next_e, sb_ord, n_used, act, w_down[L],
                     b_down[L][:, None, :], D // 2, jnp.uint32, MOE_TN // 2)
    return _combine(dest_blocks(CMB_TM), y, x1, route)


def kernel(x, attn_norm_w, w_in, q_a_norm_w, w_uq, kv_a_norm_w, w_ukv, mla_q_norm_w, mla_k_norm_w, diff_q_norm_w, diff_k_norm_w, lambda_q1, lambda_k1, lambda_q2, lambda_k2, diff_subln_w, w_o, ffn_norm_w, router_w, router_b, w_gate_up, b_gate_up, w_down, b_down, rel_bias):
    B, S, D = x.shape
    x2 = x.reshape(B * S, D)
    om, od = _token_mixers(x2, B, S, attn_norm_w, w_in, q_a_norm_w, w_uq, kv_a_norm_w, w_ukv, mla_q_norm_w,
                           mla_k_norm_w, diff_q_norm_w, diff_k_norm_w, lambda_q1, lambda_k1, lambda_q2, lambda_k2,
                           diff_subln_w, rel_bias)
    out = _moe_block(om, od, x2, w_o, ffn_norm_w, router_w, router_b, w_gate_up, b_gate_up, w_down, b_down)
    return out.reshape(B, S, D)
```
